```python
import math
import jax
import jax.numpy as jnp
from jax import lax
import numpy as np

D_MODEL = 4096
BATCH = 4
SEQ = 4096
DEPTH = 4

CHUNK = 64
Q_BLOCK = 128
HEAD_DIM = 128
N_BRANCHES = 4
BRANCH_WIDTH = D_MODEL // N_BRANCHES
N_HEADS_FOX = BRANCH_WIDTH // HEAD_DIM
N_HEADS_SB = BRANCH_WIDTH // HEAD_DIM
N_HEADS_CH = BRANCH_WIDTH // HEAD_DIM
N_HEADS_DIFF = BRANCH_WIDTH // (2 * HEAD_DIM)
CH_LEFT_CHUNKS = 8
CH_BAND = (CH_LEFT_CHUNKS + 1) * CHUNK
REL_CLIP = 128
ROPE_THETA = 10000.0
N_EXPERTS = 16
N_GROUPS = 4
EXPERTS_PER_GROUP = N_EXPERTS // N_GROUPS
TOP_K = 2
D_EXPERT = 384
RMS_EPS = 1e-6
QKV_COLS = 3 * BRANCH_WIDTH
GATE_COLS = N_BRANCHES * D_MODEL
SPLIT_FOX_F = QKV_COLS
SPLIT_SB = SPLIT_FOX_F + N_HEADS_FOX
SPLIT_CH = SPLIT_SB + QKV_COLS
SPLIT_DIFF = SPLIT_CH + QKV_COLS
SPLIT_GATE = SPLIT_DIFF + QKV_COLS
IN_COLS = SPLIT_GATE + GATE_COLS

kernel_name = 'hybrid_fox_sb_chunkrel_diff_moe'


def _rmsnorm(t, gain):
    t32 = t.astype(jnp.float32)
    y = t32 * lax.rsqrt(jnp.mean(t32 * t32, axis=-1, keepdims=True) + RMS_EPS)
    return (y * gain.astype(jnp.float32)).astype(t.dtype)


def _split_heads(t, n_heads):
    b, s, _ = t.shape
    return t.reshape(b, s, n_heads, -1).transpose(0, 2, 1, 3)


def _merge_heads(t):
    b, h, s, d = t.shape
    return t.transpose(0, 2, 1, 3).reshape(b, s, h * d)


def _to_blocks(t, size):
    b, h, s, d = t.shape
    return t.reshape(b, h, s // size, size, d).transpose(2, 0, 1, 3, 4)


def _from_blocks(t):
    n, b, h, size, d = t.shape
    return t.transpose(1, 2, 0, 3, 4).reshape(b, h, n * size, d)


def _rope(t):
    s_len, d = t.shape[-2], t.shape[-1]
    half = d // 2
    inv_freq = jnp.power(ROPE_THETA, -jnp.arange(half, dtype=jnp.float32) / half)
    ang = jnp.arange(s_len, dtype=jnp.float32)[:, None] * inv_freq[None, :]
    cos, sin = jnp.cos(ang), jnp.sin(ang)
    t32 = t.astype(jnp.float32)
    t1, t2 = t32[..., :half], t32[..., half:]
    return jnp.concatenate([t1 * cos - t2 * sin, t2 * cos + t1 * sin], axis=-1).astype(t.dtype)


def _forgetting_attention(q, k, v, log_f):
    s_len, d = q.shape[-2], q.shape[-1]
    cum = jnp.cumsum(log_f, axis=-1)
    cum_blocks = _to_blocks(cum[..., None], Q_BLOCK)[..., 0]
    k_pos = jnp.arange(s_len)
    scale = d ** -0.5

    def one_block(args):
        q_blk, cum_blk, blk = args
        q_pos = blk * Q_BLOCK + jnp.arange(Q_BLOCK)
        logits = jnp.einsum('bhqd,bhkd->bhqk', q_blk, k).astype(jnp.float32) * scale
        logits = logits + cum_blk[..., :, None] - cum[..., None, :]
        logits = jnp.where(k_pos[None, :] <= q_pos[:, None], logits, -jnp.inf)
        p = jax.nn.softmax(logits, axis=-1)
        return jnp.einsum('bhqk,bhkd->bhqd', p.astype(v.dtype), v)

    n_blocks = s_len // Q_BLOCK
    out = lax.map(one_block, (_to_blocks(q, Q_BLOCK), cum_blocks, jnp.arange(n_blocks)))
    return _from_blocks(out)


def _stick_breaking_attention(q, k, v):
    s_len, d = q.shape[-2], q.shape[-1]
    k_pos = jnp.arange(s_len)
    scale = d ** -0.5

    def one_block(args):
        q_blk, blk = args
        q_pos = blk * Q_BLOCK + jnp.arange(Q_BLOCK)
        z = jnp.einsum('bhqd,bhkd->bhqk', q_blk, k).astype(jnp.float32) * scale
        strict = k_pos[None, :] < q_pos[:, None]
        log_rest = jnp.where(strict, jax.nn.log_sigmoid(-z), 0.0)
        between = lax.cumsum(log_rest, axis=3, reverse=True) - log_rest
        w = jnp.where(strict, jnp.exp(jax.nn.log_sigmoid(z) + between), 0.0)
        return jnp.einsum('bhqk,bhkd->bhqd', w.astype(v.dtype), v)

    n_blocks = s_len // Q_BLOCK
    out = lax.map(one_block, (_to_blocks(q, Q_BLOCK), jnp.arange(n_blocks)))
    return _from_blocks(out)


def _chunk_band_attention(q, k, v, rel_bias):
    s_len, d = q.shape[-2], q.shape[-1]
    pad = CH_LEFT_CHUNKS * CHUNK
    k_pad = jnp.pad(k, ((0, 0), (0, 0), (pad, 0), (0, 0)))
    v_pad = jnp.pad(v, ((0, 0), (0, 0), (pad, 0), (0, 0)))
    rel = (jnp.arange(CHUNK)[:, None] + pad) - jnp.arange(CH_BAND)[None, :]
    bias = rel_bias[:, jnp.clip(rel, -REL_CLIP, REL_CLIP) + REL_CLIP].astype(jnp.float32)
    band_chunk = jnp.arange(CH_BAND) // CHUNK
    scale = d ** -0.5

    def one_chunk(args):
        q_c, c = args
        k_band = lax.dynamic_slice_in_dim(k_pad, c * CHUNK, CH_BAND, axis=2)
        v_band = lax.dynamic_slice_in_dim(v_pad, c * CHUNK, CH_BAND, axis=2)
        logits = jnp.einsum('bhqd,bhkd->bhqk', q_c, k_band).astype(jnp.float32) * scale + bias
        logits = jnp.where((band_chunk + c >= CH_LEFT_CHUNKS)[None, :], logits, -jnp.inf)
        p = jax.nn.softmax(logits, axis=-1)
        return jnp.einsum('bhqk,bhkd->bhqd', p.astype(v.dtype), v_band)

    n_chunks = s_len // CHUNK
    out = lax.map(one_chunk, (_to_blocks(q, CHUNK), jnp.arange(n_chunks)))
    return _from_blocks(out)


def _differential_attention(q1, q2, k1, k2, v, lam):
    s_len, d = q1.shape[-2], q1.shape[-1]
    k_chunk = jnp.arange(s_len) // CHUNK
    scale = d ** -0.5

    def one_block(args):
        q1_blk, q2_blk, blk = args
        q_chunk = (blk * Q_BLOCK + jnp.arange(Q_BLOCK)) // CHUNK
        visible = k_chunk[None, :] <= q_chunk[:, None]

        def probs(q_blk, kk):
            logits = jnp.einsum('bhqd,bhkd->bhqk', q_blk, kk).astype(jnp.float32) * scale
            return jax.nn.softmax(jnp.where(visible, logits, -jnp.inf), axis=-1)

        p = probs(q1_blk, k1) - lam * probs(q2_blk, k2)
        return jnp.einsum('bhqk,bhkd->bhqd', p.astype(v.dtype), v)

    n_blocks = s_len // Q_BLOCK
    out = lax.map(one_block, (_to_blocks(q1, Q_BLOCK), _to_blocks(q2, Q_BLOCK), jnp.arange(n_blocks)))
    return _from_blocks(out)


def _moe(h, router_w, router_bias, w_gate, w_up, w_down):
    b, s_len, d = h.shape
    t = h.reshape(-1, d)
    scores = jax.nn.sigmoid((t @ router_w).astype(jnp.float32))
    biased = scores + router_bias.astype(jnp.float32)
    grouped = biased.reshape(-1, N_GROUPS, EXPERTS_PER_GROUP)
    group_score = jnp.sum(lax.top_k(grouped, TOP_K)[0], axis=-1)
    _, g_idx = lax.top_k(group_score, 1)
    group_mask = g_idx == jnp.arange(N_GROUPS)[None, :]
    expert_mask = jnp.repeat(group_mask, EXPERTS_PER_GROUP, axis=-1)
    _, e_idx = lax.top_k(jnp.where(expert_mask, biased, -jnp.inf), TOP_K)
    w = jnp.take_along_axis(scores, e_idx, axis=-1)
    w = w / jnp.sum(w, axis=-1, keepdims=True)
    hit = e_idx[..., None] == jnp.arange(N_EXPERTS)[None, None, :]
    gate = jnp.sum(jnp.where(hit, w[..., None], 0.0), axis=1)
    hg = jnp.einsum('nd,edf->nef', t, w_gate)
    hu = jnp.einsum('nd,edf->nef', t, w_up)
    act = jax.nn.silu(hg) * hu * gate[:, :, None].astype(t.dtype)
    out = jnp.einsum('nef,efd->nd', act, w_down)
    return out.reshape(b, s_len, d)


def setup_inputs(seed: int = 0) -> dict:
    key = jax.random.key(seed)
    ks = jax.random.split(key, 21)
    f32 = jnp.float32

    def nrm(k, shape, scale):
        return jax.random.normal(k, shape, f32) * scale

    def gain(k, shape):
        return 1.0 + 0.02 * jax.random.normal(k, shape, f32)

    res_scale = (2 * DEPTH) ** -0.5
    return {
        'x': nrm(ks[0], (BATCH, SEQ, D_MODEL), 1.0),
        'attn_norm': gain(ks[1], (DEPTH, D_MODEL)),
        'w_in': nrm(ks[2], (DEPTH, D_MODEL, IN_COLS), D_MODEL ** -0.5),
        'fox_forget_bias': jax.random.uniform(ks[3], (DEPTH, N_HEADS_FOX), f32, 1.0, 6.0),
        'fox_q_norm': gain(ks[4], (DEPTH, HEAD_DIM)),
        'fox_k_norm': gain(ks[5], (DEPTH, HEAD_DIM)),
        'ch_q_norm': gain(ks[6], (DEPTH, HEAD_DIM)),
        'ch_k_norm': gain(ks[7], (DEPTH, HEAD_DIM)),
        'ch_rel_bias': nrm(ks[8], (DEPTH, N_HEADS_CH, 2 * REL_CLIP + 1), 0.1),
        'diff_q_norm': gain(ks[9], (DEPTH, HEAD_DIM)),
        'diff_k_norm': gain(ks[10], (DEPTH, HEAD_DIM)),
        'diff_lambda': nrm(ks[11], (DEPTH, 4, HEAD_DIM), 0.1),
        'diff_subln': gain(ks[12], (DEPTH, 2 * HEAD_DIM)),
        'w_branch': nrm(ks[13], (DEPTH, N_BRANCHES, BRANCH_WIDTH, D_MODEL), BRANCH_WIDTH ** -0.5),
        'w_out': nrm(ks[14], (DEPTH, D_MODEL, D_MODEL), D_MODEL ** -0.5 * res_scale),
        'ffn_norm': gain(ks[15], (DEPTH, D_MODEL)),
        'router_w': nrm(ks[16], (D_MODEL, N_EXPERTS), D_MODEL ** -0.5),
        'router_bias': nrm(ks[17], (N_EXPERTS,), 0.01),
        'w_gate': nrm(ks[18], (DEPTH, N_EXPERTS, D_MODEL, D_EXPERT), D_MODEL ** -0.5),
        'w_up': nrm(ks[19], (DEPTH, N_EXPERTS, D_MODEL, D_EXPERT), D_MODEL ** -0.5),
        'w_down': nrm(ks[20], (DEPTH, N_EXPERTS, D_EXPERT, D_MODEL), D_EXPERT ** -0.5 * res_scale),
    }


def reference(x, attn_norm, w_in, fox_forget_bias, fox_q_norm, fox_k_norm,
              ch_q_norm, ch_k_norm, ch_rel_bias, diff_q_norm, diff_k_norm,
              diff_lambda, diff_subln, w_branch, w_out, ffn_norm,
              router_w, router_bias, w_gate, w_up, w_down):
    b, s_len, _ = x.shape
    for layer in range(DEPTH):
        h = _rmsnorm(x, attn_norm[layer])
        proj = h @ w_in[layer]
        fox_qkv, fox_f, sb_qkv, ch_qkv, diff_qkv, gate_pre = jnp.split(
            proj, [SPLIT_FOX_F, SPLIT_SB, SPLIT_CH, SPLIT_DIFF, SPLIT_GATE], axis=-1)

        q, k, v = (_split_heads(t, N_HEADS_FOX) for t in jnp.split(fox_qkv, 3, axis=-1))
        q = _rmsnorm(q, fox_q_norm[layer])
        k = _rmsnorm(k, fox_k_norm[layer])
        log_f = jax.nn.log_sigmoid(fox_f.astype(jnp.float32)
                                   + fox_forget_bias[layer].astype(jnp.float32)).transpose(0, 2, 1)
        out_fox = _merge_heads(_forgetting_attention(q, k, v, log_f))

        q, k, v = (_split_heads(t, N_HEADS_SB) for t in jnp.split(sb_qkv, 3, axis=-1))
        out_sb = _merge_heads(_stick_breaking_attention(q, k, v))

        q, k, v = (_split_heads(t, N_HEADS_CH) for t in jnp.split(ch_qkv, 3, axis=-1))
        q = _rmsnorm(q, ch_q_norm[layer])
        k = _rmsnorm(k, ch_k_norm[layer])
        out_ch = _merge_heads(_chunk_band_attention(q, k, v, ch_rel_bias[layer]))

        dq, dk, dv = jnp.split(diff_qkv, 3, axis=-1)
        dq = dq.reshape(b, s_len, N_HEADS_DIFF, 2, HEAD_DIM).transpose(3, 0, 2, 1, 4)
        dk = dk.reshape(b, s_len, N_HEADS_DIFF, 2, HEAD_DIM).transpose(3, 0, 2, 1, 4)
        dq = _rope(_rmsnorm(dq, diff_q_norm[layer]))
        dk = _rope(_rmsnorm(dk, diff_k_norm[layer]))
        dv = _split_heads(dv, N_HEADS_DIFF)
        lam_init = 0.8 - 0.6 * math.exp(-0.3 * layer)
        lam_vecs = diff_lambda[layer].astype(jnp.float32)
        lam = (jnp.exp(jnp.sum(lam_vecs[0] * lam_vecs[1]))
               - jnp.exp(jnp.sum(lam_vecs[2] * lam_vecs[3])) + lam_init)
        o_diff = _differential_attention(dq[0], dq[1], dk[0], dk[1], dv, lam)
        out_diff = _merge_heads(_rmsnorm(o_diff, diff_subln[layer]) * (1.0 - lam_init))

        gates = jax.nn.sigmoid(gate_pre.reshape(b, s_len, N_BRANCHES, D_MODEL))
        branches = (out_fox, out_sb, out_ch, out_diff)
        merged = gates[:, :, 0] * (branches[0] @ w_branch[layer, 0])
        for i in range(1, N_BRANCHES):
            merged = merged + gates[:, :, i] * (branches[i] @ w_branch[layer, i])
        x = x + merged @ w_out[layer]

        x = x + _moe(_rmsnorm(x, ffn_norm[layer]), router_w, router_bias,
                     w_gate[layer], w_up[layer], w_down[layer])
    return x
```

```python
import functools
import math

import jax
import jax.numpy as jnp
from jax import lax
from jax.experimental import pallas as pl
from jax.experimental.pallas import tpu as pltpu

HEAD_DIM = 128
LANES = 128
CHUNK = 64
N_BRANCHES = 4
CH_LEFT_CHUNKS = 8
REL_CLIP = 128
ROPE_THETA = 10000.0
N_GROUPS = 4
RMS_EPS = 1e-6
NEG_INF = float("-inf")
VMEM_LIMIT_BYTES = 56 * 1024 * 1024

F32 = jnp.float32
BF16 = jnp.bfloat16


def _tile(pref, n):
    t = min(pref, n)
    while n % t:
        t //= 2
    return t


def _params(*sem):
    return pltpu.CompilerParams(dimension_semantics=sem, vmem_limit_bytes=VMEM_LIMIT_BYTES)


def _dot_nt(a, b):
    return lax.dot_general(a, b, (((1,), (1,)), ((), ())), preferred_element_type=F32)


def _log_sigmoid(x):
    return jnp.minimum(x, 0.0) - jnp.log1p(jnp.exp(-jnp.abs(x)))


def _rmsnorm_kernel(x_ref, g_ref, o_ref):
    x = x_ref[...]
    ms = jnp.mean(x * x, axis=-1, keepdims=True)
    o_ref[...] = (x * lax.rsqrt(ms + RMS_EPS) * g_ref[...]).astype(o_ref.dtype)


def _rmsnorm(x, gain):
    n, d = x.shape
    tm = _tile(256, n)
    return pl.pallas_call(
        _rmsnorm_kernel,
        grid=(n // tm,),
        in_specs=[pl.BlockSpec((tm, d), lambda i: (i, 0)), pl.BlockSpec((1, d), lambda i: (0, 0))],
        out_specs=pl.BlockSpec((tm, d), lambda i: (i, 0)),
        out_shape=jax.ShapeDtypeStruct((n, d), BF16),
        compiler_params=_params("parallel"),
        name="attn_rmsnorm",
    )(x, gain.reshape(1, d))


def _matmul(a, w, epilogue, out_dtype, extras=(), extra_specs=(), tm=1024, tn=1024, name="matmul"):
    m, k = a.shape
    _, n = w.shape
    tm, tn = _tile(tm, m), _tile(tn, n)

    def kern(a_ref, w_ref, *refs):
        acc = jnp.dot(a_ref[...], w_ref[...], preferred_element_type=F32)
        epilogue(acc, refs[:-1], refs[-1])

    return pl.pallas_call(
        kern,
        grid=(m // tm, n // tn),
        in_specs=[pl.BlockSpec((tm, k), lambda i, j: (i, 0)), pl.BlockSpec((k, tn), lambda i, j: (0, j)),
                  *[mk(tm, tn) for mk in extra_specs]],
        out_specs=pl.BlockSpec((tm, tn), lambda i, j: (i, j)),
        out_shape=jax.ShapeDtypeStruct((m, n), out_dtype),
        compiler_params=_params("parallel", "arbitrary"),
        name=name,
    )(a, w, *extras)


def _epi_sigmoid(acc, _, o_ref):
    o_ref[...] = jax.nn.sigmoid(acc).astype(o_ref.dtype)


def _epi_residual(acc, refs, o_ref):
    o_ref[...] = refs[0][...] + acc


PLAIN, NORM, NORM_ROPE = 0, 1, 2
SECTION_KINDS = (NORM, NORM, PLAIN, PLAIN, PLAIN, PLAIN, NORM, NORM, PLAIN, NORM_ROPE, NORM_ROPE, PLAIN)


def _qkv_epilogue(acc, refs, o_ref, *, sec_tiles):
    gain_ref, cos_ref, sin_ref = refs
    sec = pl.program_id(1) // sec_tiles
    g = gain_ref[pl.ds(sec, 1), :]
    tn = acc.shape[1]

    def run(kind):
        for hh in range(tn // HEAD_DIM):
            y = acc[:, hh * HEAD_DIM:(hh + 1) * HEAD_DIM]
            if kind != PLAIN:
                y = y * lax.rsqrt(jnp.mean(y * y, axis=-1, keepdims=True) + RMS_EPS)
            y = y * g
            if kind == NORM_ROPE:
                y = y * cos_ref[...] + pltpu.roll(y, HEAD_DIM // 2, 1) * sin_ref[...]
            o_ref[:, hh * HEAD_DIM:(hh + 1) * HEAD_DIM] = y.astype(o_ref.dtype)

    for kind in (PLAIN, NORM, NORM_ROPE):
        hit = functools.reduce(jnp.logical_or, [sec == s for s, kd in enumerate(SECTION_KINDS) if kd == kind])
        pl.when(hit)(functools.partial(run, kind))


def _qkv_proj(h, w_qkv, gains, cos_t, sin_t, seq, width):
    n = h.shape[0]
    tm = _tile(1024, seq)
    tn = _tile(1024, width)
    seq_tiles = seq // tm
    row_spec = lambda tm_, tn_: pl.BlockSpec((tm_, LANES), lambda i, j: (i % seq_tiles, 0))
    gain_spec = lambda tm_, tn_: pl.BlockSpec(gains.shape, lambda i, j: (0, 0))
    return _matmul(h, w_qkv, functools.partial(_qkv_epilogue, sec_tiles=width // tn), BF16,
                   extras=(gains, cos_t, sin_t), extra_specs=(gain_spec, row_spec, row_spec),
                   tm=tm, tn=tn, name="qkv_proj")


def _split3(x):
    hi = x.astype(BF16)
    r = x - hi.astype(F32)
    mid = r.astype(BF16)
    lo = (r - mid.astype(F32)).astype(BF16)
    return hi, mid, lo


def _fgate_kernel(h_ref, w_ref, b_ref, tri_ref, o_ref, carry_sc):
    @pl.when(pl.program_id(1) == 0)
    def _():
        carry_sc[...] = jnp.zeros_like(carry_sc)

    f = jnp.dot(h_ref[...], w_ref[...], preferred_element_type=F32) + b_ref[...]
    lf = _log_sigmoid(f)
    tri = tri_ref[...]
    c = carry_sc[...]
    for part in _split3(lf):
        c = c + jnp.dot(tri, part, preferred_element_type=F32)
    o_ref[...] = c
    carry_sc[...] = c[-1:, :]


def _forget_cumsum(h, w_f, bias, batch, seq):
    n, d = h.shape
    ts = _tile(512, seq)
    nt = seq // ts
    tri = (jnp.arange(ts)[:, None] >= jnp.arange(ts)[None, :]).astype(BF16)
    return pl.pallas_call(
        _fgate_kernel,
        grid=(batch, nt),
        in_specs=[pl.BlockSpec((ts, d), lambda b, t: (b * nt + t, 0)),
                  pl.BlockSpec((d, LANES), lambda b, t: (0, 0)),
                  pl.BlockSpec((1, LANES), lambda b, t: (0, 0)),
                  pl.BlockSpec((ts, ts), lambda b, t: (0, 0))],
        out_specs=pl.BlockSpec((ts, LANES), lambda b, t: (b * nt + t, 0)),
        out_shape=jax.ShapeDtypeStruct((n, LANES), F32),
        scratch_shapes=[pltpu.VMEM((1, LANES), F32)],
        compiler_params=_params("parallel", "arbitrary"),
        name="forget_cumsum",
    )(h, w_f, bias, tri)


def _softmax_step(s, v, m_sc, l_sc, acc_sc):
    m_prev = m_sc[...]
    m_new = jnp.maximum(m_prev, jnp.max(s, axis=1, keepdims=True))
    alpha = jnp.exp(m_prev - m_new)
    p = jnp.exp(s - m_new)
    l_sc[...] = alpha * l_sc[...] + jnp.sum(p, axis=1, keepdims=True)
    acc_sc[...] = alpha * acc_sc[...] + jnp.dot(p.astype(v.dtype), v, preferred_element_type=F32)
    m_sc[...] = m_new


def _softmax_init(m_sc, l_sc, acc_sc):
    m_sc[...] = jnp.full_like(m_sc, NEG_INF)
    l_sc[...] = jnp.zeros_like(l_sc)
    acc_sc[...] = jnp.zeros_like(acc_sc)


def _attn_specs(seq, tq, width, q_sec, k_sec, v_sec, v_mult=1, q_mult=1):
    nq = seq // tq
    sec = width // HEAD_DIM
    q_spec = lambda c=0: pl.BlockSpec((tq, HEAD_DIM), lambda b, h, i: (b * nq + i, q_sec * sec + h * q_mult + c))
    k_spec = lambda c=0: pl.BlockSpec((seq, HEAD_DIM), lambda b, h, i: (b, k_sec * sec + h * q_mult + c))
    v_spec = pl.BlockSpec((seq, HEAD_DIM * v_mult), lambda b, h, i: (b, v_sec * sec // v_mult + h))
    return q_spec, k_spec, v_spec


def _fox_kernel(q_ref, k_ref, v_ref, ct_ref, cs_ref, o_ref, m_sc, l_sc, acc_sc, *, t):
    h, qi = pl.program_id(1), pl.program_id(2)
    q = q_ref[...]
    lane = lax.broadcasted_iota(jnp.int32, ct_ref.shape, 1)
    ct = jnp.sum(jnp.where(lane == h, ct_ref[...], 0.0), axis=1, keepdims=True)
    _softmax_init(m_sc, l_sc, acc_sc)

    def step(ki, causal):
        ks = pl.multiple_of(ki * t, t)
        s = _dot_nt(q, k_ref[pl.ds(ks, t), :])
        s = s + ct - cs_ref[pl.ds(ki, 1), :]
        if causal:
            row = lax.broadcasted_iota(jnp.int32, (t, t), 0)
            col = lax.broadcasted_iota(jnp.int32, (t, t), 1)
            s = jnp.where(col <= row, s, NEG_INF)
        _softmax_step(s, v_ref[pl.ds(ks, t), :], m_sc, l_sc, acc_sc)

    lax.fori_loop(0, qi, lambda ki, c: (step(ki, False), c)[1], 0)
    step(qi, True)
    o_ref[...] = (acc_sc[...] * (1.0 / l_sc[...])).astype(o_ref.dtype)


def _fox_attention(qkv, cum, cum_t, batch, seq, width):
    n = qkv.shape[0]
    heads = width // HEAD_DIM
    t = _tile(512, seq)
    nq = seq // t
    q_spec, k_spec, v_spec = _attn_specs(seq, t, width, 0, 1, 2)
    return pl.pallas_call(
        functools.partial(_fox_kernel, t=t),
        grid=(batch, heads, nq),
        in_specs=[q_spec(), k_spec(), v_spec,
                  pl.BlockSpec((t, LANES), lambda b, h, i: (b * nq + i, 0)),
                  pl.BlockSpec((None, nq, t), lambda b, h, i: (h, b, 0))],
        out_specs=pl.BlockSpec((t, HEAD_DIM), lambda b, h, i: (b * nq + i, h)),
        out_shape=jax.ShapeDtypeStruct((n, width), BF16),
        scratch_shapes=[pltpu.VMEM((t, 1), F32), pltpu.VMEM((t, 1), F32), pltpu.VMEM((t, HEAD_DIM), F32)],
        compiler_params=_params("parallel", "parallel", "arbitrary"),
        name="fox_attention",
    )(qkv, qkv, qkv, cum, cum_t)


def _sb_kernel(q_ref, k_ref, v_ref, u_ref, o_ref, c_sc, acc_sc, *, t):
    qi = pl.program_id(2)
    q = q_ref[...]
    u = u_ref[...]
    c_sc[...] = jnp.zeros_like(c_sc)
    acc_sc[...] = jnp.zeros_like(acc_sc)

    def step(ki, diagonal):
        ks = pl.multiple_of(ki * t, t)
        z = _dot_nt(q, k_ref[pl.ds(ks, t), :])
        softplus = jnp.log1p(jnp.exp(-jnp.abs(z)))
        log_beta = jnp.minimum(z, 0.0) - softplus
        log_rest = jnp.minimum(-z, 0.0) - softplus
        if diagonal:
            row = lax.broadcasted_iota(jnp.int32, (t, t), 0)
            col = lax.broadcasted_iota(jnp.int32, (t, t), 1)
            strict = col < row
            log_rest = jnp.where(strict, log_rest, 0.0)
        hi = log_rest.astype(BF16)
        lo = (log_rest - hi.astype(F32)).astype(BF16)
        between = (jnp.dot(hi, u, preferred_element_type=F32) + jnp.dot(lo, u, preferred_element_type=F32)
                   + c_sc[...])
        w = jnp.exp(log_beta + between)
        if diagonal:
            w = jnp.where(strict, w, 0.0)
        v = v_ref[pl.ds(ks, t), :]
        acc_sc[...] += jnp.dot(w.astype(v.dtype), v, preferred_element_type=F32)
        c_sc[...] += jnp.sum(log_rest, axis=1, keepdims=True)

    step(qi, True)
    lax.fori_loop(0, qi, lambda kk, c: (step(qi - 1 - kk, False), c)[1], 0)
    o_ref[...] = acc_sc[...].astype(o_ref.dtype)


def _sb_attention(qkv, batch, seq, width):
    n = qkv.shape[0]
    heads = width // HEAD_DIM
    t = _tile(512, seq)
    nq = seq // t
    q_spec, k_spec, v_spec = _attn_specs(seq, t, width, 3, 4, 5)
    upper = (jnp.arange(t)[:, None] > jnp.arange(t)[None, :]).astype(BF16)
    return pl.pallas_call(
        functools.partial(_sb_kernel, t=t),
        grid=(batch, heads, nq),
        in_specs=[q_spec(), k_spec(), v_spec, pl.BlockSpec((t, t), lambda b, h, i: (0, 0))],
        out_specs=pl.BlockSpec((t, HEAD_DIM), lambda b, h, i: (b * nq + i, h)),
        out_shape=jax.ShapeDtypeStruct((n, width), BF16),
        scratch_shapes=[pltpu.VMEM((t, 1), F32), pltpu.VMEM((t, HEAD_DIM), F32)],
        compiler_params=_params("parallel", "parallel", "arbitrary"),
        name="stick_breaking_attention",
    )(qkv, qkv, qkv, upper)


def _band_kernel(q_ref, k_ref, v_ref, bias_ref, o_ref, m_sc, l_sc, acc_sc, *, t, n_back):
    qi = pl.program_id(2)
    q = q_ref[...]
    _softmax_init(m_sc, l_sc, acc_sc)

    def step(d, c):
        ks = pl.multiple_of((qi - d) * t, t)
        s = _dot_nt(q, k_ref[pl.ds(ks, t), :]) + bias_ref[d]
        _softmax_step(s, v_ref[pl.ds(ks, t), :], m_sc, l_sc, acc_sc)
        return c

    lax.fori_loop(0, jnp.minimum(qi, n_back) + 1, step, 0)
    o_ref[...] = (acc_sc[...] * (1.0 / l_sc[...])).astype(o_ref.dtype)


def _band_bias(rel_bias, t, n_back):
    d = jnp.arange(n_back + 1)[:, None, None]
    i = jnp.arange(t)[None, :, None]
    j = jnp.arange(t)[None, None, :]
    rel = i - j + d * t
    bias = rel_bias[:, jnp.clip(rel, -REL_CLIP, REL_CLIP) + REL_CLIP].astype(F32)
    chunk_gap = (i + d * t) // CHUNK - j // CHUNK
    visible = (chunk_gap >= 0) & (chunk_gap <= CH_LEFT_CHUNKS)
    return jnp.where(visible[None], bias, NEG_INF)


def _band_attention(qkv, rel_bias, batch, seq, width):
    n = qkv.shape[0]
    heads = width // HEAD_DIM
    t = _tile(256, seq)
    nq = seq // t
    n_back = -(-(CH_LEFT_CHUNKS * CHUNK) // t)
    bias = _band_bias(rel_bias, t, n_back)
    q_spec, k_spec, v_spec = _attn_specs(seq, t, width, 6, 7, 8)
    return pl.pallas_call(
        functools.partial(_band_kernel, t=t, n_back=n_back),
        grid=(batch, heads, nq),
        in_specs=[q_spec(), k_spec(), v_spec,
                  pl.BlockSpec((None, n_back + 1, t, t), lambda b, h, i: (h, 0, 0, 0))],
        out_specs=pl.BlockSpec((t, HEAD_DIM), lambda b, h, i: (b * nq + i, h)),
        out_shape=jax.ShapeDtypeStruct((n, width), BF16),
        scratch_shapes=[pltpu.VMEM((t, 1), F32), pltpu.VMEM((t, 1), F32), pltpu.VMEM((t, HEAD_DIM), F32)],
        compiler_params=_params("parallel", "parallel", "arbitrary"),
        name="chunk_band_attention",
    )(qkv, qkv, qkv, bias)


def _diff_kernel(q1_ref, q2_ref, k1_ref, k2_ref, v_ref, lam_ref, g_ref, o_ref,
                 m1_sc, l1_sc, a1_sc, m2_sc, l2_sc, a2_sc, *, t, lam_init):
    qi = pl.program_id(2)
    q1, q2 = q1_ref[...], q2_ref[...]
    _softmax_init(m1_sc, l1_sc, a1_sc)
    _softmax_init(m2_sc, l2_sc, a2_sc)

    def step(ki, diagonal):
        ks = pl.multiple_of(ki * t, t)
        v = v_ref[pl.ds(ks, t), :]
        s1 = _dot_nt(q1, k1_ref[pl.ds(ks, t), :])
        s2 = _dot_nt(q2, k2_ref[pl.ds(ks, t), :])
        if diagonal:
            row = lax.broadcasted_iota(jnp.int32, (t, t), 0) // CHUNK
            col = lax.broadcasted_iota(jnp.int32, (t, t), 1) // CHUNK
            s1 = jnp.where(col <= row, s1, NEG_INF)
            s2 = jnp.where(col <= row, s2, NEG_INF)
        _softmax_step(s1, v, m1_sc, l1_sc, a1_sc)
        _softmax_step(s2, v, m2_sc, l2_sc, a2_sc)

    lax.fori_loop(0, qi, lambda ki, c: (step(ki, False), c)[1], 0)
    step(qi, True)

    lv = lam_ref[...]
    lam = (jnp.exp(jnp.sum(lv[0:1] * lv[1:2], axis=1, keepdims=True))
           - jnp.exp(jnp.sum(lv[2:3] * lv[3:4], axis=1, keepdims=True)) + lam_init)
    o = a1_sc[...] * (1.0 / l1_sc[...]) - lam * (a2_sc[...] * (1.0 / l2_sc[...]))
    y = o * lax.rsqrt(jnp.mean(o * o, axis=-1, keepdims=True) + RMS_EPS) * g_ref[...]
    o_ref[...] = (y * (1.0 - lam_init)).astype(o_ref.dtype)


def _diff_attention(qkv, lam_vecs, subln, lam_init, batch, seq, width):
    n = qkv.shape[0]
    heads = width // (2 * HEAD_DIM)
    t = _tile(512, seq)
    nq = seq // t
    q_spec, k_spec, v_spec = _attn_specs(seq, t, width, 9, 10, 11, v_mult=2, q_mult=2)
    scratch = [pltpu.VMEM((t, 1), F32), pltpu.VMEM((t, 1), F32), pltpu.VMEM((t, 2 * HEAD_DIM), F32)]
    return pl.pallas_call(
        functools.partial(_diff_kernel, t=t, lam_init=lam_init),
        grid=(batch, heads, nq),
        in_specs=[q_spec(0), q_spec(1), k_spec(0), k_spec(1), v_spec,
                  pl.BlockSpec(lam_vecs.shape, lambda b, h, i: (0, 0)),
                  pl.BlockSpec((1, 2 * HEAD_DIM), lambda b, h, i: (0, 0))],
        out_specs=pl.BlockSpec((t, 2 * HEAD_DIM), lambda b, h, i: (b * nq + i, h)),
        out_shape=jax.ShapeDtypeStruct((n, width), BF16),
        scratch_shapes=scratch + scratch,
        compiler_params=_params("parallel", "parallel", "arbitrary"),
        name="differential_attention",
    )(qkv, qkv, qkv, qkv, qkv, lam_vecs, subln.reshape(1, -1))


def _merge_kernel(*refs):
    br, wb, gates, o_ref = refs[0:4], refs[4], refs[5:9], refs[9]
    acc = None
    for i in range(N_BRANCHES):
        term = gates[i][...].astype(F32) * jnp.dot(br[i][...], wb[i], preferred_element_type=F32)
        acc = term if acc is None else acc + term
    o_ref[...] = acc.astype(o_ref.dtype)


def _gated_merge(branches, w_branch, gates):
    n, width = branches[0].shape
    d = w_branch.shape[-1]
    tm, tn = _tile(1024, n), _tile(512, d)
    col_tiles = d // tn
    br_spec = pl.BlockSpec((tm, width), lambda i, j: (i, 0))
    gate_specs = [pl.BlockSpec((tm, tn), lambda i, j, g=g: (i, g * col_tiles + j)) for g in range(N_BRANCHES)]
    return pl.pallas_call(
        _merge_kernel,
        grid=(n // tm, col_tiles),
        in_specs=[br_spec] * N_BRANCHES + [pl.BlockSpec((N_BRANCHES, width, tn), lambda i, j: (0, 0, j))] + gate_specs,
        out_specs=pl.BlockSpec((tm, tn), lambda i, j: (i, j)),
        out_shape=jax.ShapeDtypeStruct((n, d), BF16),
        compiler_params=_params("parallel", "arbitrary"),
        name="gated_merge",
    )(*branches, w_branch, *([gates] * N_BRANCHES))


def _router_kernel(x_ref, g_ref, rw_ref, rb_ref, h_ref, gate_ref, *, n_experts):
    x = x_ref[...]
    hn = x * lax.rsqrt(jnp.mean(x * x, axis=-1, keepdims=True) + RMS_EPS) * g_ref[...]
    h_ref[...] = hn.astype(h_ref.dtype)
    logits = None
    for wi, wp in enumerate(_split3(rw_ref[...])):
        for hi, hp in enumerate(_split3(hn)):
            if wi + hi <= 2:
                term = _dot_nt(wp, hp)
                logits = term if logits is None else logits + term
    score = jax.nn.sigmoid(logits)
    biased = score + rb_ref[...]
    per = n_experts // N_GROUPS
    rows = [biased[e:e + 1, :] for e in range(n_experts)]
    group_score = []
    for g in range(N_GROUPS):
        r = rows[g * per:(g + 1) * per]
        top1 = functools.reduce(jnp.maximum, r)
        first = []
        taken = jnp.zeros_like(top1, dtype=jnp.bool_)
        for v in r:
            is1 = (v == top1) & ~taken
            taken = taken | is1
            first.append(is1)
        rest = [jnp.where(f, NEG_INF, v) for f, v in zip(first, r)]
        top2 = functools.reduce(jnp.maximum, rest)
        second = []
        taken = jnp.zeros_like(top1, dtype=jnp.bool_)
        for v in rest:
            is2 = (v == top2) & ~taken
            taken = taken | is2
            second.append(is2)
        group_score.append((top1 + top2, [a | b for a, b in zip(first, second)]))
    best = functools.reduce(jnp.maximum, [gs for gs, _ in group_score])
    taken = jnp.zeros_like(best, dtype=jnp.bool_)
    picked = []
    for gs, sel in group_score:
        chosen = (gs == best) & ~taken
        taken = taken | chosen
        picked += [s & chosen for s in sel]
    w = [jnp.where(p, score[e:e + 1, :], 0.0) for e, p in enumerate(picked)]
    total = functools.reduce(jnp.add, w)
    inv = 1.0 / total
    for e, we in enumerate(w):
        gate_ref[e:e + 1, :] = we * inv


def _router(x, gain, router_w, router_bias):
    n, d = x.shape
    e = router_w.shape[1]
    tm = _tile(256, n)
    return pl.pallas_call(
        functools.partial(_router_kernel, n_experts=e),
        grid=(n // tm,),
        in_specs=[pl.BlockSpec((tm, d), lambda i: (i, 0)), pl.BlockSpec((1, d), lambda i: (0, 0)),
                  pl.BlockSpec((e, d), lambda i: (0, 0)), pl.BlockSpec((e, 1), lambda i: (0, 0))],
        out_specs=[pl.BlockSpec((tm, d), lambda i: (i, 0)), pl.BlockSpec((e, tm), lambda i: (0, i))],
        out_shape=[jax.ShapeDtypeStruct((n, d), BF16), jax.ShapeDtypeStruct((e, n), F32)],
        compiler_params=_params("parallel"),
        name="ffn_norm_router",
    )(x, gain.reshape(1, d), router_w.T, router_bias.reshape(e, 1))


def _gate_up_kernel(h_ref, wg_ref, wu_ref, gate_ref, o_ref, *, d_expert, per_tile):
    h = h_ref[...]
    hg = jnp.dot(h, wg_ref[...], preferred_element_type=F32)
    hu = jnp.dot(h, wu_ref[...], preferred_element_type=F32)
    act = hg * jax.nn.sigmoid(hg) * hu
    gate = gate_ref[...]
    lane = lax.broadcasted_iota(jnp.int32, gate.shape, 1)
    first = pl.program_id(1) * per_tile
    for e in range(per_tile):
        ge = jnp.sum(jnp.where(lane == first + e, gate, 0.0), axis=1, keepdims=True)
        o_ref[:, e * d_expert:(e + 1) * d_expert] = (act[:, e * d_expert:(e + 1) * d_expert] * ge).astype(o_ref.dtype)


def _moe_gate_up(h, wg, wu, gate, d_expert):
    n, d = h.shape
    f = wg.shape[1]
    n_experts = f // d_expert
    per_tile = 2 if n_experts % 2 == 0 else 1
    tn = per_tile * d_expert
    tm = _tile(512, n)
    return pl.pallas_call(
        functools.partial(_gate_up_kernel, d_expert=d_expert, per_tile=per_tile),
        grid=(n // tm, f // tn),
        in_specs=[pl.BlockSpec((tm, d), lambda i, j: (i, 0)), pl.BlockSpec((d, tn), lambda i, j: (0, j)),
                  pl.BlockSpec((d, tn), lambda i, j: (0, j)), pl.BlockSpec((tm, n_experts), lambda i, j: (i, 0))],
        out_specs=pl.BlockSpec((tm, tn), lambda i, j: (i, j)),
        out_shape=jax.ShapeDtypeStruct((n, f), BF16),
        compiler_params=_params("parallel", "arbitrary"),
        name="moe_gate_up",
    )(h, wg, wu, gate)


def _rope_tables(seq):
    half = HEAD_DIM // 2
    inv_freq = jnp.power(ROPE_THETA, -jnp.arange(half, dtype=F32) / half)
    ang = jnp.arange(seq, dtype=F32)[:, None] * inv_freq[None, :]
    cos, sin = jnp.cos(ang), jnp.sin(ang)
    return jnp.concatenate([cos, cos], axis=-1), jnp.concatenate([-sin, sin], axis=-1)


def kernel(x, attn_norm, w_in, fox_forget_bias, fox_q_norm, fox_k_norm, ch_q_norm, ch_k_norm, ch_rel_bias,
           diff_q_norm, diff_k_norm, diff_lambda, diff_subln, w_branch, w_out, ffn_norm, router_w, router_bias,
           w_gate, w_up, w_down):
    batch, seq, d_model = x.shape
    depth = w_in.shape[0]
    width = d_model // N_BRANCHES
    fox_heads = width // HEAD_DIM
    qkv_cols = 3 * width
    n_experts, _, d_expert = w_gate.shape[1:]
    n = batch * seq
    scale = HEAD_DIM ** -0.5
    assert fox_heads <= LANES and width % (2 * HEAD_DIM) == 0 and seq % CHUNK == 0

    cos_t, sin_t = _rope_tables(seq)
    ones = jnp.ones((HEAD_DIM,), F32)
    res_spec = lambda tm, tn: pl.BlockSpec((tm, tn), lambda i, j: (i, j))
    xf = x.reshape(n, d_model)

    for layer in range(depth):
        wl = w_in[layer]
        s_f, s_sb, s_gate = qkv_cols, qkv_cols + fox_heads, 4 * qkv_cols + fox_heads
        w_qkv = jnp.concatenate([wl[:, :s_f], wl[:, s_sb:s_gate]], axis=1).astype(BF16)
        w_f = jnp.pad(wl[:, s_f:s_sb], ((0, 0), (0, LANES - fox_heads))).astype(BF16)
        w_g = wl[:, s_gate:].astype(BF16)
        f_bias = jnp.pad(fox_forget_bias[layer], (0, LANES - fox_heads)).reshape(1, LANES)
        gains = jnp.stack([
            fox_q_norm[layer] * scale, fox_k_norm[layer], ones,
            ones * scale, ones, ones,
            ch_q_norm[layer] * scale, ch_k_norm[layer], ones,
            diff_q_norm[layer] * scale, diff_k_norm[layer], ones,
            ones, ones, ones, ones])

        h = _rmsnorm(xf, attn_norm[layer])
        qkv = _qkv_proj(h, w_qkv, gains, cos_t, sin_t, seq, width)
        gates = _matmul(h, w_g, _epi_sigmoid, BF16, name="gate_proj")
        cum = _forget_cumsum(h, w_f, f_bias, batch, seq)
        t_fox = _tile(512, seq)
        cum_t = cum[:, :fox_heads].T.reshape(fox_heads, n // t_fox, t_fox)

        lam_init = 0.8 - 0.6 * math.exp(-0.3 * layer)
        branches = (
            _fox_attention(qkv, cum, cum_t, batch, seq, width),
            _sb_attention(qkv, batch, seq, width),
            _band_attention(qkv, ch_rel_bias[layer], batch, seq, width),
            _diff_attention(qkv, diff_lambda[layer], diff_subln[layer], lam_init, batch, seq, width),
        )
        merged = _gated_merge(branches, w_branch[layer].astype(BF16), gates)
        xf = _matmul(merged, w_out[layer].astype(BF16), _epi_residual, F32, extras=(xf,), extra_specs=(res_spec,),
                     tn=512, name="out_proj")

        h2, gate_t = _router(xf, ffn_norm[layer], router_w, router_bias)
        wg = w_gate[layer].transpose(1, 0, 2).reshape(d_model, n_experts * d_expert).astype(BF16)
        wu = w_up[layer].transpose(1, 0, 2).reshape(d_model, n_experts * d_expert).astype(BF16)
        act = _moe_gate_up(h2, wg, wu, gate_t.T, d_expert)
        wd = w_down[layer].reshape(n_experts * d_expert, d_model).astype(BF16)
        xf = _matmul(act, wd, _epi_residual, F32, extras=(xf,), extra_specs=(res_spec,), tm=512, tn=512,
                     name="moe_down")
    return xf.reshape(batch, seq, d_model)
```

```python
import functools
import math

import jax
import jax.numpy as jnp
from jax import lax
from jax.experimental import pallas as pl
from jax.experimental.pallas import tpu as pltpu

HEAD_DIM = 128
LANES = 128
CHUNK = 64
N_BRANCHES = 4
CH_LEFT_CHUNKS = 8
REL_CLIP = 128
ROPE_THETA = 10000.0
N_GROUPS = 4
RMS_EPS = 1e-6
NEG_INF = float("-inf")
VMEM_LIMIT_BYTES = 56 * 1024 * 1024

F32 = jnp.float32
BF16 = jnp.bfloat16


def _tile(pref, n):
    t = min(pref, n)
    while n % t:
        t //= 2
    return t


def _params(*sem):
    return pltpu.CompilerParams(dimension_semantics=sem, vmem_limit_bytes=VMEM_LIMIT_BYTES)


def _dot_nt(a, b):
    return lax.dot_general(a, b, (((1,), (1,)), ((), ())), preferred_element_type=F32)


def _log_sigmoid(x):
    return jnp.minimum(x, 0.0) - jnp.log1p(jnp.exp(-jnp.abs(x)))


def _rmsnorm_kernel(x_ref, g_ref, o_ref):
    x = x_ref[...]
    ms = jnp.mean(x * x, axis=-1, keepdims=True)
    o_ref[...] = (x * lax.rsqrt(ms + RMS_EPS) * g_ref[...]).astype(o_ref.dtype)


def _rmsnorm(x, gain):
    n, d = x.shape
    tm = _tile(256, n)
    return pl.pallas_call(
        _rmsnorm_kernel,
        grid=(n // tm,),
        in_specs=[pl.BlockSpec((tm, d), lambda i: (i, 0)), pl.BlockSpec((1, d), lambda i: (0, 0))],
        out_specs=pl.BlockSpec((tm, d), lambda i: (i, 0)),
        out_shape=jax.ShapeDtypeStruct((n, d), BF16),
        compiler_params=_params("parallel"),
        name="attn_rmsnorm",
    )(x, gain.reshape(1, d))


def _matmul(a, w, epilogue, out_dtype, extras=(), extra_specs=(), tm=1024, tn=1024, name="matmul"):
    m, k = a.shape
    _, n = w.shape
    tm, tn = _tile(tm, m), _tile(tn, n)

    def kern(a_ref, w_ref, *refs):
        acc = jnp.dot(a_ref[...], w_ref[...], preferred_element_type=F32)
        epilogue(acc, refs[:-1], refs[-1])

    return pl.pallas_call(
        kern,
        grid=(m // tm, n // tn),
        in_specs=[pl.BlockSpec((tm, k), lambda i, j: (i, 0)), pl.BlockSpec((k, tn), lambda i, j: (0, j)),
                  *[mk(tm, tn) for mk in extra_specs]],
        out_specs=pl.BlockSpec((tm, tn), lambda i, j: (i, j)),
        out_shape=jax.ShapeDtypeStruct((m, n), out_dtype),
        compiler_params=_params("parallel", "arbitrary"),
        name=name,
    )(a, w, *extras)


def _epi_sigmoid(acc, _, o_ref):
    o_ref[...] = jax.nn.sigmoid(acc).astype(o_ref.dtype)


def _epi_residual(acc, refs, o_ref):
    o_ref[...] = refs[0][...] + acc


PLAIN, NORM, NORM_ROPE = 0, 1, 2
SECTION_KINDS = (NORM, NORM, PLAIN, PLAIN, PLAIN, PLAIN, NORM, NORM, PLAIN, NORM_ROPE, NORM_ROPE, PLAIN)


def _qkv_epilogue(acc, refs, o_ref, *, sec_tiles):
    gain_ref, cos_ref, sin_ref = refs
    sec = pl.program_id(1) // sec_tiles
    g = gain_ref[pl.ds(sec, 1), :]
    tn = acc.shape[1]

    def run(kind):
        for hh in range(tn // HEAD_DIM):
            y = acc[:, hh * HEAD_DIM:(hh + 1) * HEAD_DIM]
            if kind != PLAIN:
                y = y * lax.rsqrt(jnp.mean(y * y, axis=-1, keepdims=True) + RMS_EPS)
            y = y * g
            if kind == NORM_ROPE:
                y = y * cos_ref[...] + pltpu.roll(y, HEAD_DIM // 2, 1) * sin_ref[...]
            o_ref[:, hh * HEAD_DIM:(hh + 1) * HEAD_DIM] = y.astype(o_ref.dtype)

    for kind in (PLAIN, NORM, NORM_ROPE):
        hit = functools.reduce(jnp.logical_or, [sec == s for s, kd in enumerate(SECTION_KINDS) if kd == kind])
        pl.when(hit)(functools.partial(run, kind))


def _qkv_proj(h, w_qkv, gains, cos_t, sin_t, seq, width):
    n = h.shape[0]
    tm = _tile(1024, seq)
    tn = _tile(1024, width)
    seq_tiles = seq // tm
    row_spec = lambda tm_, tn_: pl.BlockSpec((tm_, LANES), lambda i, j: (i % seq_tiles, 0))
    gain_spec = lambda tm_, tn_: pl.BlockSpec(gains.shape, lambda i, j: (0, 0))
    return _matmul(h, w_qkv, functools.partial(_qkv_epilogue, sec_tiles=width // tn), BF16,
                   extras=(gains, cos_t, sin_t), extra_specs=(gain_spec, row_spec, row_spec),
                   tm=tm, tn=tn, name="qkv_proj")


def _split3(x):
    hi = x.astype(BF16)
    r = x - hi.astype(F32)
    mid = r.astype(BF16)
    lo = (r - mid.astype(F32)).astype(BF16)
    return hi, mid, lo


def _fgate_kernel(h_ref, w_ref, b_ref, tri_ref, o_ref, carry_sc):
    @pl.when(pl.program_id(1) == 0)
    def _():
        carry_sc[...] = jnp.zeros_like(carry_sc)

    f = jnp.dot(h_ref[...], w_ref[...], preferred_element_type=F32) + b_ref[...]
    lf = _log_sigmoid(f)
    tri = tri_ref[...]
    c = carry_sc[...]
    for part in _split3(lf):
        c = c + jnp.dot(tri, part, preferred_element_type=F32)
    o_ref[...] = c
    carry_sc[...] = c[-1:, :]


def _forget_cumsum(h, w_f, bias, batch, seq):
    n, d = h.shape
    ts = _tile(512, seq)
    nt = seq // ts
    tri = (jnp.arange(ts)[:, None] >= jnp.arange(ts)[None, :]).astype(BF16)
    return pl.pallas_call(
        _fgate_kernel,
        grid=(batch, nt),
        in_specs=[pl.BlockSpec((ts, d), lambda b, t: (b * nt + t, 0)),
                  pl.BlockSpec((d, LANES), lambda b, t: (0, 0)),
                  pl.BlockSpec((1, LANES), lambda b, t: (0, 0)),
                  pl.BlockSpec((ts, ts), lambda b, t: (0, 0))],
        out_specs=pl.BlockSpec((ts, LANES), lambda b, t: (b * nt + t, 0)),
        out_shape=jax.ShapeDtypeStruct((n, LANES), F32),
        scratch_shapes=[pltpu.VMEM((1, LANES), F32)],
        compiler_params=_params("parallel", "arbitrary"),
        name="forget_cumsum",
    )(h, w_f, bias, tri)


def _softmax_step(s, v, m_sc, l_sc, acc_sc):
    m_prev = m_sc[...]
    m_new = jnp.maximum(m_prev, jnp.max(s, axis=1, keepdims=True))
    alpha = jnp.exp(m_prev - m_new)
    p = jnp.exp(s - m_new)
    l_sc[...] = alpha * l_sc[...] + jnp.sum(p, axis=1, keepdims=True)
    acc_sc[...] = alpha * acc_sc[...] + jnp.dot(p.astype(v.dtype), v, preferred_element_type=F32)
    m_sc[...] = m_new


def _softmax_init(m_sc, l_sc, acc_sc):
    m_sc[...] = jnp.full_like(m_sc, NEG_INF)
    l_sc[...] = jnp.zeros_like(l_sc)
    acc_sc[...] = jnp.zeros_like(acc_sc)


def _attn_specs(seq, tq, width, q_sec, k_sec, v_sec, v_mult=1, q_mult=1):
    nq = seq // tq
    sec = width // HEAD_DIM
    q_spec = lambda c=0: pl.BlockSpec((tq, HEAD_DIM), lambda b, h, i: (b * nq + i, q_sec * sec + h * q_mult + c))
    k_spec = lambda c=0: pl.BlockSpec((seq, HEAD_DIM), lambda b, h, i: (b, k_sec * sec + h * q_mult + c))
    v_spec = pl.BlockSpec((seq, HEAD_DIM * v_mult), lambda b, h, i: (b, v_sec * sec // v_mult + h))
    return q_spec, k_spec, v_spec


def _rep(x, n):
    return x if n == LANES else jnp.concatenate([x] * (n // LANES), axis=1)


def _lane_fold(x, op):
    out = x[:, :LANES]
    for c in range(1, x.shape[1] // LANES):
        out = op(out, x[:, c * LANES:(c + 1) * LANES])
    return out


def _row_max(s):
    return jnp.broadcast_to(jnp.max(_lane_fold(s, jnp.maximum), axis=1, keepdims=True), (s.shape[0], LANES))


def _softmax_update(s, mx, v, m_sc, l_sc, acc_sc):
    m_prev = m_sc[...]
    m_new = jnp.maximum(m_prev, mx)
    alpha = jnp.exp(m_prev - m_new)
    p = jnp.exp(s - _rep(m_new, s.shape[1]))
    l_sc[...] = alpha * l_sc[...] + _lane_fold(p, jnp.add)
    acc_sc[...] = (_rep(alpha, acc_sc.shape[1]) * acc_sc[...]
                   + jnp.dot(p.astype(v.dtype), v, preferred_element_type=F32))
    m_sc[...] = m_new


def _softmax_result(l_sc, acc_sc):
    return acc_sc[...] * (1.0 / jnp.sum(l_sc[...], axis=1, keepdims=True))


def _pipelined_causal(qi, score, update):
    @pl.when(qi == 0)
    def _():
        score(0, True)

    @pl.when(qi > 0)
    def _():
        score(0, False)

    def body(j, carry):
        update(j)
        score(j + 1, False)
        return carry

    lax.fori_loop(0, qi - 1, body, 0)

    @pl.when(qi > 0)
    def _():
        update(qi - 1)
        score(qi, True)

    update(qi)


def _fox_kernel(q_ref, k_ref, v_ref, ct_ref, cs_ref, o_ref, s_buf, mx_buf, m_sc, l_sc, acc_sc, ct_sc, *, t):
    h, qi = pl.program_id(1), pl.program_id(2)
    lane = lax.broadcasted_iota(jnp.int32, ct_ref.shape, 1)
    ct = jnp.sum(jnp.where(lane == h, ct_ref[...], 0.0), axis=1, keepdims=True)
    ct_sc[...] = jnp.broadcast_to(ct, ct_sc.shape)
    _softmax_init(m_sc, l_sc, acc_sc)

    def score(j, masked):
        ks = pl.multiple_of(j * t, t)
        s = _dot_nt(q_ref[...], k_ref[pl.ds(ks, t), :])
        s = s + _rep(ct_sc[...], t) - cs_ref[pl.ds(j, 1), :]
        if masked:
            row = lax.broadcasted_iota(jnp.int32, (t, t), 0)
            col = lax.broadcasted_iota(jnp.int32, (t, t), 1)
            s = jnp.where(col <= row, s, NEG_INF)
        s_buf[j % 2] = s
        mx_buf[j % 2] = _row_max(s)

    def update(j):
        ks = pl.multiple_of(j * t, t)
        _softmax_update(s_buf[j % 2], mx_buf[j % 2], v_ref[pl.ds(ks, t), :], m_sc, l_sc, acc_sc)

    _pipelined_causal(qi, score, update)
    o_ref[...] = _softmax_result(l_sc, acc_sc).astype(o_ref.dtype)


def _fox_attention(qkv, cum, cum_t, batch, seq, width):
    n = qkv.shape[0]
    heads = width // HEAD_DIM
    t = _tile(512, seq)
    nq = seq // t
    q_spec, k_spec, v_spec = _attn_specs(seq, t, width, 0, 1, 2)
    stat = pltpu.VMEM((t, LANES), F32)
    return pl.pallas_call(
        functools.partial(_fox_kernel, t=t),
        grid=(batch, heads, nq),
        in_specs=[q_spec(), k_spec(), v_spec,
                  pl.BlockSpec((t, LANES), lambda b, h, i: (b * nq + i, 0)),
                  pl.BlockSpec((None, nq, t), lambda b, h, i: (h, b, 0))],
        out_specs=pl.BlockSpec((t, HEAD_DIM), lambda b, h, i: (b * nq + i, h)),
        out_shape=jax.ShapeDtypeStruct((n, width), BF16),
        scratch_shapes=[pltpu.VMEM((2, t, t), F32), pltpu.VMEM((2, t, LANES), F32),
                        stat, stat, pltpu.VMEM((t, HEAD_DIM), F32), stat],
        compiler_params=_params("parallel", "parallel", "arbitrary"),
        name="fox_attention",
    )(qkv, qkv, qkv, cum, cum_t.reshape(heads, n // t, t))


SB_DEAD_LOG = -110.0


def _sb_kernel(q_ref, k_ref, v_ref, u_ref, o_ref, e_buf, rs_buf, c_sc, acc_sc, *, t):
    qi = pl.program_id(2)
    c_sc[...] = jnp.zeros_like(c_sc)
    acc_sc[...] = jnp.zeros_like(acc_sc)

    def score(jj, diagonal):
        ks = pl.multiple_of((qi - jj) * t, t)
        z = _dot_nt(q_ref[...], k_ref[pl.ds(ks, t), :])
        softplus = jnp.log(1.0 + jnp.exp(-jnp.abs(z)))
        log_beta = jnp.minimum(z, 0.0) - softplus
        log_rest = jnp.minimum(-z, 0.0) - softplus
        if diagonal:
            row = lax.broadcasted_iota(jnp.int32, (t, t), 0)
            col = lax.broadcasted_iota(jnp.int32, (t, t), 1)
            strict = col < row
            log_rest = jnp.where(strict, log_rest, 0.0)
        hi = log_rest.astype(BF16)
        lo = (log_rest - hi.astype(F32)).astype(BF16)
        u = u_ref[...]
        cw = u.shape[0]
        later, tail = [], None
        for c in reversed(range(t // cw)):
            sl = slice(c * cw, (c + 1) * cw)
            inside = jnp.dot(hi[:, sl], u, preferred_element_type=F32) + jnp.dot(lo[:, sl], u, preferred_element_type=F32)
            later.insert(0, inside if tail is None else inside + _rep(tail, cw))
            chunk_sum = jnp.broadcast_to(
                jnp.sum(_lane_fold(log_rest[:, sl], jnp.add), axis=1, keepdims=True), (t, LANES))
            tail = chunk_sum if tail is None else tail + chunk_sum
        e = log_beta + jnp.concatenate(later, axis=1)
        if diagonal:
            e = jnp.where(strict, e, NEG_INF)
        e_buf[jj % 2] = e
        rs_buf[jj % 2] = tail

    def update(jj):
        ks = pl.multiple_of((qi - jj) * t, t)
        c = c_sc[...]
        w = jnp.exp(e_buf[jj % 2] + _rep(c, t))
        v = v_ref[pl.ds(ks, t), :]
        acc_sc[...] += jnp.dot(w.astype(v.dtype), v, preferred_element_type=F32)
        c_sc[...] = c + rs_buf[jj % 2]

    score(0, True)

    def body(state):
        jj, _ = state
        update(jj)
        score(jj + 1, False)
        return jj + 1, jnp.max(c_sc[...])

    jj, c_max = lax.while_loop(lambda st: (st[0] < qi) & (st[1] > SB_DEAD_LOG), body, (0, jnp.float32(0.0)))
    pl.when(c_max > SB_DEAD_LOG)(lambda: update(jj))
    o_ref[...] = acc_sc[...].astype(o_ref.dtype)


def _sb_attention(qkv, batch, seq, width):
    n = qkv.shape[0]
    heads = width // HEAD_DIM
    t = _tile(512, seq)
    nq = seq // t
    q_spec, k_spec, v_spec = _attn_specs(seq, t, width, 3, 4, 5)
    cw = min(256, t)
    upper = (jnp.arange(cw)[:, None] > jnp.arange(cw)[None, :]).astype(BF16)
    return pl.pallas_call(
        functools.partial(_sb_kernel, t=t),
        grid=(batch, heads, nq),
        in_specs=[q_spec(), k_spec(), v_spec, pl.BlockSpec((cw, cw), lambda b, h, i: (0, 0))],
        out_specs=pl.BlockSpec((t, HEAD_DIM), lambda b, h, i: (b * nq + i, h)),
        out_shape=jax.ShapeDtypeStruct((n, width), BF16),
        scratch_shapes=[pltpu.VMEM((2, t, t), F32), pltpu.VMEM((2, t, LANES), F32),
                        pltpu.VMEM((t, LANES), F32), pltpu.VMEM((t, HEAD_DIM), F32)],
        compiler_params=_params("parallel", "parallel", "arbitrary"),
        name="stick_breaking_attention",
    )(qkv, qkv, qkv, upper)


def _band_kernel(q_ref, k_ref, v_ref, tab_ref, o_ref, bias_sc, s_buf, mx_buf, m_sc, l_sc, acc_sc, *, t, n_back):
    qi = pl.program_id(2)

    @pl.when(qi == 0)
    def _():
        width = tab_ref.shape[-1]
        rolled = pltpu.roll(jnp.broadcast_to(tab_ref[...], (t, width)), 0, 1, stride=1, stride_axis=0)
        row = lax.broadcasted_iota(jnp.int32, (t, t), 0)
        col = lax.broadcasted_iota(jnp.int32, (t, t), 1)
        for d in range(n_back + 1):
            gap = (row + d * t) // CHUNK - col // CHUNK
            visible = (gap >= 0) & (gap <= CH_LEFT_CHUNKS)
            c0 = (n_back + 1 - d) * t
            bias_sc[d] = jnp.where(visible, rolled[:, c0:c0 + t], NEG_INF)

    _softmax_init(m_sc, l_sc, acc_sc)

    def score(d):
        ks = pl.multiple_of((qi - d) * t, t)
        s = _dot_nt(q_ref[...], k_ref[pl.ds(ks, t), :]) + bias_sc[d]
        s_buf[d % 2] = s
        mx_buf[d % 2] = _row_max(s)

    def update(d):
        ks = pl.multiple_of((qi - d) * t, t)
        _softmax_update(s_buf[d % 2], mx_buf[d % 2], v_ref[pl.ds(ks, t), :], m_sc, l_sc, acc_sc)

    def body(d, carry):
        update(d)
        score(d + 1)
        return carry

    n_tiles = jnp.minimum(qi, n_back) + 1
    score(0)
    lax.fori_loop(0, n_tiles - 1, body, 0)
    update(n_tiles - 1)
    o_ref[...] = _softmax_result(l_sc, acc_sc).astype(o_ref.dtype)


def _band_table(rel_bias, t, n_back):
    rel = (n_back + 1) * t - jnp.arange((n_back + 2) * t)
    return rel_bias[:, None, jnp.clip(rel, -REL_CLIP, REL_CLIP) + REL_CLIP].astype(F32)


def _band_attention(qkv, rel_bias, batch, seq, width):
    n = qkv.shape[0]
    heads = width // HEAD_DIM
    t = _tile(512, seq)
    nq = seq // t
    n_back = -(-(CH_LEFT_CHUNKS * CHUNK) // t)
    table = _band_table(rel_bias, t, n_back)
    q_spec, k_spec, v_spec = _attn_specs(seq, t, width, 6, 7, 8)
    stat = pltpu.VMEM((t, LANES), F32)
    return pl.pallas_call(
        functools.partial(_band_kernel, t=t, n_back=n_back),
        grid=(batch, heads, nq),
        in_specs=[q_spec(), k_spec(), v_spec,
                  pl.BlockSpec((None, 1, table.shape[-1]), lambda b, h, i: (h, 0, 0))],
        out_specs=pl.BlockSpec((t, HEAD_DIM), lambda b, h, i: (b * nq + i, h)),
        out_shape=jax.ShapeDtypeStruct((n, width), BF16),
        scratch_shapes=[pltpu.VMEM((n_back + 1, t, t), F32), pltpu.VMEM((2, t, t), F32),
                        pltpu.VMEM((2, t, LANES), F32), stat, stat, pltpu.VMEM((t, HEAD_DIM), F32)],
        compiler_params=_params("parallel", "parallel", "arbitrary"),
        name="chunk_band_attention",
    )(qkv, qkv, qkv, table)


def _diff_kernel(q1_ref, q2_ref, k1_ref, k2_ref, v_ref, lam_ref, g_ref, o_ref, s_buf, mx_buf,
                 m1_sc, l1_sc, a1_sc, m2_sc, l2_sc, a2_sc, *, t, lam_init):
    qi = pl.program_id(2)
    _softmax_init(m1_sc, l1_sc, a1_sc)
    _softmax_init(m2_sc, l2_sc, a2_sc)
    streams = ((q1_ref, k1_ref, m1_sc, l1_sc, a1_sc), (q2_ref, k2_ref, m2_sc, l2_sc, a2_sc))

    def score(j, masked):
        ks = pl.multiple_of(j * t, t)
        for i, (q_ref, k_ref, _, _, _) in enumerate(streams):
            s = _dot_nt(q_ref[...], k_ref[pl.ds(ks, t), :])
            if masked:
                row = lax.broadcasted_iota(jnp.int32, (t, t), 0) // CHUNK
                col = lax.broadcasted_iota(jnp.int32, (t, t), 1) // CHUNK
                s = jnp.where(col <= row, s, NEG_INF)
            s_buf[j % 2, i] = s
            mx_buf[j % 2, i] = _row_max(s)

    def update(j):
        ks = pl.multiple_of(j * t, t)
        v = v_ref[pl.ds(ks, t), :]
        for i, (_, _, m_sc, l_sc, a_sc) in enumerate(streams):
            _softmax_update(s_buf[j % 2, i], mx_buf[j % 2, i], v, m_sc, l_sc, a_sc)

    _pipelined_causal(qi, score, update)

    lv = lam_ref[...]
    lam = (jnp.exp(jnp.sum(lv[0:1] * lv[1:2], axis=1, keepdims=True))
           - jnp.exp(jnp.sum(lv[2:3] * lv[3:4], axis=1, keepdims=True)) + lam_init)
    o = _softmax_result(l1_sc, a1_sc) - lam * _softmax_result(l2_sc, a2_sc)
    y = o * lax.rsqrt(jnp.mean(o * o, axis=-1, keepdims=True) + RMS_EPS) * g_ref[...]
    o_ref[...] = (y * (1.0 - lam_init)).astype(o_ref.dtype)


def _diff_attention(qkv, lam_vecs, subln, lam_init, batch, seq, width):
    n = qkv.shape[0]
    heads = width // (2 * HEAD_DIM)
    t = _tile(512, seq)
    nq = seq // t
    q_spec, k_spec, v_spec = _attn_specs(seq, t, width, 9, 10, 11, v_mult=2, q_mult=2)
    scratch = [pltpu.VMEM((t, LANES), F32), pltpu.VMEM((t, LANES), F32), pltpu.VMEM((t, 2 * HEAD_DIM), F32)]
    stage = [pltpu.VMEM((2, 2, t, t), F32), pltpu.VMEM((2, 2, t, LANES), F32)]
    return pl.pallas_call(
        functools.partial(_diff_kernel, t=t, lam_init=lam_init),
        grid=(batch, heads, nq),
        in_specs=[q_spec(0), q_spec(1), k_spec(0), k_spec(1), v_spec,
                  pl.BlockSpec(lam_vecs.shape, lambda b, h, i: (0, 0)),
                  pl.BlockSpec((1, 2 * HEAD_DIM), lambda b, h, i: (0, 0))],
        out_specs=pl.BlockSpec((t, 2 * HEAD_DIM), lambda b, h, i: (b * nq + i, h)),
        out_shape=jax.ShapeDtypeStruct((n, width), BF16),
        scratch_shapes=stage + scratch + scratch,
        compiler_params=_params("parallel", "parallel", "arbitrary"),
        name="differential_attention",
    )(qkv, qkv, qkv, qkv, qkv, lam_vecs, subln.reshape(1, -1))


def _merge_kernel(*refs):
    br, wb, gates, o_ref = refs[0:4], refs[4], refs[5:9], refs[9]
    acc = None
    for i in range(N_BRANCHES):
        term = gates[i][...].astype(F32) * jnp.dot(br[i][...], wb[i], preferred_element_type=F32)
        acc = term if acc is None else acc + term
    o_ref[...] = acc.astype(o_ref.dtype)


def _gated_merge(branches, w_branch, gates):
    n, width = branches[0].shape
    d = w_branch.shape[-1]
    tm, tn = _tile(1024, n), _tile(512, d)
    col_tiles = d // tn
    br_spec = pl.BlockSpec((tm, width), lambda i, j: (i, 0))
    gate_specs = [pl.BlockSpec((tm, tn), lambda i, j, g=g: (i, g * col_tiles + j)) for g in range(N_BRANCHES)]
    return pl.pallas_call(
        _merge_kernel,
        grid=(n // tm, col_tiles),
        in_specs=[br_spec] * N_BRANCHES + [pl.BlockSpec((N_BRANCHES, width, tn), lambda i, j: (0, 0, j))] + gate_specs,
        out_specs=pl.BlockSpec((tm, tn), lambda i, j: (i, j)),
        out_shape=jax.ShapeDtypeStruct((n, d), BF16),
        compiler_params=_params("parallel", "arbitrary"),
        name="gated_merge",
    )(*branches, w_branch, *([gates] * N_BRANCHES))


def _router_kernel(x_ref, g_ref, rw_ref, rb_ref, h_ref, gate_ref, *, n_experts):
    x = x_ref[...]
    hn = x * lax.rsqrt(jnp.mean(x * x, axis=-1, keepdims=True) + RMS_EPS) * g_ref[...]
    h_ref[...] = hn.astype(h_ref.dtype)
    logits = None
    for wi, wp in enumerate(_split3(rw_ref[...])):
        for hi, hp in enumerate(_split3(hn)):
            if wi + hi <= 2:
                term = _dot_nt(wp, hp)
                logits = term if logits is None else logits + term
    score = jax.nn.sigmoid(logits)
    biased = score + rb_ref[...]
    per = n_experts // N_GROUPS
    rows = [biased[e:e + 1, :] for e in range(n_experts)]
    group_score = []
    for g in range(N_GROUPS):
        r = rows[g * per:(g + 1) * per]
        top1 = functools.reduce(jnp.maximum, r)
        first = []
        taken = jnp.zeros_like(top1, dtype=jnp.bool_)
        for v in r:
            is1 = (v == top1) & ~taken
            taken = taken | is1
            first.append(is1)
        rest = [jnp.where(f, NEG_INF, v) for f, v in zip(first, r)]
        top2 = functools.reduce(jnp.maximum, rest)
        second = []
        taken = jnp.zeros_like(top1, dtype=jnp.bool_)
        for v in rest:
            is2 = (v == top2) & ~taken
            taken = taken | is2
            second.append(is2)
        group_score.append((top1 + top2, [a | b for a, b in zip(first, second)]))
    best = functools.reduce(jnp.maximum, [gs for gs, _ in group_score])
    taken = jnp.zeros_like(best, dtype=jnp.bool_)
    picked = []
    for gs, sel in group_score:
        chosen = (gs == best) & ~taken
        taken = taken | chosen
        picked += [s & chosen for s in sel]
    w = [jnp.where(p, score[e:e + 1, :], 0.0) for e, p in enumerate(picked)]
    total = functools.reduce(jnp.add, w)
    inv = 1.0 / total
    for e, we in enumerate(w):
        gate_ref[e:e + 1, :] = we * inv


def _router(x, gain, router_w, router_bias):
    n, d = x.shape
    e = router_w.shape[1]
    tm = _tile(256, n)
    return pl.pallas_call(
        functools.partial(_router_kernel, n_experts=e),
        grid=(n // tm,),
        in_specs=[pl.BlockSpec((tm, d), lambda i: (i, 0)), pl.BlockSpec((1, d), lambda i: (0, 0)),
                  pl.BlockSpec((e, d), lambda i: (0, 0)), pl.BlockSpec((e, 1), lambda i: (0, 0))],
        out_specs=[pl.BlockSpec((tm, d), lambda i: (i, 0)), pl.BlockSpec((e, tm), lambda i: (0, i))],
        out_shape=[jax.ShapeDtypeStruct((n, d), BF16), jax.ShapeDtypeStruct((e, n), F32)],
        compiler_params=_params("parallel"),
        name="ffn_norm_router",
    )(x, gain.reshape(1, d), router_w.T, router_bias.reshape(e, 1))


def _gate_up_kernel(h_ref, wg_ref, wu_ref, gate_ref, o_ref, *, d_expert, per_tile):
    h = h_ref[...]
    hg = jnp.dot(h, wg_ref[...], preferred_element_type=F32)
    hu = jnp.dot(h, wu_ref[...], preferred_element_type=F32)
    act = hg * jax.nn.sigmoid(hg) * hu
    gate = gate_ref[...]
    lane = lax.broadcasted_iota(jnp.int32, gate.shape, 1)
    first = pl.program_id(1) * per_tile
    for e in range(per_tile):
        ge = jnp.sum(jnp.where(lane == first + e, gate, 0.0), axis=1, keepdims=True)
        o_ref[:, e * d_expert:(e + 1) * d_expert] = (act[:, e * d_expert:(e + 1) * d_expert] * ge).astype(o_ref.dtype)


def _moe_gate_up(h, wg, wu, gate, d_expert):
    n, d = h.shape
    f = wg.shape[1]
    n_experts = f // d_expert
    per_tile = 2 if n_experts % 2 == 0 else 1
    tn = per_tile * d_expert
    tm = _tile(512, n)
    return pl.pallas_call(
        functools.partial(_gate_up_kernel, d_expert=d_expert, per_tile=per_tile),
        grid=(n // tm, f // tn),
        in_specs=[pl.BlockSpec((tm, d), lambda i, j: (i, 0)), pl.BlockSpec((d, tn), lambda i, j: (0, j)),
                  pl.BlockSpec((d, tn), lambda i, j: (0, j)), pl.BlockSpec((tm, n_experts), lambda i, j: (i, 0))],
        out_specs=pl.BlockSpec((tm, tn), lambda i, j: (i, j)),
        out_shape=jax.ShapeDtypeStruct((n, f), BF16),
        compiler_params=_params("parallel", "arbitrary"),
        name="moe_gate_up",
    )(h, wg, wu, gate)


def _rope_tables(seq):
    half = HEAD_DIM // 2
    inv_freq = jnp.power(ROPE_THETA, -jnp.arange(half, dtype=F32) / half)
    ang = jnp.arange(seq, dtype=F32)[:, None] * inv_freq[None, :]
    cos, sin = jnp.cos(ang), jnp.sin(ang)
    return jnp.concatenate([cos, cos], axis=-1), jnp.concatenate([-sin, sin], axis=-1)


def kernel(x, attn_norm, w_in, fox_forget_bias, fox_q_norm, fox_k_norm, ch_q_norm, ch_k_norm, ch_rel_bias,
           diff_q_norm, diff_k_norm, diff_lambda, diff_subln, w_branch, w_out, ffn_norm, router_w, router_bias,
           w_gate, w_up, w_down):
    batch, seq, d_model = x.shape
    depth = w_in.shape[0]
    width = d_model // N_BRANCHES
    fox_heads = width // HEAD_DIM
    qkv_cols = 3 * width
    n_experts, _, d_expert = w_gate.shape[1:]
    n = batch * seq
    scale = HEAD_DIM ** -0.5
    assert fox_heads <= LANES and width % (2 * HEAD_DIM) == 0 and seq % CHUNK == 0

    cos_t, sin_t = _rope_tables(seq)
    ones = jnp.ones((HEAD_DIM,), F32)
    res_spec = lambda tm, tn: pl.BlockSpec((tm, tn), lambda i, j: (i, j))
    xf = x.reshape(n, d_model)

    for layer in range(depth):
        wl = w_in[layer]
        s_f, s_sb, s_gate = qkv_cols, qkv_cols + fox_heads, 4 * qkv_cols + fox_heads
        w_qkv = jnp.concatenate([wl[:, :s_f], wl[:, s_sb:s_gate]], axis=1).astype(BF16)
        w_f = jnp.pad(wl[:, s_f:s_sb], ((0, 0), (0, LANES - fox_heads))).astype(BF16)
        w_g = wl[:, s_gate:].astype(BF16)
        f_bias = jnp.pad(fox_forget_bias[layer], (0, LANES - fox_heads)).reshape(1, LANES)
        gains = jnp.stack([
            fox_q_norm[layer] * scale, fox_k_norm[layer], ones,
            ones * scale, ones, ones,
            ch_q_norm[layer] * scale, ch_k_norm[layer], ones,
            diff_q_norm[layer] * scale, diff_k_norm[layer], ones,
            ones, ones, ones, ones])

        h = _rmsnorm(xf, attn_norm[layer])
        qkv = _qkv_proj(h, w_qkv, gains, cos_t, sin_t, seq, width)
        gates = _matmul(h, w_g, _epi_sigmoid, BF16, name="gate_proj")
        cum = _forget_cumsum(h, w_f, f_bias, batch, seq)
        cum_t = cum[:, :fox_heads].T

        lam_init = 0.8 - 0.6 * math.exp(-0.3 * layer)
        branches = (
            _fox_attention(qkv, cum, cum_t, batch, seq, width),
            _sb_attention(qkv, batch, seq, width),
            _band_attention(qkv, ch_rel_bias[layer], batch, seq, width),
            _diff_attention(qkv, diff_lambda[layer], diff_subln[layer], lam_init, batch, seq, width),
        )
        merged = _gated_merge(branches, w_branch[layer].astype(BF16), gates)
        xf = _matmul(merged, w_out[layer].astype(BF16), _epi_residual, F32, extras=(xf,), extra_specs=(res_spec,),
                     tn=512, name="out_proj")

        h2, gate_t = _router(xf, ffn_norm[layer], router_w, router_bias)
        wg = w_gate[layer].transpose(1, 0, 2).reshape(d_model, n_experts * d_expert).astype(BF16)
        wu = w_up[layer].transpose(1, 0, 2).reshape(d_model, n_experts * d_expert).astype(BF16)
        act = _moe_gate_up(h2, wg, wu, gate_t.T, d_expert)
        wd = w_down[layer].reshape(n_experts * d_expert, d_model).astype(BF16)
        xf = _matmul(act, wd, _epi_residual, F32, extras=(xf,), extra_specs=(res_spec,), tm=512, tn=512,
                     name="moe_down")
    return xf.reshape(batch, seq, d_model)
```

```python
import functools
import math

import jax
import jax.numpy as jnp
from jax import lax
from jax.experimental import pallas as pl
from jax.experimental.pallas import tpu as pltpu

HEAD_DIM = 128
LANES = 128
CHUNK = 64
N_BRANCHES = 4
CH_LEFT_CHUNKS = 8
REL_CLIP = 128
ROPE_THETA = 10000.0
N_GROUPS = 4
RMS_EPS = 1e-6
NEG_INF = float("-inf")
VMEM_LIMIT_BYTES = 56 * 1024 * 1024

F32 = jnp.float32
BF16 = jnp.bfloat16


def _tile(pref, n):
    t = min(pref, n)
    while n % t:
        t //= 2
    return t


def _params(*sem):
    return pltpu.CompilerParams(dimension_semantics=sem, vmem_limit_bytes=VMEM_LIMIT_BYTES)


def _dot_nt(a, b):
    return lax.dot_general(a, b, (((1,), (1,)), ((), ())), preferred_element_type=F32)


def _log_sigmoid(x):
    return jnp.minimum(x, 0.0) - jnp.log1p(jnp.exp(-jnp.abs(x)))


def _rmsnorm_kernel(x_ref, g_ref, o_ref):
    x = x_ref[...]
    ms = jnp.mean(x * x, axis=-1, keepdims=True)
    o_ref[...] = (x * lax.rsqrt(ms + RMS_EPS) * g_ref[...]).astype(o_ref.dtype)


def _rmsnorm(x, gain):
    n, d = x.shape
    tm = _tile(256, n)
    return pl.pallas_call(
        _rmsnorm_kernel,
        grid=(n // tm,),
        in_specs=[pl.BlockSpec((tm, d), lambda i: (i, 0)), pl.BlockSpec((1, d), lambda i: (0, 0))],
        out_specs=pl.BlockSpec((tm, d), lambda i: (i, 0)),
        out_shape=jax.ShapeDtypeStruct((n, d), BF16),
        compiler_params=_params("parallel"),
        name="attn_rmsnorm",
    )(x, gain.reshape(1, d))


def _matmul(a, w, epilogue, out_dtype, extras=(), extra_specs=(), tm=1024, tn=1024, name="matmul"):
    m, k = a.shape
    _, n = w.shape
    tm, tn = _tile(tm, m), _tile(tn, n)

    def kern(a_ref, w_ref, *refs):
        acc = jnp.dot(a_ref[...], w_ref[...], preferred_element_type=F32)
        epilogue(acc, refs[:-1], refs[-1])

    return pl.pallas_call(
        kern,
        grid=(m // tm, n // tn),
        in_specs=[pl.BlockSpec((tm, k), lambda i, j: (i, 0)), pl.BlockSpec((k, tn), lambda i, j: (0, j)),
                  *[mk(tm, tn) for mk in extra_specs]],
        out_specs=pl.BlockSpec((tm, tn), lambda i, j: (i, j)),
        out_shape=jax.ShapeDtypeStruct((m, n), out_dtype),
        compiler_params=_params("parallel", "arbitrary"),
        name=name,
    )(a, w, *extras)


def _epi_sigmoid(acc, _, o_ref):
    o_ref[...] = jax.nn.sigmoid(acc).astype(o_ref.dtype)


def _epi_residual(acc, refs, o_ref):
    o_ref[...] = refs[0][...] + acc


PLAIN, NORM, NORM_ROPE = 0, 1, 2
SECTION_KINDS = (NORM, NORM, PLAIN, PLAIN, PLAIN, PLAIN, NORM, NORM, PLAIN, NORM_ROPE, NORM_ROPE, PLAIN)


def _qkv_epilogue(acc, refs, o_ref, *, sec_tiles):
    gain_ref, cos_ref, sin_ref = refs
    sec = pl.program_id(1) // sec_tiles
    g = gain_ref[pl.ds(sec, 1), :]
    tn = acc.shape[1]

    def run(kind):
        for hh in range(tn // HEAD_DIM):
            y = acc[:, hh * HEAD_DIM:(hh + 1) * HEAD_DIM]
            if kind != PLAIN:
                y = y * lax.rsqrt(jnp.mean(y * y, axis=-1, keepdims=True) + RMS_EPS)
            y = y * g
            if kind == NORM_ROPE:
                y = y * cos_ref[...] + pltpu.roll(y, HEAD_DIM // 2, 1) * sin_ref[...]
            o_ref[:, hh * HEAD_DIM:(hh + 1) * HEAD_DIM] = y.astype(o_ref.dtype)

    for kind in (PLAIN, NORM, NORM_ROPE):
        hit = functools.reduce(jnp.logical_or, [sec == s for s, kd in enumerate(SECTION_KINDS) if kd == kind])
        pl.when(hit)(functools.partial(run, kind))


def _qkv_proj(h, w_qkv, gains, cos_t, sin_t, seq, width):
    n = h.shape[0]
    tm = _tile(1024, seq)
    tn = _tile(1024, width)
    seq_tiles = seq // tm
    row_spec = lambda tm_, tn_: pl.BlockSpec((tm_, LANES), lambda i, j: (i % seq_tiles, 0))
    gain_spec = lambda tm_, tn_: pl.BlockSpec(gains.shape, lambda i, j: (0, 0))
    return _matmul(h, w_qkv, functools.partial(_qkv_epilogue, sec_tiles=width // tn), BF16,
                   extras=(gains, cos_t, sin_t), extra_specs=(gain_spec, row_spec, row_spec),
                   tm=tm, tn=tn, name="qkv_proj")


def _split3(x):
    hi = x.astype(BF16)
    r = x - hi.astype(F32)
    mid = r.astype(BF16)
    lo = (r - mid.astype(F32)).astype(BF16)
    return hi, mid, lo


def _fgate_kernel(h_ref, w_ref, b_ref, tri_ref, o_ref, carry_sc):
    @pl.when(pl.program_id(1) == 0)
    def _():
        carry_sc[...] = jnp.zeros_like(carry_sc)

    f = jnp.dot(h_ref[...], w_ref[...], preferred_element_type=F32) + b_ref[...]
    lf = _log_sigmoid(f)
    tri = tri_ref[...]
    c = carry_sc[...]
    for part in _split3(lf):
        c = c + jnp.dot(tri, part, preferred_element_type=F32)
    o_ref[...] = c
    carry_sc[...] = c[-1:, :]


def _forget_cumsum(h, w_f, bias, batch, seq):
    n, d = h.shape
    ts = _tile(512, seq)
    nt = seq // ts
    tri = (jnp.arange(ts)[:, None] >= jnp.arange(ts)[None, :]).astype(BF16)
    return pl.pallas_call(
        _fgate_kernel,
        grid=(batch, nt),
        in_specs=[pl.BlockSpec((ts, d), lambda b, t: (b * nt + t, 0)),
                  pl.BlockSpec((d, LANES), lambda b, t: (0, 0)),
                  pl.BlockSpec((1, LANES), lambda b, t: (0, 0)),
                  pl.BlockSpec((ts, ts), lambda b, t: (0, 0))],
        out_specs=pl.BlockSpec((ts, LANES), lambda b, t: (b * nt + t, 0)),
        out_shape=jax.ShapeDtypeStruct((n, LANES), F32),
        scratch_shapes=[pltpu.VMEM((1, LANES), F32)],
        compiler_params=_params("parallel", "arbitrary"),
        name="forget_cumsum",
    )(h, w_f, bias, tri)


def _softmax_step(s, v, m_sc, l_sc, acc_sc):
    m_prev = m_sc[...]
    m_new = jnp.maximum(m_prev, jnp.max(s, axis=1, keepdims=True))
    alpha = jnp.exp(m_prev - m_new)
    p = jnp.exp(s - m_new)
    l_sc[...] = alpha * l_sc[...] + jnp.sum(p, axis=1, keepdims=True)
    acc_sc[...] = alpha * acc_sc[...] + jnp.dot(p.astype(v.dtype), v, preferred_element_type=F32)
    m_sc[...] = m_new


def _softmax_init(m_sc, l_sc, acc_sc):
    m_sc[...] = jnp.full_like(m_sc, NEG_INF)
    l_sc[...] = jnp.zeros_like(l_sc)
    acc_sc[...] = jnp.zeros_like(acc_sc)


def _attn_specs(seq, tq, width, q_sec, k_sec, v_sec, v_mult=1, q_mult=1):
    nq = seq // tq
    sec = width // HEAD_DIM
    q_spec = lambda c=0: pl.BlockSpec((tq, HEAD_DIM), lambda b, h, i: (b * nq + i, q_sec * sec + h * q_mult + c))
    k_spec = lambda c=0: pl.BlockSpec((seq, HEAD_DIM), lambda b, h, i: (b, k_sec * sec + h * q_mult + c))
    v_spec = pl.BlockSpec((seq, HEAD_DIM * v_mult), lambda b, h, i: (b, v_sec * sec // v_mult + h))
    return q_spec, k_spec, v_spec


def _rep(x, n):
    return x if n == LANES else jnp.concatenate([x] * (n // LANES), axis=1)


def _lane_fold(x, op):
    out = x[:, :LANES]
    for c in range(1, x.shape[1] // LANES):
        out = op(out, x[:, c * LANES:(c + 1) * LANES])
    return out


def _row_max(s):
    return jnp.broadcast_to(jnp.max(_lane_fold(s, jnp.maximum), axis=1, keepdims=True), (s.shape[0], LANES))


def _softmax_update(s, mx, v, m_sc, l_sc, acc_sc):
    m_prev = m_sc[...]
    m_new = jnp.maximum(m_prev, mx)
    alpha = jnp.exp(m_prev - m_new)
    p = jnp.exp(s - _rep(m_new, s.shape[1]))
    l_sc[...] = alpha * l_sc[...] + _lane_fold(p, jnp.add)
    acc_sc[...] = (_rep(alpha, acc_sc.shape[1]) * acc_sc[...]
                   + jnp.dot(p.astype(v.dtype), v, preferred_element_type=F32))
    m_sc[...] = m_new


def _softmax_result(l_sc, acc_sc):
    return acc_sc[...] * (1.0 / jnp.sum(l_sc[...], axis=1, keepdims=True))


def _pipelined_causal(qi, score, update):
    @pl.when(qi == 0)
    def _():
        score(0, True)

    @pl.when(qi > 0)
    def _():
        score(0, False)

    def body(j, carry):
        update(j)
        score(j + 1, False)
        return carry

    lax.fori_loop(0, qi - 1, body, 0)

    @pl.when(qi > 0)
    def _():
        update(qi - 1)
        score(qi, True)

    update(qi)


def _fox_kernel(q_ref, k_ref, v_ref, ct_ref, cs_ref, o_ref, s_buf, mx_buf, m_sc, l_sc, acc_sc, ct_sc, *, t):
    h, qi = pl.program_id(1), pl.program_id(2)
    lane = lax.broadcasted_iota(jnp.int32, ct_ref.shape, 1)
    ct = jnp.sum(jnp.where(lane == h, ct_ref[...], 0.0), axis=1, keepdims=True)
    ct_sc[...] = jnp.broadcast_to(ct, ct_sc.shape)
    _softmax_init(m_sc, l_sc, acc_sc)

    def score(j, masked):
        ks = pl.multiple_of(j * t, t)
        s = _dot_nt(q_ref[...], k_ref[pl.ds(ks, t), :])
        s = s + _rep(ct_sc[...], t) - cs_ref[pl.ds(j, 1), :]
        if masked:
            row = lax.broadcasted_iota(jnp.int32, (t, t), 0)
            col = lax.broadcasted_iota(jnp.int32, (t, t), 1)
            s = jnp.where(col <= row, s, NEG_INF)
        s_buf[j % 2] = s
        mx_buf[j % 2] = _row_max(s)

    def update(j):
        ks = pl.multiple_of(j * t, t)
        _softmax_update(s_buf[j % 2], mx_buf[j % 2], v_ref[pl.ds(ks, t), :], m_sc, l_sc, acc_sc)

    _pipelined_causal(qi, score, update)
    o_ref[...] = _softmax_result(l_sc, acc_sc).astype(o_ref.dtype)


def _fox_attention(qkv, cum, cum_t, batch, seq, width):
    n = qkv.shape[0]
    heads = width // HEAD_DIM
    t = _tile(512, seq)
    nq = seq // t
    q_spec, k_spec, v_spec = _attn_specs(seq, t, width, 0, 1, 2)
    stat = pltpu.VMEM((t, LANES), F32)
    return pl.pallas_call(
        functools.partial(_fox_kernel, t=t),
        grid=(batch, heads, nq),
        in_specs=[q_spec(), k_spec(), v_spec,
                  pl.BlockSpec((t, LANES), lambda b, h, i: (b * nq + i, 0)),
                  pl.BlockSpec((None, nq, t), lambda b, h, i: (h, b, 0))],
        out_specs=pl.BlockSpec((t, HEAD_DIM), lambda b, h, i: (b * nq + i, h)),
        out_shape=jax.ShapeDtypeStruct((n, width), BF16),
        scratch_shapes=[pltpu.VMEM((2, t, t), F32), pltpu.VMEM((2, t, LANES), F32),
                        stat, stat, pltpu.VMEM((t, HEAD_DIM), F32), stat],
        compiler_params=_params("parallel", "parallel", "arbitrary"),
        name="fox_attention",
    )(qkv, qkv, qkv, cum, cum_t.reshape(heads, n // t, t))


SB_DEAD_LOG = -110.0


def _sb_kernel(q_ref, k_ref, v_ref, u_ref, o_ref, e_buf, rs_buf, c_sc, acc_sc, *, t):
    qi = pl.program_id(2)
    c_sc[...] = jnp.zeros_like(c_sc)
    acc_sc[...] = jnp.zeros_like(acc_sc)

    def score(jj, diagonal):
        ks = pl.multiple_of((qi - jj) * t, t)
        z = _dot_nt(q_ref[...], k_ref[pl.ds(ks, t), :])
        softplus = jnp.log(1.0 + jnp.exp(-jnp.abs(z)))
        log_beta = jnp.minimum(z, 0.0) - softplus
        log_rest = jnp.minimum(-z, 0.0) - softplus
        if diagonal:
            row = lax.broadcasted_iota(jnp.int32, (t, t), 0)
            col = lax.broadcasted_iota(jnp.int32, (t, t), 1)
            strict = col < row
            log_rest = jnp.where(strict, log_rest, 0.0)
        hi = log_rest.astype(BF16)
        lo = (log_rest - hi.astype(F32)).astype(BF16)
        u = u_ref[...]
        cw = u.shape[0]
        later, tail = [], None
        for c in reversed(range(t // cw)):
            sl = slice(c * cw, (c + 1) * cw)
            inside = jnp.dot(hi[:, sl], u, preferred_element_type=F32) + jnp.dot(lo[:, sl], u, preferred_element_type=F32)
            later.insert(0, inside if tail is None else inside + _rep(tail, cw))
            chunk_sum = jnp.broadcast_to(
                jnp.sum(_lane_fold(log_rest[:, sl], jnp.add), axis=1, keepdims=True), (t, LANES))
            tail = chunk_sum if tail is None else tail + chunk_sum
        e = log_beta + jnp.concatenate(later, axis=1)
        if diagonal:
            e = jnp.where(strict, e, NEG_INF)
        e_buf[jj % 2] = e
        rs_buf[jj % 2] = tail

    def update(jj):
        ks = pl.multiple_of((qi - jj) * t, t)
        c = c_sc[...]
        w = jnp.exp(e_buf[jj % 2] + _rep(c, t))
        v = v_ref[pl.ds(ks, t), :]
        acc_sc[...] += jnp.dot(w.astype(v.dtype), v, preferred_element_type=F32)
        c_sc[...] = c + rs_buf[jj % 2]

    score(0, True)

    def body(state):
        jj, _ = state
        update(jj)
        score(jj + 1, False)
        return jj + 1, jnp.max(c_sc[...])

    jj, c_max = lax.while_loop(lambda st: (st[0] < qi) & (st[1] > SB_DEAD_LOG), body, (0, jnp.float32(0.0)))
    pl.when(c_max > SB_DEAD_LOG)(lambda: update(jj))
    o_ref[...] = acc_sc[...].astype(o_ref.dtype)


def _sb_attention(qkv, batch, seq, width):
    n = qkv.shape[0]
    heads = width // HEAD_DIM
    t = _tile(512, seq)
    nq = seq // t
    q_spec, k_spec, v_spec = _attn_specs(seq, t, width, 3, 4, 5)
    cw = min(256, t)
    upper = (jnp.arange(cw)[:, None] > jnp.arange(cw)[None, :]).astype(BF16)
    return pl.pallas_call(
        functools.partial(_sb_kernel, t=t),
        grid=(batch, heads, nq),
        in_specs=[q_spec(), k_spec(), v_spec, pl.BlockSpec((cw, cw), lambda b, h, i: (0, 0))],
        out_specs=pl.BlockSpec((t, HEAD_DIM), lambda b, h, i: (b * nq + i, h)),
        out_shape=jax.ShapeDtypeStruct((n, width), BF16),
        scratch_shapes=[pltpu.VMEM((2, t, t), F32), pltpu.VMEM((2, t, LANES), F32),
                        pltpu.VMEM((t, LANES), F32), pltpu.VMEM((t, HEAD_DIM), F32)],
        compiler_params=_params("parallel", "parallel", "arbitrary"),
        name="stick_breaking_attention",
    )(qkv, qkv, qkv, upper)


def _band_kernel(q_ref, k_ref, v_ref, tab_ref, o_ref, bias_sc, s_buf, mx_buf, m_sc, l_sc, acc_sc, *, t, n_back):
    qi = pl.program_id(2)

    @pl.when(qi == 0)
    def _():
        width = tab_ref.shape[-1]
        rolled = pltpu.roll(jnp.broadcast_to(tab_ref[...], (t, width)), 0, 1, stride=1, stride_axis=0)
        row = lax.broadcasted_iota(jnp.int32, (t, t), 0)
        col = lax.broadcasted_iota(jnp.int32, (t, t), 1)
        for d in range(n_back + 1):
            gap = (row + d * t) // CHUNK - col // CHUNK
            visible = (gap >= 0) & (gap <= CH_LEFT_CHUNKS)
            c0 = (n_back + 1 - d) * t
            bias_sc[d] = jnp.where(visible, rolled[:, c0:c0 + t], NEG_INF)

    _softmax_init(m_sc, l_sc, acc_sc)

    def score(d):
        ks = pl.multiple_of((qi - d) * t, t)
        s = _dot_nt(q_ref[...], k_ref[pl.ds(ks, t), :]) + bias_sc[d]
        s_buf[d % 2] = s
        mx_buf[d % 2] = _row_max(s)

    def update(d):
        ks = pl.multiple_of((qi - d) * t, t)
        _softmax_update(s_buf[d % 2], mx_buf[d % 2], v_ref[pl.ds(ks, t), :], m_sc, l_sc, acc_sc)

    def body(d, carry):
        update(d)
        score(d + 1)
        return carry

    n_tiles = jnp.minimum(qi, n_back) + 1
    score(0)
    lax.fori_loop(0, n_tiles - 1, body, 0)
    update(n_tiles - 1)
    o_ref[...] = _softmax_result(l_sc, acc_sc).astype(o_ref.dtype)


def _band_table(rel_bias, t, n_back):
    rel = (n_back + 1) * t - jnp.arange((n_back + 2) * t)
    return rel_bias[:, None, jnp.clip(rel, -REL_CLIP, REL_CLIP) + REL_CLIP].astype(F32)


def _band_attention(qkv, rel_bias, batch, seq, width):
    n = qkv.shape[0]
    heads = width // HEAD_DIM
    t = _tile(512, seq)
    nq = seq // t
    n_back = -(-(CH_LEFT_CHUNKS * CHUNK) // t)
    table = _band_table(rel_bias, t, n_back)
    q_spec, k_spec, v_spec = _attn_specs(seq, t, width, 6, 7, 8)
    stat = pltpu.VMEM((t, LANES), F32)
    return pl.pallas_call(
        functools.partial(_band_kernel, t=t, n_back=n_back),
        grid=(batch, heads, nq),
        in_specs=[q_spec(), k_spec(), v_spec,
                  pl.BlockSpec((None, 1, table.shape[-1]), lambda b, h, i: (h, 0, 0))],
        out_specs=pl.BlockSpec((t, HEAD_DIM), lambda b, h, i: (b * nq + i, h)),
        out_shape=jax.ShapeDtypeStruct((n, width), BF16),
        scratch_shapes=[pltpu.VMEM((n_back + 1, t, t), F32), pltpu.VMEM((2, t, t), F32),
                        pltpu.VMEM((2, t, LANES), F32), stat, stat, pltpu.VMEM((t, HEAD_DIM), F32)],
        compiler_params=_params("parallel", "parallel", "arbitrary"),
        name="chunk_band_attention",
    )(qkv, qkv, qkv, table)


def _diff_kernel(q1_ref, q2_ref, k1_ref, k2_ref, v_ref, lam_ref, g_ref, o_ref, s_buf, mx_buf,
                 m1_sc, l1_sc, a1_sc, m2_sc, l2_sc, a2_sc, *, t, lam_init):
    qi = pl.program_id(2)
    _softmax_init(m1_sc, l1_sc, a1_sc)
    _softmax_init(m2_sc, l2_sc, a2_sc)
    streams = ((q1_ref, k1_ref, m1_sc, l1_sc, a1_sc), (q2_ref, k2_ref, m2_sc, l2_sc, a2_sc))

    def score(j, masked):
        ks = pl.multiple_of(j * t, t)
        for i, (q_ref, k_ref, _, _, _) in enumerate(streams):
            s = _dot_nt(q_ref[...], k_ref[pl.ds(ks, t), :])
            if masked:
                row = lax.broadcasted_iota(jnp.int32, (t, t), 0) // CHUNK
                col = lax.broadcasted_iota(jnp.int32, (t, t), 1) // CHUNK
                s = jnp.where(col <= row, s, NEG_INF)
            s_buf[j % 2, i] = s
            mx_buf[j % 2, i] = _row_max(s)

    def update(j):
        ks = pl.multiple_of(j * t, t)
        v = v_ref[pl.ds(ks, t), :]
        for i, (_, _, m_sc, l_sc, a_sc) in enumerate(streams):
            _softmax_update(s_buf[j % 2, i], mx_buf[j % 2, i], v, m_sc, l_sc, a_sc)

    _pipelined_causal(qi, score, update)

    lv = lam_ref[...]
    lam = (jnp.exp(jnp.sum(lv[0:1] * lv[1:2], axis=1, keepdims=True))
           - jnp.exp(jnp.sum(lv[2:3] * lv[3:4], axis=1, keepdims=True)) + lam_init)
    o = _softmax_result(l1_sc, a1_sc) - lam * _softmax_result(l2_sc, a2_sc)
    y = o * lax.rsqrt(jnp.mean(o * o, axis=-1, keepdims=True) + RMS_EPS) * g_ref[...]
    o_ref[...] = (y * (1.0 - lam_init)).astype(o_ref.dtype)


def _diff_attention(qkv, lam_vecs, subln, lam_init, batch, seq, width):
    n = qkv.shape[0]
    heads = width // (2 * HEAD_DIM)
    t = _tile(512, seq)
    nq = seq // t
    q_spec, k_spec, v_spec = _attn_specs(seq, t, width, 9, 10, 11, v_mult=2, q_mult=2)
    scratch = [pltpu.VMEM((t, LANES), F32), pltpu.VMEM((t, LANES), F32), pltpu.VMEM((t, 2 * HEAD_DIM), F32)]
    stage = [pltpu.VMEM((2, 2, t, t), F32), pltpu.VMEM((2, 2, t, LANES), F32)]
    return pl.pallas_call(
        functools.partial(_diff_kernel, t=t, lam_init=lam_init),
        grid=(batch, heads, nq),
        in_specs=[q_spec(0), q_spec(1), k_spec(0), k_spec(1), v_spec,
                  pl.BlockSpec(lam_vecs.shape, lambda b, h, i: (0, 0)),
                  pl.BlockSpec((1, 2 * HEAD_DIM), lambda b, h, i: (0, 0))],
        out_specs=pl.BlockSpec((t, 2 * HEAD_DIM), lambda b, h, i: (b * nq + i, h)),
        out_shape=jax.ShapeDtypeStruct((n, width), BF16),
        scratch_shapes=stage + scratch + scratch,
        compiler_params=_params("parallel", "parallel", "arbitrary"),
        name="differential_attention",
    )(qkv, qkv, qkv, qkv, qkv, lam_vecs, subln.reshape(1, -1))


def _merge_kernel(*refs):
    br, wb, gates, o_ref = refs[0:4], refs[4], refs[5:9], refs[9]
    acc = None
    for i in range(N_BRANCHES):
        term = gates[i][...].astype(F32) * jnp.dot(br[i][...], wb[i], preferred_element_type=F32)
        acc = term if acc is None else acc + term
    o_ref[...] = acc.astype(o_ref.dtype)


def _gated_merge(branches, w_branch, gates):
    n, width = branches[0].shape
    d = w_branch.shape[-1]
    tm, tn = _tile(1024, n), _tile(512, d)
    col_tiles = d // tn
    br_spec = pl.BlockSpec((tm, width), lambda i, j: (i, 0))
    gate_specs = [pl.BlockSpec((tm, tn), lambda i, j, g=g: (i, g * col_tiles + j)) for g in range(N_BRANCHES)]
    return pl.pallas_call(
        _merge_kernel,
        grid=(n // tm, col_tiles),
        in_specs=[br_spec] * N_BRANCHES + [pl.BlockSpec((N_BRANCHES, width, tn), lambda i, j: (0, 0, j))] + gate_specs,
        out_specs=pl.BlockSpec((tm, tn), lambda i, j: (i, j)),
        out_shape=jax.ShapeDtypeStruct((n, d), BF16),
        compiler_params=_params("parallel", "arbitrary"),
        name="gated_merge",
    )(*branches, w_branch, *([gates] * N_BRANCHES))


def _router_kernel(x_ref, g_ref, rw_ref, rb_ref, idx_ref, wt_ref, *, n_experts):
    x = x_ref[...]
    hn = x * lax.rsqrt(jnp.mean(x * x, axis=-1, keepdims=True) + RMS_EPS) * g_ref[...]
    logits = None
    for wi, wp in enumerate(_split3(rw_ref[...])):
        for hi, hp in enumerate(_split3(hn)):
            if wi + hi <= 2:
                term = _dot_nt(wp, hp)
                logits = term if logits is None else logits + term
    score = jax.nn.sigmoid(logits)
    biased = score + rb_ref[...]
    per = n_experts // N_GROUPS
    rows = [biased[e:e + 1, :] for e in range(n_experts)]
    group_score = []
    for g in range(N_GROUPS):
        r = rows[g * per:(g + 1) * per]
        top1 = functools.reduce(jnp.maximum, r)
        first = []
        taken = jnp.zeros_like(top1, dtype=jnp.bool_)
        for v in r:
            is1 = (v == top1) & ~taken
            taken = taken | is1
            first.append(is1)
        rest = [jnp.where(f, NEG_INF, v) for f, v in zip(first, r)]
        top2 = functools.reduce(jnp.maximum, rest)
        second = []
        taken = jnp.zeros_like(top1, dtype=jnp.bool_)
        for v in rest:
            is2 = (v == top2) & ~taken
            taken = taken | is2
            second.append(is2)
        group_score.append((top1 + top2, [a | b for a, b in zip(first, second)]))
    best = functools.reduce(jnp.maximum, [gs for gs, _ in group_score])
    taken = jnp.zeros_like(best, dtype=jnp.bool_)
    picked = []
    for gs, sel in group_score:
        chosen = (gs == best) & ~taken
        taken = taken | chosen
        picked += [s & chosen for s in sel]
    w = [jnp.where(p, score[e:e + 1, :], 0.0) for e, p in enumerate(picked)]
    total = functools.reduce(jnp.add, w)
    inv = 1.0 / total
    lo = functools.reduce(jnp.minimum, [jnp.where(p, e, n_experts) for e, p in enumerate(picked)])
    hi = functools.reduce(jnp.maximum, [jnp.where(p, e, -1) for e, p in enumerate(picked)])
    idx_ref[0:1, :] = lo
    idx_ref[1:2, :] = hi
    wt_ref[0:1, :] = functools.reduce(jnp.add, [jnp.where(lo == e, we, 0.0) for e, we in enumerate(w)]) * inv
    wt_ref[1:2, :] = functools.reduce(jnp.add, [jnp.where(hi == e, we, 0.0) for e, we in enumerate(w)]) * inv


def _router(x, gain, router_w, router_bias):
    n, d = x.shape
    e = router_w.shape[1]
    tm = _tile(256, n)
    return pl.pallas_call(
        functools.partial(_router_kernel, n_experts=e),
        grid=(n // tm,),
        in_specs=[pl.BlockSpec((tm, d), lambda i: (i, 0)), pl.BlockSpec((1, d), lambda i: (0, 0)),
                  pl.BlockSpec((e, d), lambda i: (0, 0)), pl.BlockSpec((e, 1), lambda i: (0, 0))],
        out_specs=[pl.BlockSpec((2, tm), lambda i: (0, i)), pl.BlockSpec((2, tm), lambda i: (0, i))],
        out_shape=[jax.ShapeDtypeStruct((2, n), jnp.int32), jax.ShapeDtypeStruct((2, n), F32)],
        compiler_params=_params("parallel"),
        name="ffn_router",
    )(x, gain.reshape(1, d), router_w.T, router_bias.reshape(e, 1))


MOE_TILE = 256


def _moe_plan(idx, n_experts, tm):
    n = idx.shape[1]
    pairs = 2 * n
    n_rows = pairs + n_experts * tm
    expert = idx.reshape(pairs)
    onehot = (expert[:, None] == jnp.arange(n_experts)[None, :]).astype(jnp.int32)
    rank = jnp.sum((jnp.cumsum(onehot, axis=0) - onehot) * onehot, axis=1)
    group = -(-jnp.sum(onehot, axis=0) // tm) * tm
    group_end = jnp.cumsum(group)
    pos = (group_end - group)[expert] + rank
    pair_of_row = jnp.full((n_rows,), -1, jnp.int32).at[pos].set(jnp.arange(pairs, dtype=jnp.int32))
    is_pad = pair_of_row < 0
    spare = pairs + jnp.cumsum(is_pad.astype(jnp.int32)) - 1
    dst = jnp.where(is_pad, spare, pair_of_row)
    src = jnp.where(is_pad, 0, pair_of_row % n)
    tile_expert = jnp.minimum(
        jnp.searchsorted(group_end, jnp.arange(n_rows // tm, dtype=jnp.int32) * tm, side="right"), n_experts - 1)
    return src.reshape(-1, tm), dst.reshape(-1, tm), tile_expert.astype(jnp.int32)


def _expert_kernel(te_ref, src_ref, src_next_ref, dst_ref, x_hbm, g_ref, wgu_ref, wd_ref, y_hbm,
                   xbuf, ybuf, gsem, ssem, *, tm, d_expert):
    t, n_tiles = pl.program_id(0), pl.num_programs(0)
    slot = t % 2

    def row_copy_in(ids_ref, s, r):
        return pltpu.make_async_copy(x_hbm.at[pl.ds(ids_ref[0, r], 1), :], xbuf.at[s, pl.ds(r, 1), :], gsem.at[s])

    def row_copy_out(s, r):
        return pltpu.make_async_copy(ybuf.at[s, pl.ds(r, 1), :], y_hbm.at[pl.ds(dst_ref[0, r], 1), :], ssem.at[s])

    def each_row(fn):
        lax.fori_loop(0, tm, lambda r, c: (fn(r), c)[1], 0, unroll=8)

    def gather(ids_ref, s):
        each_row(lambda r: row_copy_in(ids_ref, s, r).start())

    def gather_wait(s):
        each_row(lambda r: row_copy_in(src_ref, s, r).wait())

    def scatter_start(s):
        each_row(lambda r: row_copy_out(s, r).start())

    def scatter_wait(s):
        each_row(lambda r: row_copy_out(s, r).wait())

    @pl.when(t == 0)
    def _():
        gather(src_ref, 0)

    @pl.when(t + 1 < n_tiles)
    def _():
        gather(src_next_ref, 1 - slot)

    gather_wait(slot)
    x = xbuf[slot]
    xn = (x * lax.rsqrt(jnp.mean(x * x, axis=-1, keepdims=True) + RMS_EPS) * g_ref[...]).astype(BF16)
    h = jnp.dot(xn, wgu_ref[...], preferred_element_type=F32)
    hg, hu = h[:, :d_expert], h[:, d_expert:]
    act = (hg * jax.nn.sigmoid(hg) * hu).astype(BF16)

    @pl.when(t >= 2)
    def _():
        scatter_wait(slot)

    ybuf[slot] = jnp.dot(act, wd_ref[...], preferred_element_type=F32)
    scatter_start(slot)

    @pl.when(t == n_tiles - 1)
    def _():
        scatter_wait(slot)

        @pl.when(t >= 1)
        def _():
            scatter_wait(1 - slot)


def _moe_experts(x, gain, idx, wgu, wd):
    n, d = x.shape
    n_experts, _, two_f = wgu.shape
    tm = MOE_TILE
    src, dst, tile_expert = _moe_plan(idx, n_experts, tm)
    n_tiles = src.shape[0]
    src, dst = src.reshape(n_tiles, 1, tm), dst.reshape(n_tiles, 1, tm)
    smem = lambda fn: pl.BlockSpec((None, 1, tm), fn, memory_space=pltpu.SMEM)
    grid_spec = pltpu.PrefetchScalarGridSpec(
        num_scalar_prefetch=1,
        grid=(n_tiles,),
        in_specs=[smem(lambda t, te: (t, 0, 0)),
                  smem(lambda t, te: (jnp.minimum(t + 1, n_tiles - 1), 0, 0)),
                  smem(lambda t, te: (t, 0, 0)),
                  pl.BlockSpec(memory_space=pl.ANY),
                  pl.BlockSpec((1, d), lambda t, te: (0, 0)),
                  pl.BlockSpec((None, d, two_f), lambda t, te: (te[t], 0, 0)),
                  pl.BlockSpec((None, two_f // 2, d), lambda t, te: (te[t], 0, 0))],
        out_specs=pl.BlockSpec(memory_space=pl.ANY),
        scratch_shapes=[pltpu.VMEM((2, tm, d), F32), pltpu.VMEM((2, tm, d), F32),
                        pltpu.SemaphoreType.DMA((2,)), pltpu.SemaphoreType.DMA((2,))])
    return pl.pallas_call(
        functools.partial(_expert_kernel, tm=tm, d_expert=two_f // 2),
        grid_spec=grid_spec,
        out_shape=jax.ShapeDtypeStruct((n_tiles * tm, d), F32),
        compiler_params=_params("arbitrary"),
        name="moe_experts",
    )(tile_expert, src, src, dst, x, gain.reshape(1, d), wgu, wd)


def _combine_kernel(x_ref, ya_ref, yb_ref, w_ref, o_ref):
    w = w_ref[...]
    o_ref[...] = x_ref[...] + w[:, 0:1] * ya_ref[...] + w[:, 1:2] * yb_ref[...]


def _moe_combine(x, y, wts):
    n, d = x.shape
    tm = _tile(128, n)
    nt = n // tm
    return pl.pallas_call(
        _combine_kernel,
        grid=(nt,),
        in_specs=[pl.BlockSpec((tm, d), lambda i: (i, 0)), pl.BlockSpec((tm, d), lambda i: (i, 0)),
                  pl.BlockSpec((tm, d), lambda i: (nt + i, 0)), pl.BlockSpec((tm, 2), lambda i: (i, 0))],
        out_specs=pl.BlockSpec((tm, d), lambda i: (i, 0)),
        out_shape=jax.ShapeDtypeStruct((n, d), F32),
        compiler_params=_params("parallel"),
        name="moe_combine",
    )(x, y, y, wts)


def _rope_tables(seq):
    half = HEAD_DIM // 2
    inv_freq = jnp.power(ROPE_THETA, -jnp.arange(half, dtype=F32) / half)
    ang = jnp.arange(seq, dtype=F32)[:, None] * inv_freq[None, :]
    cos, sin = jnp.cos(ang), jnp.sin(ang)
    return jnp.concatenate([cos, cos], axis=-1), jnp.concatenate([-sin, sin], axis=-1)


def kernel(x, attn_norm, w_in, fox_forget_bias, fox_q_norm, fox_k_norm, ch_q_norm, ch_k_norm, ch_rel_bias,
           diff_q_norm, diff_k_norm, diff_lambda, diff_subln, w_branch, w_out, ffn_norm, router_w, router_bias,
           w_gate, w_up, w_down):
    batch, seq, d_model = x.shape
    depth = w_in.shape[0]
    width = d_model // N_BRANCHES
    fox_heads = width // HEAD_DIM
    qkv_cols = 3 * width
    n_experts, _, d_expert = w_gate.shape[1:]
    n = batch * seq
    scale = HEAD_DIM ** -0.5
    assert fox_heads <= LANES and width % (2 * HEAD_DIM) == 0 and seq % CHUNK == 0

    cos_t, sin_t = _rope_tables(seq)
    ones = jnp.ones((HEAD_DIM,), F32)
    res_spec = lambda tm, tn: pl.BlockSpec((tm, tn), lambda i, j: (i, j))
    xf = x.reshape(n, d_model)

    for layer in range(depth):
        wl = w_in[layer]
        s_f, s_sb, s_gate = qkv_cols, qkv_cols + fox_heads, 4 * qkv_cols + fox_heads
        w_qkv = jnp.concatenate([wl[:, :s_f], wl[:, s_sb:s_gate]], axis=1).astype(BF16)
        w_f = jnp.pad(wl[:, s_f:s_sb], ((0, 0), (0, LANES - fox_heads))).astype(BF16)
        w_g = wl[:, s_gate:].astype(BF16)
        f_bias = jnp.pad(fox_forget_bias[layer], (0, LANES - fox_heads)).reshape(1, LANES)
        gains = jnp.stack([
            fox_q_norm[layer] * scale, fox_k_norm[layer], ones,
            ones * scale, ones, ones,
            ch_q_norm[layer] * scale, ch_k_norm[layer], ones,
            diff_q_norm[layer] * scale, diff_k_norm[layer], ones,
            ones, ones, ones, ones])

        h = _rmsnorm(xf, attn_norm[layer])
        qkv = _qkv_proj(h, w_qkv, gains, cos_t, sin_t, seq, width)
        gates = _matmul(h, w_g, _epi_sigmoid, BF16, name="gate_proj")
        cum = _forget_cumsum(h, w_f, f_bias, batch, seq)
        cum_t = cum[:, :fox_heads].T

        lam_init = 0.8 - 0.6 * math.exp(-0.3 * layer)
        branches = (
            _fox_attention(qkv, cum, cum_t, batch, seq, width),
            _sb_attention(qkv, batch, seq, width),
            _band_attention(qkv, ch_rel_bias[layer], batch, seq, width),
            _diff_attention(qkv, diff_lambda[layer], diff_subln[layer], lam_init, batch, seq, width),
        )
        merged = _gated_merge(branches, w_branch[layer].astype(BF16), gates)
        xf = _matmul(merged, w_out[layer].astype(BF16), _epi_residual, F32, extras=(xf,), extra_specs=(res_spec,),
                     tn=512, name="out_proj")

        idx, wts = _router(xf, ffn_norm[layer], router_w, router_bias)
        wgu = jnp.concatenate([w_gate[layer], w_up[layer]], axis=2).astype(BF16)
        y = _moe_experts(xf, ffn_norm[layer], idx, wgu, w_down[layer].astype(BF16))
        xf = _moe_combine(xf, y, wts.T)
    return xf.reshape(batch, seq, d_model)
```

```python
import functools
import math

import jax
import jax.numpy as jnp
from jax import lax
from jax.experimental import pallas as pl
from jax.experimental.pallas import tpu as pltpu

HEAD_DIM = 128
LANES = 128
CHUNK = 64
N_BRANCHES = 4
CH_LEFT_CHUNKS = 8
REL_CLIP = 128
ROPE_THETA = 10000.0
N_GROUPS = 4
RMS_EPS = 1e-6
NEG_INF = float("-inf")
VMEM_LIMIT_BYTES = 56 * 1024 * 1024

F32 = jnp.float32
BF16 = jnp.bfloat16


def _tile(pref, n):
    t = min(pref, n)
    while n % t:
        t //= 2
    return t


def _params(*sem):
    return pltpu.CompilerParams(dimension_semantics=sem, vmem_limit_bytes=VMEM_LIMIT_BYTES)


def _dot_nt(a, b):
    return lax.dot_general(a, b, (((1,), (1,)), ((), ())), preferred_element_type=F32)


def _log_sigmoid(x):
    return jnp.minimum(x, 0.0) - jnp.log1p(jnp.exp(-jnp.abs(x)))


FP8 = jnp.float8_e4m3fn
FP8_MAX = 448.0


def _rmsnorm_kernel(x_ref, g_ref, o_ref, o8_ref, s_ref):
    x = x_ref[...]
    ms = jnp.mean(x * x, axis=-1, keepdims=True)
    y = x * lax.rsqrt(ms + RMS_EPS) * g_ref[...]
    o_ref[...] = y.astype(o_ref.dtype)
    amax = jnp.max(jnp.abs(y), axis=-1, keepdims=True)
    o8_ref[...] = (y * jnp.where(amax > 0.0, FP8_MAX / amax, 0.0)).astype(o8_ref.dtype)
    s_ref[...] = jnp.broadcast_to(amax * (1.0 / FP8_MAX), s_ref.shape)


def _rmsnorm(x, gain):
    n, d = x.shape
    tm = _tile(256, n)
    row = pl.BlockSpec((tm, d), lambda i: (i, 0))
    return pl.pallas_call(
        _rmsnorm_kernel,
        grid=(n // tm,),
        in_specs=[row, pl.BlockSpec((1, d), lambda i: (0, 0))],
        out_specs=[row, row, pl.BlockSpec((tm, LANES), lambda i: (i, 0))],
        out_shape=[jax.ShapeDtypeStruct((n, d), BF16), jax.ShapeDtypeStruct((n, d), FP8),
                   jax.ShapeDtypeStruct((n, LANES), F32)],
        compiler_params=_params("parallel"),
        name="attn_rmsnorm",
    )(x, gain.reshape(1, d))


def _matmul(a, w, epilogue, out_dtype, extras=(), extra_specs=(), tm=1024, tn=1024, name="matmul"):
    m, k = a.shape
    _, n = w.shape
    tm, tn = _tile(tm, m), _tile(tn, n)

    def kern(a_ref, w_ref, *refs):
        acc = jnp.dot(a_ref[...], w_ref[...], preferred_element_type=F32)
        epilogue(acc, refs[:-1], refs[-1])

    return pl.pallas_call(
        kern,
        grid=(m // tm, n // tn),
        in_specs=[pl.BlockSpec((tm, k), lambda i, j: (i, 0)), pl.BlockSpec((k, tn), lambda i, j: (0, j)),
                  *[mk(tm, tn) for mk in extra_specs]],
        out_specs=pl.BlockSpec((tm, tn), lambda i, j: (i, j)),
        out_shape=jax.ShapeDtypeStruct((m, n), out_dtype),
        compiler_params=_params("parallel", "arbitrary"),
        name=name,
    )(a, w, *extras)


def _epi_scaled_sigmoid(acc, refs, o_ref):
    row_scale, col_scale = refs
    o_ref[...] = jax.nn.sigmoid(acc * row_scale[:, :1] * col_scale[...]).astype(o_ref.dtype)


def _epi_residual(acc, refs, o_ref):
    o_ref[...] = refs[0][...] + acc


PLAIN, NORM, NORM_ROPE = 0, 1, 2
SECTION_KINDS = (NORM, NORM, PLAIN, PLAIN, PLAIN, PLAIN, NORM, NORM, PLAIN, NORM_ROPE, NORM_ROPE, PLAIN)


def _qkv_epilogue(acc, refs, o_ref, *, sec_tiles):
    gain_ref, cos_ref, sin_ref = refs
    sec = pl.program_id(1) // sec_tiles
    g = gain_ref[pl.ds(sec, 1), :]
    tn = acc.shape[1]

    def run(kind):
        for hh in range(tn // HEAD_DIM):
            y = acc[:, hh * HEAD_DIM:(hh + 1) * HEAD_DIM]
            if kind != PLAIN:
                y = y * lax.rsqrt(jnp.mean(y * y, axis=-1, keepdims=True) + RMS_EPS)
            y = y * g
            if kind == NORM_ROPE:
                y = y * cos_ref[...] + pltpu.roll(y, HEAD_DIM // 2, 1) * sin_ref[...]
            o_ref[:, hh * HEAD_DIM:(hh + 1) * HEAD_DIM] = y.astype(o_ref.dtype)

    for kind in (PLAIN, NORM, NORM_ROPE):
        hit = functools.reduce(jnp.logical_or, [sec == s for s, kd in enumerate(SECTION_KINDS) if kd == kind])
        pl.when(hit)(functools.partial(run, kind))


def _qkv_proj(h, w_qkv, gains, cos_t, sin_t, seq, width):
    n = h.shape[0]
    tm = _tile(1024, seq)
    tn = _tile(1024, width)
    seq_tiles = seq // tm
    row_spec = lambda tm_, tn_: pl.BlockSpec((tm_, LANES), lambda i, j: (i % seq_tiles, 0))
    gain_spec = lambda tm_, tn_: pl.BlockSpec(gains.shape, lambda i, j: (0, 0))
    return _matmul(h, w_qkv, functools.partial(_qkv_epilogue, sec_tiles=width // tn), BF16,
                   extras=(gains, cos_t, sin_t), extra_specs=(gain_spec, row_spec, row_spec),
                   tm=tm, tn=tn, name="qkv_proj")


def _split3(x):
    hi = x.astype(BF16)
    r = x - hi.astype(F32)
    mid = r.astype(BF16)
    lo = (r - mid.astype(F32)).astype(BF16)
    return hi, mid, lo


def _fgate_kernel(h_ref, w_ref, b_ref, tri_ref, o_ref, carry_sc):
    @pl.when(pl.program_id(1) == 0)
    def _():
        carry_sc[...] = jnp.zeros_like(carry_sc)

    f = jnp.dot(h_ref[...], w_ref[...], preferred_element_type=F32) + b_ref[...]
    lf = _log_sigmoid(f)
    tri = tri_ref[...]
    c = carry_sc[...]
    for part in _split3(lf):
        c = c + jnp.dot(tri, part, preferred_element_type=F32)
    o_ref[...] = c
    carry_sc[...] = c[-1:, :]


def _forget_cumsum(h, w_f, bias, batch, seq):
    n, d = h.shape
    ts = _tile(512, seq)
    nt = seq // ts
    tri = (jnp.arange(ts)[:, None] >= jnp.arange(ts)[None, :]).astype(BF16)
    return pl.pallas_call(
        _fgate_kernel,
        grid=(batch, nt),
        in_specs=[pl.BlockSpec((ts, d), lambda b, t: (b * nt + t, 0)),
                  pl.BlockSpec((d, LANES), lambda b, t: (0, 0)),
                  pl.BlockSpec((1, LANES), lambda b, t: (0, 0)),
                  pl.BlockSpec((ts, ts), lambda b, t: (0, 0))],
        out_specs=pl.BlockSpec((ts, LANES), lambda b, t: (b * nt + t, 0)),
        out_shape=jax.ShapeDtypeStruct((n, LANES), F32),
        scratch_shapes=[pltpu.VMEM((1, LANES), F32)],
        compiler_params=_params("parallel", "arbitrary"),
        name="forget_cumsum",
    )(h, w_f, bias, tri)


def _softmax_step(s, v, m_sc, l_sc, acc_sc):
    m_prev = m_sc[...]
    m_new = jnp.maximum(m_prev, jnp.max(s, axis=1, keepdims=True))
    alpha = jnp.exp(m_prev - m_new)
    p = jnp.exp(s - m_new)
    l_sc[...] = alpha * l_sc[...] + jnp.sum(p, axis=1, keepdims=True)
    acc_sc[...] = alpha * acc_sc[...] + jnp.dot(p.astype(v.dtype), v, preferred_element_type=F32)
    m_sc[...] = m_new


def _softmax_init(m_sc, l_sc, acc_sc):
    m_sc[...] = jnp.full_like(m_sc, NEG_INF)
    l_sc[...] = jnp.zeros_like(l_sc)
    acc_sc[...] = jnp.zeros_like(acc_sc)


def _attn_specs(seq, tq, width, q_sec, k_sec, v_sec, v_mult=1, q_mult=1):
    nq = seq // tq
    sec = width // HEAD_DIM
    q_spec = lambda c=0: pl.BlockSpec((tq, HEAD_DIM), lambda b, h, i: (b * nq + i, q_sec * sec + h * q_mult + c))
    k_spec = lambda c=0: pl.BlockSpec((seq, HEAD_DIM), lambda b, h, i: (b, k_sec * sec + h * q_mult + c))
    v_spec = pl.BlockSpec((seq, HEAD_DIM * v_mult), lambda b, h, i: (b, v_sec * sec // v_mult + h))
    return q_spec, k_spec, v_spec


def _rep(x, n):
    return x if n == LANES else jnp.concatenate([x] * (n // LANES), axis=1)


def _lane_fold(x, op):
    out = x[:, :LANES]
    for c in range(1, x.shape[1] // LANES):
        out = op(out, x[:, c * LANES:(c + 1) * LANES])
    return out


def _row_max(s):
    return jnp.broadcast_to(jnp.max(_lane_fold(s, jnp.maximum), axis=1, keepdims=True), (s.shape[0], LANES))


def _softmax_update(s, mx, v, m_sc, l_sc, acc_sc):
    m_prev = m_sc[...]
    m_new = jnp.maximum(m_prev, mx)
    alpha = jnp.exp(m_prev - m_new)
    p = jnp.exp(s - _rep(m_new, s.shape[1]))
    l_sc[...] = alpha * l_sc[...] + _lane_fold(p, jnp.add)
    acc_sc[...] = (_rep(alpha, acc_sc.shape[1]) * acc_sc[...]
                   + jnp.dot(p.astype(v.dtype), v, preferred_element_type=F32))
    m_sc[...] = m_new


def _softmax_result(l_sc, acc_sc):
    return acc_sc[...] * (1.0 / jnp.sum(l_sc[...], axis=1, keepdims=True))


def _pipelined_causal(qi, score, update):
    @pl.when(qi == 0)
    def _():
        score(0, True)

    @pl.when(qi > 0)
    def _():
        score(0, False)

    def body(j, carry):
        update(j)
        score(j + 1, False)
        return carry

    lax.fori_loop(0, qi - 1, body, 0)

    @pl.when(qi > 0)
    def _():
        update(qi - 1)
        score(qi, True)

    update(qi)


def _fox_kernel(q_ref, k_ref, v_ref, ct_ref, cs_ref, o_ref, s_buf, mx_buf, m_sc, l_sc, acc_sc, ct_sc, *, t):
    h, qi = pl.program_id(1), pl.program_id(2)
    lane = lax.broadcasted_iota(jnp.int32, ct_ref.shape, 1)
    ct = jnp.sum(jnp.where(lane == h, ct_ref[...], 0.0), axis=1, keepdims=True)
    ct_sc[...] = jnp.broadcast_to(ct, ct_sc.shape)
    _softmax_init(m_sc, l_sc, acc_sc)

    def score(j, masked):
        ks = pl.multiple_of(j * t, t)
        s = _dot_nt(q_ref[...], k_ref[pl.ds(ks, t), :])
        s = s + _rep(ct_sc[...], t) - cs_ref[pl.ds(j, 1), :]
        if masked:
            row = lax.broadcasted_iota(jnp.int32, (t, t), 0)
            col = lax.broadcasted_iota(jnp.int32, (t, t), 1)
            s = jnp.where(col <= row, s, NEG_INF)
        s_buf[j % 2] = s
        mx_buf[j % 2] = _row_max(s)

    def update(j):
        ks = pl.multiple_of(j * t, t)
        _softmax_update(s_buf[j % 2], mx_buf[j % 2], v_ref[pl.ds(ks, t), :], m_sc, l_sc, acc_sc)

    _pipelined_causal(qi, score, update)
    o_ref[...] = _softmax_result(l_sc, acc_sc).astype(o_ref.dtype)


def _fox_attention(qkv, cum, cum_t, batch, seq, width):
    n = qkv.shape[0]
    heads = width // HEAD_DIM
    t = _tile(512, seq)
    nq = seq // t
    q_spec, k_spec, v_spec = _attn_specs(seq, t, width, 0, 1, 2)
    stat = pltpu.VMEM((t, LANES), F32)
    return pl.pallas_call(
        functools.partial(_fox_kernel, t=t),
        grid=(batch, heads, nq),
        in_specs=[q_spec(), k_spec(), v_spec,
                  pl.BlockSpec((t, LANES), lambda b, h, i: (b * nq + i, 0)),
                  pl.BlockSpec((None, nq, t), lambda b, h, i: (h, b, 0))],
        out_specs=pl.BlockSpec((t, HEAD_DIM), lambda b, h, i: (b * nq + i, h)),
        out_shape=jax.ShapeDtypeStruct((n, width), BF16),
        scratch_shapes=[pltpu.VMEM((2, t, t), F32), pltpu.VMEM((2, t, LANES), F32),
                        stat, stat, pltpu.VMEM((t, HEAD_DIM), F32), stat],
        compiler_params=_params("parallel", "parallel", "arbitrary"),
        name="fox_attention",
    )(qkv, qkv, qkv, cum, cum_t.reshape(heads, n // t, t))


SB_DEAD_LOG = -110.0


def _sb_kernel(q_ref, k_ref, v_ref, u_ref, o_ref, e_buf, rs_buf, c_sc, acc_sc, *, t):
    qi = pl.program_id(2)
    c_sc[...] = jnp.zeros_like(c_sc)
    acc_sc[...] = jnp.zeros_like(acc_sc)

    def score(jj, diagonal):
        ks = pl.multiple_of((qi - jj) * t, t)
        z = _dot_nt(q_ref[...], k_ref[pl.ds(ks, t), :])
        softplus = jnp.log(1.0 + jnp.exp(-jnp.abs(z)))
        log_beta = jnp.minimum(z, 0.0) - softplus
        log_rest = jnp.minimum(-z, 0.0) - softplus
        if diagonal:
            row = lax.broadcasted_iota(jnp.int32, (t, t), 0)
            col = lax.broadcasted_iota(jnp.int32, (t, t), 1)
            strict = col < row
            log_rest = jnp.where(strict, log_rest, 0.0)
        hi = log_rest.astype(BF16)
        lo = (log_rest - hi.astype(F32)).astype(BF16)
        u = u_ref[...]
        cw = u.shape[0]
        later, tail = [], None
        for c in reversed(range(t // cw)):
            sl = slice(c * cw, (c + 1) * cw)
            inside = jnp.dot(hi[:, sl], u, preferred_element_type=F32) + jnp.dot(lo[:, sl], u, preferred_element_type=F32)
            later.insert(0, inside if tail is None else inside + _rep(tail, cw))
            chunk_sum = jnp.broadcast_to(
                jnp.sum(_lane_fold(log_rest[:, sl], jnp.add), axis=1, keepdims=True), (t, LANES))
            tail = chunk_sum if tail is None else tail + chunk_sum
        e = log_beta + jnp.concatenate(later, axis=1)
        if diagonal:
            e = jnp.where(strict, e, NEG_INF)
        e_buf[jj % 2] = e
        rs_buf[jj % 2] = tail

    def update(jj):
        ks = pl.multiple_of((qi - jj) * t, t)
        c = c_sc[...]
        w = jnp.exp(e_buf[jj % 2] + _rep(c, t))
        v = v_ref[pl.ds(ks, t), :]
        acc_sc[...] += jnp.dot(w.astype(v.dtype), v, preferred_element_type=F32)
        c_sc[...] = c + rs_buf[jj % 2]

    score(0, True)

    def body(state):
        jj, _ = state
        update(jj)
        score(jj + 1, False)
        return jj + 1, jnp.max(c_sc[...])

    jj, c_max = lax.while_loop(lambda st: (st[0] < qi) & (st[1] > SB_DEAD_LOG), body, (0, jnp.float32(0.0)))
    pl.when(c_max > SB_DEAD_LOG)(lambda: update(jj))
    o_ref[...] = acc_sc[...].astype(o_ref.dtype)


def _sb_attention(qkv, batch, seq, width):
    n = qkv.shape[0]
    heads = width // HEAD_DIM
    t = _tile(512, seq)
    nq = seq // t
    q_spec, k_spec, v_spec = _attn_specs(seq, t, width, 3, 4, 5)
    cw = min(256, t)
    upper = (jnp.arange(cw)[:, None] > jnp.arange(cw)[None, :]).astype(BF16)
    return pl.pallas_call(
        functools.partial(_sb_kernel, t=t),
        grid=(batch, heads, nq),
        in_specs=[q_spec(), k_spec(), v_spec, pl.BlockSpec((cw, cw), lambda b, h, i: (0, 0))],
        out_specs=pl.BlockSpec((t, HEAD_DIM), lambda b, h, i: (b * nq + i, h)),
        out_shape=jax.ShapeDtypeStruct((n, width), BF16),
        scratch_shapes=[pltpu.VMEM((2, t, t), F32), pltpu.VMEM((2, t, LANES), F32),
                        pltpu.VMEM((t, LANES), F32), pltpu.VMEM((t, HEAD_DIM), F32)],
        compiler_params=_params("parallel", "parallel", "arbitrary"),
        name="stick_breaking_attention",
    )(qkv, qkv, qkv, upper)


def _band_kernel(q_ref, k_ref, v_ref, tab_ref, o_ref, bias_sc, s_buf, mx_buf, m_sc, l_sc, acc_sc, *, t, n_back):
    qi = pl.program_id(2)

    @pl.when(qi == 0)
    def _():
        width = tab_ref.shape[-1]
        rolled = pltpu.roll(jnp.broadcast_to(tab_ref[...], (t, width)), 0, 1, stride=1, stride_axis=0)
        row = lax.broadcasted_iota(jnp.int32, (t, t), 0)
        col = lax.broadcasted_iota(jnp.int32, (t, t), 1)
        for d in range(n_back + 1):
            gap = (row + d * t) // CHUNK - col // CHUNK
            visible = (gap >= 0) & (gap <= CH_LEFT_CHUNKS)
            c0 = (n_back + 1 - d) * t
            bias_sc[d] = jnp.where(visible, rolled[:, c0:c0 + t], NEG_INF)

    _softmax_init(m_sc, l_sc, acc_sc)

    def score(d):
        ks = pl.multiple_of((qi - d) * t, t)
        s = _dot_nt(q_ref[...], k_ref[pl.ds(ks, t), :]) + bias_sc[d]
        s_buf[d % 2] = s
        mx_buf[d % 2] = _row_max(s)

    def update(d):
        ks = pl.multiple_of((qi - d) * t, t)
        _softmax_update(s_buf[d % 2], mx_buf[d % 2], v_ref[pl.ds(ks, t), :], m_sc, l_sc, acc_sc)

    def body(d, carry):
        update(d)
        score(d + 1)
        return carry

    n_tiles = jnp.minimum(qi, n_back) + 1
    score(0)
    lax.fori_loop(0, n_tiles - 1, body, 0)
    update(n_tiles - 1)
    o_ref[...] = _softmax_result(l_sc, acc_sc).astype(o_ref.dtype)


def _band_table(rel_bias, t, n_back):
    rel = (n_back + 1) * t - jnp.arange((n_back + 2) * t)
    return rel_bias[:, None, jnp.clip(rel, -REL_CLIP, REL_CLIP) + REL_CLIP].astype(F32)


def _band_attention(qkv, rel_bias, batch, seq, width):
    n = qkv.shape[0]
    heads = width // HEAD_DIM
    t = _tile(512, seq)
    nq = seq // t
    n_back = -(-(CH_LEFT_CHUNKS * CHUNK) // t)
    table = _band_table(rel_bias, t, n_back)
    q_spec, k_spec, v_spec = _attn_specs(seq, t, width, 6, 7, 8)
    stat = pltpu.VMEM((t, LANES), F32)
    return pl.pallas_call(
        functools.partial(_band_kernel, t=t, n_back=n_back),
        grid=(batch, heads, nq),
        in_specs=[q_spec(), k_spec(), v_spec,
                  pl.BlockSpec((None, 1, table.shape[-1]), lambda b, h, i: (h, 0, 0))],
        out_specs=pl.BlockSpec((t, HEAD_DIM), lambda b, h, i: (b * nq + i, h)),
        out_shape=jax.ShapeDtypeStruct((n, width), BF16),
        scratch_shapes=[pltpu.VMEM((n_back + 1, t, t), F32), pltpu.VMEM((2, t, t), F32),
                        pltpu.VMEM((2, t, LANES), F32), stat, stat, pltpu.VMEM((t, HEAD_DIM), F32)],
        compiler_params=_params("parallel", "parallel", "arbitrary"),
        name="chunk_band_attention",
    )(qkv, qkv, qkv, table)


def _diff_kernel(q1_ref, q2_ref, k1_ref, k2_ref, v_ref, lam_ref, g_ref, o_ref, s_buf, mx_buf,
                 m1_sc, l1_sc, a1_sc, m2_sc, l2_sc, a2_sc, *, t, lam_init):
    qi = pl.program_id(2)
    _softmax_init(m1_sc, l1_sc, a1_sc)
    _softmax_init(m2_sc, l2_sc, a2_sc)
    streams = ((q1_ref, k1_ref, m1_sc, l1_sc, a1_sc), (q2_ref, k2_ref, m2_sc, l2_sc, a2_sc))

    def score(j, masked):
        ks = pl.multiple_of(j * t, t)
        for i, (q_ref, k_ref, _, _, _) in enumerate(streams):
            s = _dot_nt(q_ref[...], k_ref[pl.ds(ks, t), :])
            if masked:
                row = lax.broadcasted_iota(jnp.int32, (t, t), 0) // CHUNK
                col = lax.broadcasted_iota(jnp.int32, (t, t), 1) // CHUNK
                s = jnp.where(col <= row, s, NEG_INF)
            s_buf[j % 2, i] = s
            mx_buf[j % 2, i] = _row_max(s)

    def update(j):
        ks = pl.multiple_of(j * t, t)
        v = v_ref[pl.ds(ks, t), :]
        for i, (_, _, m_sc, l_sc, a_sc) in enumerate(streams):
            _softmax_update(s_buf[j % 2, i], mx_buf[j % 2, i], v, m_sc, l_sc, a_sc)

    _pipelined_causal(qi, score, update)

    lv = lam_ref[...]
    lam = (jnp.exp(jnp.sum(lv[0:1] * lv[1:2], axis=1, keepdims=True))
           - jnp.exp(jnp.sum(lv[2:3] * lv[3:4], axis=1, keepdims=True)) + lam_init)
    o = _softmax_result(l1_sc, a1_sc) - lam * _softmax_result(l2_sc, a2_sc)
    y = o * lax.rsqrt(jnp.mean(o * o, axis=-1, keepdims=True) + RMS_EPS) * g_ref[...]
    o_ref[...] = (y * (1.0 - lam_init)).astype(o_ref.dtype)


def _diff_attention(qkv, lam_vecs, subln, lam_init, batch, seq, width):
    n = qkv.shape[0]
    heads = width // (2 * HEAD_DIM)
    t = _tile(512, seq)
    nq = seq // t
    q_spec, k_spec, v_spec = _attn_specs(seq, t, width, 9, 10, 11, v_mult=2, q_mult=2)
    scratch = [pltpu.VMEM((t, LANES), F32), pltpu.VMEM((t, LANES), F32), pltpu.VMEM((t, 2 * HEAD_DIM), F32)]
    stage = [pltpu.VMEM((2, 2, t, t), F32), pltpu.VMEM((2, 2, t, LANES), F32)]
    return pl.pallas_call(
        functools.partial(_diff_kernel, t=t, lam_init=lam_init),
        grid=(batch, heads, nq),
        in_specs=[q_spec(0), q_spec(1), k_spec(0), k_spec(1), v_spec,
                  pl.BlockSpec(lam_vecs.shape, lambda b, h, i: (0, 0)),
                  pl.BlockSpec((1, 2 * HEAD_DIM), lambda b, h, i: (0, 0))],
        out_specs=pl.BlockSpec((t, 2 * HEAD_DIM), lambda b, h, i: (b * nq + i, h)),
        out_shape=jax.ShapeDtypeStruct((n, width), BF16),
        scratch_shapes=stage + scratch + scratch,
        compiler_params=_params("parallel", "parallel", "arbitrary"),
        name="differential_attention",
    )(qkv, qkv, qkv, qkv, qkv, lam_vecs, subln.reshape(1, -1))


def _merge_kernel(*refs):
    br, wb, gates, o_ref = refs[0:4], refs[4], refs[5:9], refs[9]
    acc = None
    for i in range(N_BRANCHES):
        term = gates[i][...].astype(F32) * jnp.dot(br[i][...], wb[i], preferred_element_type=F32)
        acc = term if acc is None else acc + term
    o_ref[...] = acc.astype(o_ref.dtype)


def _gated_merge(branches, w_branch, gates):
    n, width = branches[0].shape
    d = w_branch.shape[-1]
    tm, tn = _tile(1024, n), _tile(512, d)
    col_tiles = d // tn
    br_spec = pl.BlockSpec((tm, width), lambda i, j: (i, 0))
    gate_specs = [pl.BlockSpec((tm, tn), lambda i, j, g=g: (i, g * col_tiles + j)) for g in range(N_BRANCHES)]
    return pl.pallas_call(
        _merge_kernel,
        grid=(n // tm, col_tiles),
        in_specs=[br_spec] * N_BRANCHES + [pl.BlockSpec((N_BRANCHES, width, tn), lambda i, j: (0, 0, j))] + gate_specs,
        out_specs=pl.BlockSpec((tm, tn), lambda i, j: (i, j)),
        out_shape=jax.ShapeDtypeStruct((n, d), BF16),
        compiler_params=_params("parallel", "arbitrary"),
        name="gated_merge",
    )(*branches, w_branch, *([gates] * N_BRANCHES))


def _pack_halves(y):
    bits = lax.bitcast_convert_type(y.astype(BF16).astype(F32), jnp.uint32)
    half = y.shape[1] // 2
    return (bits[:, :half] >> 16) | bits[:, half:]


def _unpack_halves(p):
    lo = lax.bitcast_convert_type(p << 16, F32)
    hi = lax.bitcast_convert_type(p & jnp.uint32(0xFFFF0000), F32)
    return lo, hi


def _router_kernel(x_ref, g_ref, rw_ref, rb_ref, hp_ref, idx_ref, wt_ref, *, n_experts):
    x = x_ref[...]
    hn = x * lax.rsqrt(jnp.mean(x * x, axis=-1, keepdims=True) + RMS_EPS) * g_ref[...]
    hp_ref[...] = _pack_halves(hn)
    logits = None
    for wi, wp in enumerate(_split3(rw_ref[...])):
        for hi, hp in enumerate(_split3(hn)):
            if wi + hi <= 2:
                term = _dot_nt(wp, hp)
                logits = term if logits is None else logits + term
    score = jax.nn.sigmoid(logits)
    biased = score + rb_ref[...]
    per = n_experts // N_GROUPS
    rows = [biased[e:e + 1, :] for e in range(n_experts)]
    group_score = []
    for g in range(N_GROUPS):
        r = rows[g * per:(g + 1) * per]
        top1 = functools.reduce(jnp.maximum, r)
        first = []
        taken = jnp.zeros_like(top1, dtype=jnp.bool_)
        for v in r:
            is1 = (v == top1) & ~taken
            taken = taken | is1
            first.append(is1)
        rest = [jnp.where(f, NEG_INF, v) for f, v in zip(first, r)]
        top2 = functools.reduce(jnp.maximum, rest)
        second = []
        taken = jnp.zeros_like(top1, dtype=jnp.bool_)
        for v in rest:
            is2 = (v == top2) & ~taken
            taken = taken | is2
            second.append(is2)
        group_score.append((top1 + top2, [a | b for a, b in zip(first, second)]))
    best = functools.reduce(jnp.maximum, [gs for gs, _ in group_score])
    taken = jnp.zeros_like(best, dtype=jnp.bool_)
    picked = []
    for gs, sel in group_score:
        chosen = (gs == best) & ~taken
        taken = taken | chosen
        picked += [s & chosen for s in sel]
    w = [jnp.where(p, score[e:e + 1, :], 0.0) for e, p in enumerate(picked)]
    total = functools.reduce(jnp.add, w)
    inv = 1.0 / total
    lo = functools.reduce(jnp.minimum, [jnp.where(p, e, n_experts) for e, p in enumerate(picked)])
    hi = functools.reduce(jnp.maximum, [jnp.where(p, e, -1) for e, p in enumerate(picked)])
    idx_ref[0:1, :] = lo
    idx_ref[1:2, :] = hi
    wt_ref[0:1, :] = functools.reduce(jnp.add, [jnp.where(lo == e, we, 0.0) for e, we in enumerate(w)]) * inv
    wt_ref[1:2, :] = functools.reduce(jnp.add, [jnp.where(hi == e, we, 0.0) for e, we in enumerate(w)]) * inv


def _router(x, gain, router_w, router_bias):
    n, d = x.shape
    e = router_w.shape[1]
    tm = _tile(256, n)
    return pl.pallas_call(
        functools.partial(_router_kernel, n_experts=e),
        grid=(n // tm,),
        in_specs=[pl.BlockSpec((tm, d), lambda i: (i, 0)), pl.BlockSpec((1, d), lambda i: (0, 0)),
                  pl.BlockSpec((e, d), lambda i: (0, 0)), pl.BlockSpec((e, 1), lambda i: (0, 0))],
        out_specs=[pl.BlockSpec((tm, d // 2), lambda i: (i, 0)),
                   pl.BlockSpec((2, tm), lambda i: (0, i)), pl.BlockSpec((2, tm), lambda i: (0, i))],
        out_shape=[jax.ShapeDtypeStruct((n, d // 2), jnp.uint32),
                   jax.ShapeDtypeStruct((2, n), jnp.int32), jax.ShapeDtypeStruct((2, n), F32)],
        compiler_params=_params("parallel"),
        name="ffn_router",
    )(x, gain.reshape(1, d), router_w.T, router_bias.reshape(e, 1))


MOE_TILE = 256


def _moe_plan(idx, n_experts, tm):
    n = idx.shape[1]
    pairs = 2 * n
    n_rows = pairs + n_experts * tm
    expert = idx.reshape(pairs)
    onehot = (expert[:, None] == jnp.arange(n_experts)[None, :]).astype(jnp.int32)
    rank = jnp.sum((jnp.cumsum(onehot, axis=0) - onehot) * onehot, axis=1)
    group = -(-jnp.sum(onehot, axis=0) // tm) * tm
    group_end = jnp.cumsum(group)
    pos = (group_end - group)[expert] + rank
    pair_of_row = jnp.full((n_rows,), -1, jnp.int32).at[pos].set(jnp.arange(pairs, dtype=jnp.int32))
    is_pad = pair_of_row < 0
    spare = pairs + jnp.cumsum(is_pad.astype(jnp.int32)) - 1
    dst = jnp.where(is_pad, spare, pair_of_row)
    src = jnp.where(is_pad, 0, pair_of_row % n)
    tile_expert = jnp.minimum(
        jnp.searchsorted(group_end, jnp.arange(n_rows // tm, dtype=jnp.int32) * tm, side="right"), n_experts - 1)
    return src.reshape(-1, tm), dst.reshape(-1, tm), tile_expert.astype(jnp.int32)


def _expert_kernel(te_ref, src_ref, src_next_ref, dst_ref, x_hbm, wgu_ref, wd_ref, y_hbm,
                   xbuf, ybuf, gsem, ssem, *, tm, d_expert):
    t, n_tiles = pl.program_id(0), pl.num_programs(0)
    slot = t % 2

    def row_copy_in(ids_ref, s, r):
        return pltpu.make_async_copy(x_hbm.at[pl.ds(ids_ref[0, r], 1), :], xbuf.at[s, pl.ds(r, 1), :], gsem.at[s])

    def row_copy_out(s, r):
        return pltpu.make_async_copy(ybuf.at[s, pl.ds(r, 1), :], y_hbm.at[pl.ds(dst_ref[0, r], 1), :], ssem.at[s])

    def each_row(fn):
        lax.fori_loop(0, tm, lambda r, c: (fn(r), c)[1], 0, unroll=8)

    def gather(ids_ref, s):
        each_row(lambda r: row_copy_in(ids_ref, s, r).start())

    def gather_wait(s):
        each_row(lambda r: row_copy_in(src_ref, s, r).wait())

    def scatter_start(s):
        each_row(lambda r: row_copy_out(s, r).start(priority=1))

    def scatter_wait(s):
        each_row(lambda r: row_copy_out(s, r).wait())

    @pl.when(t == 0)
    def _():
        gather(src_ref, 0)

    @pl.when(t + 1 < n_tiles)
    def _():
        gather(src_next_ref, 1 - slot)

    gather_wait(slot)
    xn = jnp.concatenate(_unpack_halves(xbuf[slot]), axis=1).astype(BF16)
    h = jnp.dot(xn, wgu_ref[...], preferred_element_type=F32)
    hg, hu = h[:, :d_expert], h[:, d_expert:]
    act = (hg * jax.nn.sigmoid(hg) * hu).astype(BF16)

    @pl.when(t >= 2)
    def _():
        scatter_wait(slot)

    ybuf[slot] = _pack_halves(jnp.dot(act, wd_ref[...], preferred_element_type=F32))
    scatter_start(slot)

    @pl.when(t == n_tiles - 1)
    def _():
        scatter_wait(slot)

        @pl.when(t >= 1)
        def _():
            scatter_wait(1 - slot)


def _moe_experts(hp, idx, wgu, wd):
    n, half = hp.shape
    d = 2 * half
    n_experts, _, two_f = wgu.shape
    tm = MOE_TILE
    src, dst, tile_expert = _moe_plan(idx, n_experts, tm)
    n_tiles = src.shape[0]
    src, dst = src.reshape(n_tiles, 1, tm), dst.reshape(n_tiles, 1, tm)
    smem = lambda fn: pl.BlockSpec((None, 1, tm), fn, memory_space=pltpu.SMEM)
    grid_spec = pltpu.PrefetchScalarGridSpec(
        num_scalar_prefetch=1,
        grid=(n_tiles,),
        in_specs=[smem(lambda t, te: (t, 0, 0)),
                  smem(lambda t, te: (jnp.minimum(t + 1, n_tiles - 1), 0, 0)),
                  smem(lambda t, te: (t, 0, 0)),
                  pl.BlockSpec(memory_space=pl.ANY),
                  pl.BlockSpec((None, d, two_f), lambda t, te: (te[t], 0, 0)),
                  pl.BlockSpec((None, two_f // 2, d), lambda t, te: (te[t], 0, 0))],
        out_specs=pl.BlockSpec(memory_space=pl.ANY),
        scratch_shapes=[pltpu.VMEM((2, tm, half), jnp.uint32), pltpu.VMEM((2, tm, half), jnp.uint32),
                        pltpu.SemaphoreType.DMA((2,)), pltpu.SemaphoreType.DMA((2,))])
    return pl.pallas_call(
        functools.partial(_expert_kernel, tm=tm, d_expert=two_f // 2),
        grid_spec=grid_spec,
        out_shape=jax.ShapeDtypeStruct((n_tiles * tm, half), jnp.uint32),
        compiler_params=_params("arbitrary"),
        name="moe_experts",
    )(tile_expert, src, src, dst, hp, wgu, wd)


def _combine_kernel(x_ref, ya_ref, yb_ref, w_ref, o_ref):
    w = w_ref[...]
    half = ya_ref.shape[1]
    for c, (a, b) in enumerate(zip(_unpack_halves(ya_ref[...]), _unpack_halves(yb_ref[...]))):
        cols = slice(c * half, (c + 1) * half)
        o_ref[:, cols] = x_ref[:, cols] + w[:, 0:1] * a + w[:, 1:2] * b


def _moe_combine(x, y, wts):
    n, d = x.shape
    tm = _tile(256, n)
    nt = n // tm
    return pl.pallas_call(
        _combine_kernel,
        grid=(nt,),
        in_specs=[pl.BlockSpec((tm, d), lambda i: (i, 0)), pl.BlockSpec((tm, d // 2), lambda i: (i, 0)),
                  pl.BlockSpec((tm, d // 2), lambda i: (nt + i, 0)), pl.BlockSpec((tm, 2), lambda i: (i, 0))],
        out_specs=pl.BlockSpec((tm, d), lambda i: (i, 0)),
        out_shape=jax.ShapeDtypeStruct((n, d), F32),
        compiler_params=_params("parallel"),
        name="moe_combine",
    )(x, y, y, wts)


def _rope_tables(seq):
    half = HEAD_DIM // 2
    inv_freq = jnp.power(ROPE_THETA, -jnp.arange(half, dtype=F32) / half)
    ang = jnp.arange(seq, dtype=F32)[:, None] * inv_freq[None, :]
    cos, sin = jnp.cos(ang), jnp.sin(ang)
    return jnp.concatenate([cos, cos], axis=-1), jnp.concatenate([-sin, sin], axis=-1)


def kernel(x, attn_norm, w_in, fox_forget_bias, fox_q_norm, fox_k_norm, ch_q_norm, ch_k_norm, ch_rel_bias,
           diff_q_norm, diff_k_norm, diff_lambda, diff_subln, w_branch, w_out, ffn_norm, router_w, router_bias,
           w_gate, w_up, w_down):
    batch, seq, d_model = x.shape
    depth = w_in.shape[0]
    width = d_model // N_BRANCHES
    fox_heads = width // HEAD_DIM
    qkv_cols = 3 * width
    n_experts, _, d_expert = w_gate.shape[1:]
    n = batch * seq
    scale = HEAD_DIM ** -0.5
    assert fox_heads <= LANES and width % (2 * HEAD_DIM) == 0 and seq % CHUNK == 0

    cos_t, sin_t = _rope_tables(seq)
    ones = jnp.ones((HEAD_DIM,), F32)
    res_spec = lambda tm, tn: pl.BlockSpec((tm, tn), lambda i, j: (i, j))
    xf = x.reshape(n, d_model)

    for layer in range(depth):
        wl = w_in[layer]
        s_f, s_sb, s_gate = qkv_cols, qkv_cols + fox_heads, 4 * qkv_cols + fox_heads
        w_qkv = jnp.concatenate([wl[:, :s_f], wl[:, s_sb:s_gate]], axis=1).astype(BF16)
        w_f = jnp.pad(wl[:, s_f:s_sb], ((0, 0), (0, LANES - fox_heads))).astype(BF16)
        w_g = wl[:, s_gate:]
        g_amax = jnp.max(jnp.abs(w_g), axis=0, keepdims=True)
        w_g8 = (w_g * jnp.where(g_amax > 0.0, FP8_MAX / g_amax, 0.0)).astype(FP8)
        g_scale = g_amax * (1.0 / FP8_MAX)
        f_bias = jnp.pad(fox_forget_bias[layer], (0, LANES - fox_heads)).reshape(1, LANES)
        gains = jnp.stack([
            fox_q_norm[layer] * scale, fox_k_norm[layer], ones,
            ones * scale, ones, ones,
            ch_q_norm[layer] * scale, ch_k_norm[layer], ones,
            diff_q_norm[layer] * scale, diff_k_norm[layer], ones,
            ones, ones, ones, ones])

        h, h8, h_scale = _rmsnorm(xf, attn_norm[layer])
        qkv = _qkv_proj(h, w_qkv, gains, cos_t, sin_t, seq, width)
        gates = _matmul(h8, w_g8, _epi_scaled_sigmoid, BF16, extras=(h_scale, g_scale),
                        extra_specs=(lambda tm, tn: pl.BlockSpec((tm, LANES), lambda i, j: (i, 0)),
                                     lambda tm, tn: pl.BlockSpec((1, tn), lambda i, j: (0, j))),
                        name="gate_proj")
        cum = _forget_cumsum(h, w_f, f_bias, batch, seq)
        cum_t = cum[:, :fox_heads].T

        lam_init = 0.8 - 0.6 * math.exp(-0.3 * layer)
        branches = (
            _fox_attention(qkv, cum, cum_t, batch, seq, width),
            _sb_attention(qkv, batch, seq, width),
            _band_attention(qkv, ch_rel_bias[layer], batch, seq, width),
            _diff_attention(qkv, diff_lambda[layer], diff_subln[layer], lam_init, batch, seq, width),
        )
        merged = _gated_merge(branches, w_branch[layer].astype(BF16), gates)
        xf = _matmul(merged, w_out[layer].astype(BF16), _epi_residual, F32, extras=(xf,), extra_specs=(res_spec,),
                     tn=512, name="out_proj")

        hp, idx, wts = _router(xf, ffn_norm[layer], router_w, router_bias)
        wgu = jnp.concatenate([w_gate[layer], w_up[layer]], axis=2).astype(BF16)
        y = _moe_experts(hp, idx, wgu, w_down[layer].astype(BF16))
        xf = _moe_combine(xf, y, wts.T)
    return xf.reshape(batch, seq, d_model)
```

```python
import functools
import math

import jax
import jax.numpy as jnp
from jax import lax
from jax.experimental import pallas as pl
from jax.experimental.pallas import tpu as pltpu

HEAD_DIM = 128
LANES = 128
CHUNK = 64
N_BRANCHES = 4
CH_LEFT_CHUNKS = 8
REL_CLIP = 128
ROPE_THETA = 10000.0
N_GROUPS = 4
RMS_EPS = 1e-6
NEG_INF = float("-inf")
VMEM_LIMIT_BYTES = 56 * 1024 * 1024

F32 = jnp.float32
BF16 = jnp.bfloat16


def _tile(pref, n):
    t = min(pref, n)
    while n % t:
        t //= 2
    return t


def _params(*sem):
    return pltpu.CompilerParams(dimension_semantics=sem, vmem_limit_bytes=VMEM_LIMIT_BYTES)


def _dot_nt(a, b):
    return lax.dot_general(a, b, (((1,), (1,)), ((), ())), preferred_element_type=F32)


def _log_sigmoid(x):
    return jnp.minimum(x, 0.0) - jnp.log1p(jnp.exp(-jnp.abs(x)))


FP8 = jnp.float8_e4m3fn
FP8_MAX = 448.0


def _rmsnorm_kernel(x_ref, g_ref, o_ref, o8_ref, s_ref):
    x = x_ref[...]
    ms = jnp.mean(x * x, axis=-1, keepdims=True)
    y = x * lax.rsqrt(ms + RMS_EPS) * g_ref[...]
    o_ref[...] = y.astype(o_ref.dtype)
    amax = jnp.max(jnp.abs(y), axis=-1, keepdims=True)
    o8_ref[...] = (y * jnp.where(amax > 0.0, FP8_MAX / amax, 0.0)).astype(o8_ref.dtype)
    s_ref[...] = jnp.broadcast_to(amax * (1.0 / FP8_MAX), s_ref.shape)


def _rmsnorm(x, gain):
    n, d = x.shape
    tm = _tile(256, n)
    row = pl.BlockSpec((tm, d), lambda i: (i, 0))
    return pl.pallas_call(
        _rmsnorm_kernel,
        grid=(n // tm,),
        in_specs=[row, pl.BlockSpec((1, d), lambda i: (0, 0))],
        out_specs=[row, row, pl.BlockSpec((tm, LANES), lambda i: (i, 0))],
        out_shape=[jax.ShapeDtypeStruct((n, d), BF16), jax.ShapeDtypeStruct((n, d), FP8),
                   jax.ShapeDtypeStruct((n, LANES), F32)],
        compiler_params=_params("parallel"),
        name="attn_rmsnorm",
    )(x, gain.reshape(1, d))


def _cast_kernel(w_ref, o_ref):
    o_ref[...] = w_ref[...].astype(o_ref.dtype)


def _cast_layer(w, layer):
    cols = w.shape[-1]
    rows = math.prod(w.shape[1:-1])
    w2 = w.reshape(w.shape[0] * rows, cols)
    tr, tc = _tile(1024, rows), _tile(2048, cols)
    return pl.pallas_call(
        _cast_kernel,
        grid=(rows // tr, cols // tc),
        in_specs=[pl.BlockSpec((tr, tc), lambda i, j: (layer * (rows // tr) + i, j))],
        out_specs=pl.BlockSpec((tr, tc), lambda i, j: (i, j)),
        out_shape=jax.ShapeDtypeStruct((rows, cols), BF16),
        compiler_params=_params("parallel", "parallel"),
        name="cast_weights",
    )(w2)


def _pair_cast_kernel(a_ref, b_ref, o_ref):
    f = a_ref.shape[-1]
    o_ref[:, :f] = a_ref[...].astype(o_ref.dtype)
    o_ref[:, f:] = b_ref[...].astype(o_ref.dtype)


def _cast_gate_up(w_gate, w_up, layer):
    _, e, d, f = w_gate.shape
    tr = _tile(1024, d)
    spec = pl.BlockSpec((None, None, tr, f), lambda x, i: (layer, x, i, 0))
    return pl.pallas_call(
        _pair_cast_kernel,
        grid=(e, d // tr),
        in_specs=[spec, spec],
        out_specs=pl.BlockSpec((None, tr, 2 * f), lambda x, i: (x, i, 0)),
        out_shape=jax.ShapeDtypeStruct((e, d, 2 * f), BF16),
        compiler_params=_params("parallel", "parallel"),
        name="cast_gate_up",
    )(w_gate, w_up)


def _in_proj_kernel(a_ref, b_ref, o_ref, amax_ref, *, first_shifted, shift):
    j, i = pl.program_id(0), pl.program_id(1)

    def emit(w):
        wb = w.astype(o_ref.dtype)
        o_ref[...] = wb
        col_max = jnp.max(jnp.abs(wb.astype(F32)), axis=0, keepdims=True)
        amax_ref[...] = jnp.where(i == 0, col_max, jnp.maximum(amax_ref[...], col_max))

    @pl.when(j < first_shifted)
    def _():
        emit(a_ref[...])

    @pl.when(j >= first_shifted)
    def _():
        tc = a_ref.shape[1]
        emit(jnp.concatenate([a_ref[...], b_ref[...]], axis=1)[:, shift:shift + tc])


def _stage_in_proj(w_in, layer, keep_cols, drop_cols):
    _, d, cols = w_in.shape
    out_cols = cols - drop_cols
    tc = _tile(1024, math.gcd(keep_cols, out_cols))
    tr = _tile(2048, d)
    assert tc % LANES == 0 and drop_cols < LANES
    lane_blocks = tc // LANES
    return pl.pallas_call(
        functools.partial(_in_proj_kernel, first_shifted=keep_cols // tc, shift=drop_cols),
        grid=(out_cols // tc, d // tr),
        in_specs=[pl.BlockSpec((None, tr, tc), lambda j, i: (layer, i, j)),
                  pl.BlockSpec((None, tr, LANES), lambda j, i: (layer, i, (j + 1) * lane_blocks))],
        out_specs=[pl.BlockSpec((tr, tc), lambda j, i: (i, j)), pl.BlockSpec((1, tc), lambda j, i: (0, j))],
        out_shape=[jax.ShapeDtypeStruct((d, out_cols), BF16), jax.ShapeDtypeStruct((1, out_cols), F32)],
        compiler_params=_params("parallel", "arbitrary"),
        name="stage_in_proj",
    )(w_in, w_in)


def _quantize_kernel(w_ref, amax_ref, o_ref):
    amax = amax_ref[...]
    o_ref[...] = (w_ref[...].astype(F32) * jnp.where(amax > 0.0, FP8_MAX / amax, 0.0)).astype(o_ref.dtype)


def _quantize_cols(w, amax, col0, ncols):
    d = w.shape[0]
    tr, tc = _tile(2048, d), _tile(1024, math.gcd(col0, ncols))
    assert tc % LANES == 0
    off = col0 // tc
    return pl.pallas_call(
        _quantize_kernel,
        grid=(d // tr, ncols // tc),
        in_specs=[pl.BlockSpec((tr, tc), lambda i, j: (i, off + j)), pl.BlockSpec((1, tc), lambda i, j: (0, off + j))],
        out_specs=pl.BlockSpec((tr, tc), lambda i, j: (i, j)),
        out_shape=jax.ShapeDtypeStruct((d, ncols), FP8),
        compiler_params=_params("parallel", "parallel"),
        name="quantize_gate_weights",
    )(w, amax)


def _matmul(a, w, epilogue, out_dtype, extras=(), extra_specs=(), tm=1024, tn=1024, n_cols=None, name="matmul"):
    m, k = a.shape
    n = w.shape[1] if n_cols is None else n_cols
    tm, tn = _tile(tm, m), _tile(tn, n)

    def kern(a_ref, w_ref, *refs):
        acc = jnp.dot(a_ref[...], w_ref[...], preferred_element_type=F32)
        epilogue(acc, refs[:-1], refs[-1])

    return pl.pallas_call(
        kern,
        grid=(m // tm, n // tn),
        in_specs=[pl.BlockSpec((tm, k), lambda i, j: (i, 0)), pl.BlockSpec((k, tn), lambda i, j: (0, j)),
                  *[mk(tm, tn) for mk in extra_specs]],
        out_specs=pl.BlockSpec((tm, tn), lambda i, j: (i, j)),
        out_shape=jax.ShapeDtypeStruct((m, n), out_dtype),
        compiler_params=_params("parallel", "arbitrary"),
        name=name,
    )(a, w, *extras)


def _epi_scaled_sigmoid(acc, refs, o_ref):
    row_scale, col_scale = refs
    o_ref[...] = jax.nn.sigmoid(acc * row_scale[:, :1] * col_scale[...]).astype(o_ref.dtype)


def _epi_residual(acc, refs, o_ref):
    o_ref[...] = refs[0][...] + acc


PLAIN, NORM, NORM_ROPE = 0, 1, 2
SECTION_KINDS = (NORM, NORM, PLAIN, PLAIN, PLAIN, PLAIN, NORM, NORM, PLAIN, NORM_ROPE, NORM_ROPE, PLAIN)


def _qkv_epilogue(acc, refs, o_ref, *, sec_tiles):
    gain_ref, cos_ref, sin_ref = refs
    sec = pl.program_id(1) // sec_tiles
    g = gain_ref[pl.ds(sec, 1), :]
    tn = acc.shape[1]

    def run(kind):
        for hh in range(tn // HEAD_DIM):
            y = acc[:, hh * HEAD_DIM:(hh + 1) * HEAD_DIM]
            if kind != PLAIN:
                y = y * lax.rsqrt(jnp.mean(y * y, axis=-1, keepdims=True) + RMS_EPS)
            y = y * g
            if kind == NORM_ROPE:
                y = y * cos_ref[...] + pltpu.roll(y, HEAD_DIM // 2, 1) * sin_ref[...]
            o_ref[:, hh * HEAD_DIM:(hh + 1) * HEAD_DIM] = y.astype(o_ref.dtype)

    for kind in (PLAIN, NORM, NORM_ROPE):
        hit = functools.reduce(jnp.logical_or, [sec == s for s, kd in enumerate(SECTION_KINDS) if kd == kind])
        pl.when(hit)(functools.partial(run, kind))


def _qkv_proj(h, w_qkv, gains, cos_t, sin_t, seq, width):
    n = h.shape[0]
    tm = _tile(1024, seq)
    tn = _tile(1024, width)
    seq_tiles = seq // tm
    row_spec = lambda tm_, tn_: pl.BlockSpec((tm_, LANES), lambda i, j: (i % seq_tiles, 0))
    gain_spec = lambda tm_, tn_: pl.BlockSpec(gains.shape, lambda i, j: (0, 0))
    return _matmul(h, w_qkv, functools.partial(_qkv_epilogue, sec_tiles=width // tn), BF16,
                   extras=(gains, cos_t, sin_t), extra_specs=(gain_spec, row_spec, row_spec),
                   tm=tm, tn=tn, n_cols=len(SECTION_KINDS) * width, name="qkv_proj")


def _split3(x):
    hi = x.astype(BF16)
    r = x - hi.astype(F32)
    mid = r.astype(BF16)
    lo = (r - mid.astype(F32)).astype(BF16)
    return hi, mid, lo


def _fgate_kernel(h_ref, w_ref, b_ref, tri_ref, o_ref, carry_sc):
    @pl.when(pl.program_id(1) == 0)
    def _():
        carry_sc[...] = jnp.zeros_like(carry_sc)

    f = jnp.dot(h_ref[...], w_ref[...], preferred_element_type=F32) + b_ref[...]
    lf = _log_sigmoid(f)
    tri = tri_ref[...]
    c = carry_sc[...]
    for part in _split3(lf):
        c = c + jnp.dot(tri, part, preferred_element_type=F32)
    o_ref[...] = c
    carry_sc[...] = c[-1:, :]


def _forget_cumsum(h, w_f, bias, batch, seq):
    n, d = h.shape
    ts = _tile(512, seq)
    nt = seq // ts
    tri = (jnp.arange(ts)[:, None] >= jnp.arange(ts)[None, :]).astype(BF16)
    return pl.pallas_call(
        _fgate_kernel,
        grid=(batch, nt),
        in_specs=[pl.BlockSpec((ts, d), lambda b, t: (b * nt + t, 0)),
                  pl.BlockSpec((d, LANES), lambda b, t: (0, 0)),
                  pl.BlockSpec((1, LANES), lambda b, t: (0, 0)),
                  pl.BlockSpec((ts, ts), lambda b, t: (0, 0))],
        out_specs=pl.BlockSpec((ts, LANES), lambda b, t: (b * nt + t, 0)),
        out_shape=jax.ShapeDtypeStruct((n, LANES), F32),
        scratch_shapes=[pltpu.VMEM((1, LANES), F32)],
        compiler_params=_params("parallel", "arbitrary"),
        name="forget_cumsum",
    )(h, w_f, bias, tri)


def _softmax_step(s, v, m_sc, l_sc, acc_sc):
    m_prev = m_sc[...]
    m_new = jnp.maximum(m_prev, jnp.max(s, axis=1, keepdims=True))
    alpha = jnp.exp(m_prev - m_new)
    p = jnp.exp(s - m_new)
    l_sc[...] = alpha * l_sc[...] + jnp.sum(p, axis=1, keepdims=True)
    acc_sc[...] = alpha * acc_sc[...] + jnp.dot(p.astype(v.dtype), v, preferred_element_type=F32)
    m_sc[...] = m_new


def _softmax_init(m_sc, l_sc, acc_sc):
    m_sc[...] = jnp.full_like(m_sc, NEG_INF)
    l_sc[...] = jnp.zeros_like(l_sc)
    acc_sc[...] = jnp.zeros_like(acc_sc)


def _attn_specs(seq, tq, width, q_sec, k_sec, v_sec, v_mult=1, q_mult=1):
    nq = seq // tq
    sec = width // HEAD_DIM
    q_spec = lambda c=0: pl.BlockSpec((tq, HEAD_DIM), lambda b, h, i: (b * nq + i, q_sec * sec + h * q_mult + c))
    k_spec = lambda c=0: pl.BlockSpec((seq, HEAD_DIM), lambda b, h, i: (b, k_sec * sec + h * q_mult + c))
    v_spec = pl.BlockSpec((seq, HEAD_DIM * v_mult), lambda b, h, i: (b, v_sec * sec // v_mult + h))
    return q_spec, k_spec, v_spec


def _rep(x, n):
    return x if n == LANES else jnp.concatenate([x] * (n // LANES), axis=1)


def _lane_fold(x, op):
    out = x[:, :LANES]
    for c in range(1, x.shape[1] // LANES):
        out = op(out, x[:, c * LANES:(c + 1) * LANES])
    return out


def _row_max(s):
    return jnp.broadcast_to(jnp.max(_lane_fold(s, jnp.maximum), axis=1, keepdims=True), (s.shape[0], LANES))


def _softmax_update(s, mx, v, m_sc, l_sc, acc_sc):
    m_prev = m_sc[...]
    m_new = jnp.maximum(m_prev, mx)
    alpha = jnp.exp(m_prev - m_new)
    p = jnp.exp(s - _rep(m_new, s.shape[1]))
    l_sc[...] = alpha * l_sc[...] + _lane_fold(p, jnp.add)
    acc_sc[...] = (_rep(alpha, acc_sc.shape[1]) * acc_sc[...]
                   + jnp.dot(p.astype(v.dtype), v, preferred_element_type=F32))
    m_sc[...] = m_new


def _softmax_result(l_sc, acc_sc):
    return acc_sc[...] * (1.0 / jnp.sum(l_sc[...], axis=1, keepdims=True))


def _pipelined_causal(qi, score, update):
    @pl.when(qi == 0)
    def _():
        score(0, True)

    @pl.when(qi > 0)
    def _():
        score(0, False)

    def body(j, carry):
        update(j)
        score(j + 1, False)
        return carry

    lax.fori_loop(0, qi - 1, body, 0)

    @pl.when(qi > 0)
    def _():
        update(qi - 1)
        score(qi, True)

    update(qi)


def _fox_kernel(q_ref, k_ref, v_ref, ct_ref, cs_ref, o_ref, s_buf, mx_buf, m_sc, l_sc, acc_sc, ct_sc, *, t):
    h, qi = pl.program_id(1), pl.program_id(2)
    lane = lax.broadcasted_iota(jnp.int32, ct_ref.shape, 1)
    ct = jnp.sum(jnp.where(lane == h, ct_ref[...], 0.0), axis=1, keepdims=True)
    ct_sc[...] = jnp.broadcast_to(ct, ct_sc.shape)
    _softmax_init(m_sc, l_sc, acc_sc)

    def score(j, masked):
        ks = pl.multiple_of(j * t, t)
        s = _dot_nt(q_ref[...], k_ref[pl.ds(ks, t), :])
        s = s + _rep(ct_sc[...], t) - cs_ref[pl.ds(j, 1), :]
        if masked:
            row = lax.broadcasted_iota(jnp.int32, (t, t), 0)
            col = lax.broadcasted_iota(jnp.int32, (t, t), 1)
            s = jnp.where(col <= row, s, NEG_INF)
        s_buf[j % 2] = s
        mx_buf[j % 2] = _row_max(s)

    def update(j):
        ks = pl.multiple_of(j * t, t)
        _softmax_update(s_buf[j % 2], mx_buf[j % 2], v_ref[pl.ds(ks, t), :], m_sc, l_sc, acc_sc)

    _pipelined_causal(qi, score, update)
    o_ref[...] = _softmax_result(l_sc, acc_sc).astype(o_ref.dtype)


def _fox_attention(qkv, cum, cum_t, batch, seq, width):
    n = qkv.shape[0]
    heads = width // HEAD_DIM
    t = _tile(512, seq)
    nq = seq // t
    q_spec, k_spec, v_spec = _attn_specs(seq, t, width, 0, 1, 2)
    stat = pltpu.VMEM((t, LANES), F32)
    return pl.pallas_call(
        functools.partial(_fox_kernel, t=t),
        grid=(batch, heads, nq),
        in_specs=[q_spec(), k_spec(), v_spec,
                  pl.BlockSpec((t, LANES), lambda b, h, i: (b * nq + i, 0)),
                  pl.BlockSpec((None, nq, t), lambda b, h, i: (h, b, 0))],
        out_specs=pl.BlockSpec((t, HEAD_DIM), lambda b, h, i: (b * nq + i, h)),
        out_shape=jax.ShapeDtypeStruct((n, width), BF16),
        scratch_shapes=[pltpu.VMEM((2, t, t), F32), pltpu.VMEM((2, t, LANES), F32),
                        stat, stat, pltpu.VMEM((t, HEAD_DIM), F32), stat],
        compiler_params=_params("parallel", "parallel", "arbitrary"),
        name="fox_attention",
    )(qkv, qkv, qkv, cum, cum_t.reshape(heads, n // t, t))


SB_DEAD_LOG = -110.0


def _sb_kernel(q_ref, k_ref, v_ref, u_ref, o_ref, e_buf, rs_buf, c_sc, acc_sc, *, t):
    qi = pl.program_id(2)
    c_sc[...] = jnp.zeros_like(c_sc)
    acc_sc[...] = jnp.zeros_like(acc_sc)

    def score(jj, diagonal):
        ks = pl.multiple_of((qi - jj) * t, t)
        z = _dot_nt(q_ref[...], k_ref[pl.ds(ks, t), :])
        softplus = jnp.log(1.0 + jnp.exp(-jnp.abs(z)))
        log_beta = jnp.minimum(z, 0.0) - softplus
        log_rest = jnp.minimum(-z, 0.0) - softplus
        if diagonal:
            row = lax.broadcasted_iota(jnp.int32, (t, t), 0)
            col = lax.broadcasted_iota(jnp.int32, (t, t), 1)
            strict = col < row
            log_rest = jnp.where(strict, log_rest, 0.0)
        hi = log_rest.astype(BF16)
        lo = (log_rest - hi.astype(F32)).astype(BF16)
        u = u_ref[...]
        cw = u.shape[0]
        later, tail = [], None
        for c in reversed(range(t // cw)):
            sl = slice(c * cw, (c + 1) * cw)
            inside = jnp.dot(hi[:, sl], u, preferred_element_type=F32) + jnp.dot(lo[:, sl], u, preferred_element_type=F32)
            later.insert(0, inside if tail is None else inside + _rep(tail, cw))
            chunk_sum = jnp.broadcast_to(
                jnp.sum(_lane_fold(log_rest[:, sl], jnp.add), axis=1, keepdims=True), (t, LANES))
            tail = chunk_sum if tail is None else tail + chunk_sum
        e = log_beta + jnp.concatenate(later, axis=1)
        if diagonal:
            e = jnp.where(strict, e, NEG_INF)
        e_buf[jj % 2] = e
        rs_buf[jj % 2] = tail

    def update(jj):
        ks = pl.multiple_of((qi - jj) * t, t)
        c = c_sc[...]
        w = jnp.exp(e_buf[jj % 2] + _rep(c, t))
        v = v_ref[pl.ds(ks, t), :]
        acc_sc[...] += jnp.dot(w.astype(v.dtype), v, preferred_element_type=F32)
        c_sc[...] = c + rs_buf[jj % 2]

    score(0, True)

    def body(state):
        jj, _ = state
        update(jj)
        score(jj + 1, False)
        return jj + 1, jnp.max(c_sc[...])

    jj, c_max = lax.while_loop(lambda st: (st[0] < qi) & (st[1] > SB_DEAD_LOG), body, (0, jnp.float32(0.0)))
    pl.when(c_max > SB_DEAD_LOG)(lambda: update(jj))
    o_ref[...] = acc_sc[...].astype(o_ref.dtype)


def _sb_attention(qkv, batch, seq, width):
    n = qkv.shape[0]
    heads = width // HEAD_DIM
    t = _tile(512, seq)
    nq = seq // t
    q_spec, k_spec, v_spec = _attn_specs(seq, t, width, 3, 4, 5)
    cw = min(256, t)
    upper = (jnp.arange(cw)[:, None] > jnp.arange(cw)[None, :]).astype(BF16)
    return pl.pallas_call(
        functools.partial(_sb_kernel, t=t),
        grid=(batch, heads, nq),
        in_specs=[q_spec(), k_spec(), v_spec, pl.BlockSpec((cw, cw), lambda b, h, i: (0, 0))],
        out_specs=pl.BlockSpec((t, HEAD_DIM), lambda b, h, i: (b * nq + i, h)),
        out_shape=jax.ShapeDtypeStruct((n, width), BF16),
        scratch_shapes=[pltpu.VMEM((2, t, t), F32), pltpu.VMEM((2, t, LANES), F32),
                        pltpu.VMEM((t, LANES), F32), pltpu.VMEM((t, HEAD_DIM), F32)],
        compiler_params=_params("parallel", "parallel", "arbitrary"),
        name="stick_breaking_attention",
    )(qkv, qkv, qkv, upper)


def _band_kernel(q_ref, k_ref, v_ref, tab_ref, o_ref, bias_sc, s_buf, mx_buf, m_sc, l_sc, acc_sc, *, t, n_back):
    qi = pl.program_id(2)

    @pl.when(qi == 0)
    def _():
        width = tab_ref.shape[-1]
        rolled = pltpu.roll(jnp.broadcast_to(tab_ref[...], (t, width)), 0, 1, stride=1, stride_axis=0)
        row = lax.broadcasted_iota(jnp.int32, (t, t), 0)
        col = lax.broadcasted_iota(jnp.int32, (t, t), 1)
        for d in range(n_back + 1):
            gap = (row + d * t) // CHUNK - col // CHUNK
            visible = (gap >= 0) & (gap <= CH_LEFT_CHUNKS)
            c0 = (n_back + 1 - d) * t
            bias_sc[d] = jnp.where(visible, rolled[:, c0:c0 + t], NEG_INF)

    _softmax_init(m_sc, l_sc, acc_sc)

    def score(d):
        ks = pl.multiple_of((qi - d) * t, t)
        s = _dot_nt(q_ref[...], k_ref[pl.ds(ks, t), :]) + bias_sc[d]
        s_buf[d % 2] = s
        mx_buf[d % 2] = _row_max(s)

    def update(d):
        ks = pl.multiple_of((qi - d) * t, t)
        _softmax_update(s_buf[d % 2], mx_buf[d % 2], v_ref[pl.ds(ks, t), :], m_sc, l_sc, acc_sc)

    def body(d, carry):
        update(d)
        score(d + 1)
        return carry

    n_tiles = jnp.minimum(qi, n_back) + 1
    score(0)
    lax.fori_loop(0, n_tiles - 1, body, 0)
    update(n_tiles - 1)
    o_ref[...] = _softmax_result(l_sc, acc_sc).astype(o_ref.dtype)


def _band_table(rel_bias, t, n_back):
    rel = (n_back + 1) * t - jnp.arange((n_back + 2) * t)
    return rel_bias[:, None, jnp.clip(rel, -REL_CLIP, REL_CLIP) + REL_CLIP].astype(F32)


def _band_attention(qkv, rel_bias, batch, seq, width):
    n = qkv.shape[0]
    heads = width // HEAD_DIM
    t = _tile(512, seq)
    nq = seq // t
    n_back = -(-(CH_LEFT_CHUNKS * CHUNK) // t)
    table = _band_table(rel_bias, t, n_back)
    q_spec, k_spec, v_spec = _attn_specs(seq, t, width, 6, 7, 8)
    stat = pltpu.VMEM((t, LANES), F32)
    return pl.pallas_call(
        functools.partial(_band_kernel, t=t, n_back=n_back),
        grid=(batch, heads, nq),
        in_specs=[q_spec(), k_spec(), v_spec,
                  pl.BlockSpec((None, 1, table.shape[-1]), lambda b, h, i: (h, 0, 0))],
        out_specs=pl.BlockSpec((t, HEAD_DIM), lambda b, h, i: (b * nq + i, h)),
        out_shape=jax.ShapeDtypeStruct((n, width), BF16),
        scratch_shapes=[pltpu.VMEM((n_back + 1, t, t), F32), pltpu.VMEM((2, t, t), F32),
                        pltpu.VMEM((2, t, LANES), F32), stat, stat, pltpu.VMEM((t, HEAD_DIM), F32)],
        compiler_params=_params("parallel", "parallel", "arbitrary"),
        name="chunk_band_attention",
    )(qkv, qkv, qkv, table)


def _diff_kernel(q1_ref, q2_ref, k1_ref, k2_ref, v_ref, lam_ref, g_ref, o_ref, s_buf, mx_buf,
                 m1_sc, l1_sc, a1_sc, m2_sc, l2_sc, a2_sc, *, t, lam_init):
    qi = pl.program_id(2)
    _softmax_init(m1_sc, l1_sc, a1_sc)
    _softmax_init(m2_sc, l2_sc, a2_sc)
    streams = ((q1_ref, k1_ref, m1_sc, l1_sc, a1_sc), (q2_ref, k2_ref, m2_sc, l2_sc, a2_sc))

    def score(j, masked):
        ks = pl.multiple_of(j * t, t)
        for i, (q_ref, k_ref, _, _, _) in enumerate(streams):
            s = _dot_nt(q_ref[...], k_ref[pl.ds(ks, t), :])
            if masked:
                row = lax.broadcasted_iota(jnp.int32, (t, t), 0) // CHUNK
                col = lax.broadcasted_iota(jnp.int32, (t, t), 1) // CHUNK
                s = jnp.where(col <= row, s, NEG_INF)
            s_buf[j % 2, i] = s
            mx_buf[j % 2, i] = _row_max(s)

    def update(j):
        ks = pl.multiple_of(j * t, t)
        v = v_ref[pl.ds(ks, t), :]
        for i, (_, _, m_sc, l_sc, a_sc) in enumerate(streams):
            _softmax_update(s_buf[j % 2, i], mx_buf[j % 2, i], v, m_sc, l_sc, a_sc)

    _pipelined_causal(qi, score, update)

    lv = lam_ref[...]
    lam = (jnp.exp(jnp.sum(lv[0:1] * lv[1:2], axis=1, keepdims=True))
           - jnp.exp(jnp.sum(lv[2:3] * lv[3:4], axis=1, keepdims=True)) + lam_init)
    o = _softmax_result(l1_sc, a1_sc) - lam * _softmax_result(l2_sc, a2_sc)
    y = o * lax.rsqrt(jnp.mean(o * o, axis=-1, keepdims=True) + RMS_EPS) * g_ref[...]
    o_ref[...] = (y * (1.0 - lam_init)).astype(o_ref.dtype)


def _diff_attention(qkv, lam_vecs, subln, lam_init, batch, seq, width):
    n = qkv.shape[0]
    heads = width // (2 * HEAD_DIM)
    t = _tile(512, seq)
    nq = seq // t
    q_spec, k_spec, v_spec = _attn_specs(seq, t, width, 9, 10, 11, v_mult=2, q_mult=2)
    scratch = [pltpu.VMEM((t, LANES), F32), pltpu.VMEM((t, LANES), F32), pltpu.VMEM((t, 2 * HEAD_DIM), F32)]
    stage = [pltpu.VMEM((2, 2, t, t), F32), pltpu.VMEM((2, 2, t, LANES), F32)]
    return pl.pallas_call(
        functools.partial(_diff_kernel, t=t, lam_init=lam_init),
        grid=(batch, heads, nq),
        in_specs=[q_spec(0), q_spec(1), k_spec(0), k_spec(1), v_spec,
                  pl.BlockSpec(lam_vecs.shape, lambda b, h, i: (0, 0)),
                  pl.BlockSpec((1, 2 * HEAD_DIM), lambda b, h, i: (0, 0))],
        out_specs=pl.BlockSpec((t, 2 * HEAD_DIM), lambda b, h, i: (b * nq + i, h)),
        out_shape=jax.ShapeDtypeStruct((n, width), BF16),
        scratch_shapes=stage + scratch + scratch,
        compiler_params=_params("parallel", "parallel", "arbitrary"),
        name="differential_attention",
    )(qkv, qkv, qkv, qkv, qkv, lam_vecs, subln.reshape(1, -1))


def _merge_kernel(*refs):
    br, wb, gates, o_ref = refs[0:4], refs[4], refs[5:9], refs[9]
    acc = None
    for i in range(N_BRANCHES):
        term = gates[i][...].astype(F32) * jnp.dot(br[i][...], wb[i], preferred_element_type=F32)
        acc = term if acc is None else acc + term
    o_ref[...] = acc.astype(o_ref.dtype)


def _gated_merge(branches, w_branch, gates):
    n, width = branches[0].shape
    d = w_branch.shape[-1]
    tm, tn = _tile(1024, n), _tile(512, d)
    col_tiles = d // tn
    br_spec = pl.BlockSpec((tm, width), lambda i, j: (i, 0))
    gate_specs = [pl.BlockSpec((tm, tn), lambda i, j, g=g: (i, g * col_tiles + j)) for g in range(N_BRANCHES)]
    return pl.pallas_call(
        _merge_kernel,
        grid=(n // tm, col_tiles),
        in_specs=[br_spec] * N_BRANCHES + [pl.BlockSpec((N_BRANCHES, width, tn), lambda i, j: (0, 0, j))] + gate_specs,
        out_specs=pl.BlockSpec((tm, tn), lambda i, j: (i, j)),
        out_shape=jax.ShapeDtypeStruct((n, d), BF16),
        compiler_params=_params("parallel", "arbitrary"),
        name="gated_merge",
    )(*branches, w_branch, *([gates] * N_BRANCHES))


def _pack_halves(y):
    bits = lax.bitcast_convert_type(y.astype(BF16).astype(F32), jnp.uint32)
    half = y.shape[1] // 2
    return (bits[:, :half] >> 16) | bits[:, half:]


def _unpack_halves(p):
    lo = lax.bitcast_convert_type(p << 16, F32)
    hi = lax.bitcast_convert_type(p & jnp.uint32(0xFFFF0000), F32)
    return lo, hi


def _router_kernel(x_ref, g_ref, rw_ref, rb_ref, hp_ref, idx_ref, wt_ref, *, n_experts):
    x = x_ref[...]
    hn = x * lax.rsqrt(jnp.mean(x * x, axis=-1, keepdims=True) + RMS_EPS) * g_ref[...]
    hp_ref[...] = _pack_halves(hn)
    logits = None
    for wi, wp in enumerate(_split3(rw_ref[...])):
        for hi, hp in enumerate(_split3(hn)):
            if wi + hi <= 2:
                term = _dot_nt(wp, hp)
                logits = term if logits is None else logits + term
    score = jax.nn.sigmoid(logits)
    biased = score + rb_ref[...]
    per = n_experts // N_GROUPS
    rows = [biased[e:e + 1, :] for e in range(n_experts)]
    group_score = []
    for g in range(N_GROUPS):
        r = rows[g * per:(g + 1) * per]
        top1 = functools.reduce(jnp.maximum, r)
        first = []
        taken = jnp.zeros_like(top1, dtype=jnp.bool_)
        for v in r:
            is1 = (v == top1) & ~taken
            taken = taken | is1
            first.append(is1)
        rest = [jnp.where(f, NEG_INF, v) for f, v in zip(first, r)]
        top2 = functools.reduce(jnp.maximum, rest)
        second = []
        taken = jnp.zeros_like(top1, dtype=jnp.bool_)
        for v in rest:
            is2 = (v == top2) & ~taken
            taken = taken | is2
            second.append(is2)
        group_score.append((top1 + top2, [a | b for a, b in zip(first, second)]))
    best = functools.reduce(jnp.maximum, [gs for gs, _ in group_score])
    taken = jnp.zeros_like(best, dtype=jnp.bool_)
    picked = []
    for gs, sel in group_score:
        chosen = (gs == best) & ~taken
        taken = taken | chosen
        picked += [s & chosen for s in sel]
    w = [jnp.where(p, score[e:e + 1, :], 0.0) for e, p in enumerate(picked)]
    total = functools.reduce(jnp.add, w)
    inv = 1.0 / total
    lo = functools.reduce(jnp.minimum, [jnp.where(p, e, n_experts) for e, p in enumerate(picked)])
    hi = functools.reduce(jnp.maximum, [jnp.where(p, e, -1) for e, p in enumerate(picked)])
    idx_ref[0:1, :] = lo
    idx_ref[1:2, :] = hi
    wt_ref[0:1, :] = functools.reduce(jnp.add, [jnp.where(lo == e, we, 0.0) for e, we in enumerate(w)]) * inv
    wt_ref[1:2, :] = functools.reduce(jnp.add, [jnp.where(hi == e, we, 0.0) for e, we in enumerate(w)]) * inv


def _router(x, gain, router_w, router_bias):
    n, d = x.shape
    e = router_w.shape[1]
    tm = _tile(256, n)
    return pl.pallas_call(
        functools.partial(_router_kernel, n_experts=e),
        grid=(n // tm,),
        in_specs=[pl.BlockSpec((tm, d), lambda i: (i, 0)), pl.BlockSpec((1, d), lambda i: (0, 0)),
                  pl.BlockSpec((e, d), lambda i: (0, 0)), pl.BlockSpec((e, 1), lambda i: (0, 0))],
        out_specs=[pl.BlockSpec((tm, d // 2), lambda i: (i, 0)),
                   pl.BlockSpec((2, tm), lambda i: (0, i)), pl.BlockSpec((2, tm), lambda i: (0, i))],
        out_shape=[jax.ShapeDtypeStruct((n, d // 2), jnp.uint32),
                   jax.ShapeDtypeStruct((2, n), jnp.int32), jax.ShapeDtypeStruct((2, n), F32)],
        compiler_params=_params("parallel"),
        name="ffn_router",
    )(x, gain.reshape(1, d), router_w.T, router_bias.reshape(e, 1))


MOE_TILE = 256


def _moe_plan(idx, n_experts, tm):
    n = idx.shape[1]
    pairs = 2 * n
    n_rows = pairs + n_experts * tm
    expert = idx.reshape(pairs)
    onehot = (expert[:, None] == jnp.arange(n_experts)[None, :]).astype(jnp.int32)
    rank = jnp.sum((jnp.cumsum(onehot, axis=0) - onehot) * onehot, axis=1)
    group = -(-jnp.sum(onehot, axis=0) // tm) * tm
    group_end = jnp.cumsum(group)
    pos = (group_end - group)[expert] + rank
    pair_of_row = jnp.full((n_rows,), -1, jnp.int32).at[pos].set(jnp.arange(pairs, dtype=jnp.int32))
    is_pad = pair_of_row < 0
    spare = pairs + jnp.cumsum(is_pad.astype(jnp.int32)) - 1
    dst = jnp.where(is_pad, spare, pair_of_row)
    src = jnp.where(is_pad, 0, pair_of_row % n)
    tile_expert = jnp.minimum(
        jnp.searchsorted(group_end, jnp.arange(n_rows // tm, dtype=jnp.int32) * tm, side="right"), n_experts - 1)
    return src.reshape(-1, tm), dst.reshape(-1, tm), tile_expert.astype(jnp.int32)


def _expert_kernel(te_ref, src_ref, src_next_ref, dst_ref, x_hbm, wgu_ref, wd_ref, y_hbm,
                   xbuf, ybuf, gsem, ssem, *, tm, d_expert):
    t, n_tiles = pl.program_id(0), pl.num_programs(0)
    slot = t % 2

    def row_copy_in(ids_ref, s, r):
        return pltpu.make_async_copy(x_hbm.at[pl.ds(ids_ref[0, r], 1), :], xbuf.at[s, pl.ds(r, 1), :], gsem.at[s])

    def row_copy_out(s, r):
        return pltpu.make_async_copy(ybuf.at[s, pl.ds(r, 1), :], y_hbm.at[pl.ds(dst_ref[0, r], 1), :], ssem.at[s])

    def each_row(fn):
        lax.fori_loop(0, tm, lambda r, c: (fn(r), c)[1], 0, unroll=8)

    def gather(ids_ref, s):
        each_row(lambda r: row_copy_in(ids_ref, s, r).start())

    def gather_wait(s):
        each_row(lambda r: row_copy_in(src_ref, s, r).wait())

    def scatter_start(s):
        each_row(lambda r: row_copy_out(s, r).start(priority=1))

    def scatter_wait(s):
        each_row(lambda r: row_copy_out(s, r).wait())

    @pl.when(t == 0)
    def _():
        gather(src_ref, 0)

    @pl.when(t + 1 < n_tiles)
    def _():
        gather(src_next_ref, 1 - slot)

    gather_wait(slot)
    xn = jnp.concatenate(_unpack_halves(xbuf[slot]), axis=1).astype(BF16)
    h = jnp.dot(xn, wgu_ref[...], preferred_element_type=F32)
    hg, hu = h[:, :d_expert], h[:, d_expert:]
    act = (hg * jax.nn.sigmoid(hg) * hu).astype(BF16)

    @pl.when(t >= 2)
    def _():
        scatter_wait(slot)

    ybuf[slot] = _pack_halves(jnp.dot(act, wd_ref[...], preferred_element_type=F32))
    scatter_start(slot)

    @pl.when(t == n_tiles - 1)
    def _():
        scatter_wait(slot)

        @pl.when(t >= 1)
        def _():
            scatter_wait(1 - slot)


def _moe_experts(hp, idx, wgu, wd):
    n, half = hp.shape
    d = 2 * half
    n_experts, _, two_f = wgu.shape
    tm = MOE_TILE
    src, dst, tile_expert = _moe_plan(idx, n_experts, tm)
    n_tiles = src.shape[0]
    src, dst = src.reshape(n_tiles, 1, tm), dst.reshape(n_tiles, 1, tm)
    smem = lambda fn: pl.BlockSpec((None, 1, tm), fn, memory_space=pltpu.SMEM)
    grid_spec = pltpu.PrefetchScalarGridSpec(
        num_scalar_prefetch=1,
        grid=(n_tiles,),
        in_specs=[smem(lambda t, te: (t, 0, 0)),
                  smem(lambda t, te: (jnp.minimum(t + 1, n_tiles - 1), 0, 0)),
                  smem(lambda t, te: (t, 0, 0)),
                  pl.BlockSpec(memory_space=pl.ANY),
                  pl.BlockSpec((None, d, two_f), lambda t, te: (te[t], 0, 0)),
                  pl.BlockSpec((None, two_f // 2, d), lambda t, te: (te[t], 0, 0))],
        out_specs=pl.BlockSpec(memory_space=pl.ANY),
        scratch_shapes=[pltpu.VMEM((2, tm, half), jnp.uint32), pltpu.VMEM((2, tm, half), jnp.uint32),
                        pltpu.SemaphoreType.DMA((2,)), pltpu.SemaphoreType.DMA((2,))])
    return pl.pallas_call(
        functools.partial(_expert_kernel, tm=tm, d_expert=two_f // 2),
        grid_spec=grid_spec,
        out_shape=jax.ShapeDtypeStruct((n_tiles * tm, half), jnp.uint32),
        compiler_params=_params("arbitrary"),
        name="moe_experts",
    )(tile_expert, src, src, dst, hp, wgu, wd)


def _combine_kernel(x_ref, ya_ref, yb_ref, w_ref, o_ref):
    w = w_ref[...]
    half = ya_ref.shape[1]
    for c, (a, b) in enumerate(zip(_unpack_halves(ya_ref[...]), _unpack_halves(yb_ref[...]))):
        cols = slice(c * half, (c + 1) * half)
        o_ref[:, cols] = x_ref[:, cols] + w[:, 0:1] * a + w[:, 1:2] * b


def _moe_combine(x, y, wts):
    n, d = x.shape
    tm = _tile(256, n)
    nt = n // tm
    return pl.pallas_call(
        _combine_kernel,
        grid=(nt,),
        in_specs=[pl.BlockSpec((tm, d), lambda i: (i, 0)), pl.BlockSpec((tm, d // 2), lambda i: (i, 0)),
                  pl.BlockSpec((tm, d // 2), lambda i: (nt + i, 0)), pl.BlockSpec((tm, 2), lambda i: (i, 0))],
        out_specs=pl.BlockSpec((tm, d), lambda i: (i, 0)),
        out_shape=jax.ShapeDtypeStruct((n, d), F32),
        compiler_params=_params("parallel"),
        name="moe_combine",
    )(x, y, y, wts)


def _rope_tables(seq):
    half = HEAD_DIM // 2
    inv_freq = jnp.power(ROPE_THETA, -jnp.arange(half, dtype=F32) / half)
    ang = jnp.arange(seq, dtype=F32)[:, None] * inv_freq[None, :]
    cos, sin = jnp.cos(ang), jnp.sin(ang)
    return jnp.concatenate([cos, cos], axis=-1), jnp.concatenate([-sin, sin], axis=-1)


def kernel(x, attn_norm, w_in, fox_forget_bias, fox_q_norm, fox_k_norm, ch_q_norm, ch_k_norm, ch_rel_bias,
           diff_q_norm, diff_k_norm, diff_lambda, diff_subln, w_branch, w_out, ffn_norm, router_w, router_bias,
           w_gate, w_up, w_down):
    batch, seq, d_model = x.shape
    depth = w_in.shape[0]
    width = d_model // N_BRANCHES
    fox_heads = width // HEAD_DIM
    qkv_cols = 3 * width
    n_experts, _, d_expert = w_gate.shape[1:]
    n = batch * seq
    scale = HEAD_DIM ** -0.5
    assert fox_heads <= LANES and width % (2 * HEAD_DIM) == 0 and seq % CHUNK == 0

    cos_t, sin_t = _rope_tables(seq)
    ones = jnp.ones((HEAD_DIM,), F32)
    res_spec = lambda tm, tn: pl.BlockSpec((tm, tn), lambda i, j: (i, j))
    xf = x.reshape(n, d_model)

    for layer in range(depth):
        w_proj, col_amax = _stage_in_proj(w_in, layer, qkv_cols, fox_heads)
        w_qkv = w_proj
        w_f = jnp.pad(w_in[layer, :, qkv_cols:qkv_cols + fox_heads], ((0, 0), (0, LANES - fox_heads))).astype(BF16)
        gate_col0 = 4 * qkv_cols
        w_g8 = _quantize_cols(w_proj, col_amax, gate_col0, N_BRANCHES * d_model)
        g_scale = col_amax[:, gate_col0:] * (1.0 / FP8_MAX)
        f_bias = jnp.pad(fox_forget_bias[layer], (0, LANES - fox_heads)).reshape(1, LANES)
        gains = jnp.stack([
            fox_q_norm[layer] * scale, fox_k_norm[layer], ones,
            ones * scale, ones, ones,
            ch_q_norm[layer] * scale, ch_k_norm[layer], ones,
            diff_q_norm[layer] * scale, diff_k_norm[layer], ones,
            ones, ones, ones, ones])

        h, h8, h_scale = _rmsnorm(xf, attn_norm[layer])
        qkv = _qkv_proj(h, w_qkv, gains, cos_t, sin_t, seq, width)
        gates = _matmul(h8, w_g8, _epi_scaled_sigmoid, BF16, extras=(h_scale, g_scale),
                        extra_specs=(lambda tm, tn: pl.BlockSpec((tm, LANES), lambda i, j: (i, 0)),
                                     lambda tm, tn: pl.BlockSpec((1, tn), lambda i, j: (0, j))),
                        name="gate_proj")
        cum = _forget_cumsum(h, w_f, f_bias, batch, seq)
        cum_t = cum[:, :fox_heads].T

        lam_init = 0.8 - 0.6 * math.exp(-0.3 * layer)
        branches = (
            _fox_attention(qkv, cum, cum_t, batch, seq, width),
            _sb_attention(qkv, batch, seq, width),
            _band_attention(qkv, ch_rel_bias[layer], batch, seq, width),
            _diff_attention(qkv, diff_lambda[layer], diff_subln[layer], lam_init, batch, seq, width),
        )
        merged = _gated_merge(branches, _cast_layer(w_branch, layer).reshape(w_branch.shape[1:]), gates)
        xf = _matmul(merged, _cast_layer(w_out, layer), _epi_residual, F32, extras=(xf,), extra_specs=(res_spec,),
                     tn=512, name="out_proj")

        hp, idx, wts = _router(xf, ffn_norm[layer], router_w, router_bias)
        wgu = _cast_gate_up(w_gate, w_up, layer)
        y = _moe_experts(hp, idx, wgu, _cast_layer(w_down, layer).reshape(w_down.shape[1:]))
        xf = _moe_combine(xf, y, wts.T)
    return xf.reshape(batch, seq, d_model)
```

```python
import functools
import math

import jax
import jax.numpy as jnp
from jax import lax
from jax.experimental import pallas as pl
from jax.experimental.pallas import tpu as pltpu

HEAD_DIM = 128
LANES = 128
CHUNK = 64
N_BRANCHES = 4
CH_LEFT_CHUNKS = 8
REL_CLIP = 128
ROPE_THETA = 10000.0
N_GROUPS = 4
RMS_EPS = 1e-6
NEG_INF = float("-inf")
VMEM_LIMIT_BYTES = 56 * 1024 * 1024

F32 = jnp.float32
BF16 = jnp.bfloat16


def _tile(pref, n):
    t = min(pref, n)
    while n % t:
        t //= 2
    return t


def _params(*sem):
    return pltpu.CompilerParams(dimension_semantics=sem, vmem_limit_bytes=VMEM_LIMIT_BYTES)


def _dot_nt(a, b):
    return lax.dot_general(a, b, (((1,), (1,)), ((), ())), preferred_element_type=F32)


def _log_sigmoid(x):
    return jnp.minimum(x, 0.0) - jnp.log1p(jnp.exp(-jnp.abs(x)))


FP8 = jnp.float8_e4m3fn
FP8_MAX = 448.0


def _rmsnorm_kernel(x_ref, g_ref, o_ref, o8_ref, s_ref):
    x = x_ref[...]
    ms = jnp.mean(x * x, axis=-1, keepdims=True)
    y = x * lax.rsqrt(ms + RMS_EPS) * g_ref[...]
    o_ref[...] = y.astype(o_ref.dtype)
    amax = jnp.max(jnp.abs(y), axis=-1, keepdims=True)
    o8_ref[...] = (y * jnp.where(amax > 0.0, FP8_MAX / amax, 0.0)).astype(o8_ref.dtype)
    s_ref[...] = jnp.broadcast_to(amax * (1.0 / FP8_MAX), s_ref.shape)


def _rmsnorm(x, gain):
    n, d = x.shape
    tm = _tile(256, n)
    row = pl.BlockSpec((tm, d), lambda i: (i, 0))
    return pl.pallas_call(
        _rmsnorm_kernel,
        grid=(n // tm,),
        in_specs=[row, pl.BlockSpec((1, d), lambda i: (0, 0))],
        out_specs=[row, row, pl.BlockSpec((tm, LANES), lambda i: (i, 0))],
        out_shape=[jax.ShapeDtypeStruct((n, d), BF16), jax.ShapeDtypeStruct((n, d), FP8),
                   jax.ShapeDtypeStruct((n, LANES), F32)],
        compiler_params=_params("parallel"),
        name="attn_rmsnorm",
    )(x, gain.reshape(1, d))


def _cast_kernel(w_ref, o_ref):
    o_ref[...] = w_ref[...].astype(o_ref.dtype)


def _cast_layer(w, layer):
    cols = w.shape[-1]
    rows = math.prod(w.shape[1:-1])
    w2 = w.reshape(w.shape[0] * rows, cols)
    tr, tc = _tile(1024, rows), _tile(2048, cols)
    return pl.pallas_call(
        _cast_kernel,
        grid=(rows // tr, cols // tc),
        in_specs=[pl.BlockSpec((tr, tc), lambda i, j: (layer * (rows // tr) + i, j))],
        out_specs=pl.BlockSpec((tr, tc), lambda i, j: (i, j)),
        out_shape=jax.ShapeDtypeStruct((rows, cols), BF16),
        compiler_params=_params("parallel", "parallel"),
        name="cast_weights",
    )(w2)


def _pair_cast_kernel(a_ref, b_ref, o_ref):
    f = a_ref.shape[-1]
    o_ref[:, :f] = a_ref[...].astype(o_ref.dtype)
    o_ref[:, f:] = b_ref[...].astype(o_ref.dtype)


def _cast_gate_up(w_gate, w_up, layer):
    _, e, d, f = w_gate.shape
    tr = _tile(1024, d)
    spec = pl.BlockSpec((None, None, tr, f), lambda x, i: (layer, x, i, 0))
    return pl.pallas_call(
        _pair_cast_kernel,
        grid=(e, d // tr),
        in_specs=[spec, spec],
        out_specs=pl.BlockSpec((None, tr, 2 * f), lambda x, i: (x, i, 0)),
        out_shape=jax.ShapeDtypeStruct((e, d, 2 * f), BF16),
        compiler_params=_params("parallel", "parallel"),
        name="cast_gate_up",
    )(w_gate, w_up)


def _in_proj_kernel(a_ref, b_ref, o_ref, amax_ref, *, first_shifted, shift):
    j, i = pl.program_id(0), pl.program_id(1)

    def emit(w):
        wb = w.astype(o_ref.dtype)
        o_ref[...] = wb
        col_max = jnp.max(jnp.abs(wb.astype(F32)), axis=0, keepdims=True)
        amax_ref[...] = jnp.where(i == 0, col_max, jnp.maximum(amax_ref[...], col_max))

    @pl.when(j < first_shifted)
    def _():
        emit(a_ref[...])

    @pl.when(j >= first_shifted)
    def _():
        tc = a_ref.shape[1]
        emit(jnp.concatenate([a_ref[...], b_ref[...]], axis=1)[:, shift:shift + tc])


def _stage_in_proj(w_in, layer, keep_cols, drop_cols):
    _, d, cols = w_in.shape
    out_cols = cols - drop_cols
    tc = _tile(1024, math.gcd(keep_cols, out_cols))
    tr = _tile(2048, d)
    assert tc % LANES == 0 and drop_cols < LANES
    lane_blocks = tc // LANES
    return pl.pallas_call(
        functools.partial(_in_proj_kernel, first_shifted=keep_cols // tc, shift=drop_cols),
        grid=(out_cols // tc, d // tr),
        in_specs=[pl.BlockSpec((None, tr, tc), lambda j, i: (layer, i, j)),
                  pl.BlockSpec((None, tr, LANES), lambda j, i: (layer, i, (j + 1) * lane_blocks))],
        out_specs=[pl.BlockSpec((tr, tc), lambda j, i: (i, j)), pl.BlockSpec((1, tc), lambda j, i: (0, j))],
        out_shape=[jax.ShapeDtypeStruct((d, out_cols), BF16), jax.ShapeDtypeStruct((1, out_cols), F32)],
        compiler_params=_params("parallel", "arbitrary"),
        name="stage_in_proj",
    )(w_in, w_in)


def _lane_group_kernel(w_ref, o_ref, *, ncols):
    lane = lax.broadcasted_iota(jnp.int32, w_ref.shape, 1)
    o_ref[...] = jnp.where(lane < ncols, w_ref[...], 0.0).astype(o_ref.dtype)


def _cast_lane_group(w, layer, col0, ncols):
    _, d, _ = w.shape
    assert col0 % LANES == 0 and ncols <= LANES
    tr = _tile(2048, d)
    return pl.pallas_call(
        functools.partial(_lane_group_kernel, ncols=ncols),
        grid=(d // tr,),
        in_specs=[pl.BlockSpec((None, tr, LANES), lambda i: (layer, i, col0 // LANES))],
        out_specs=pl.BlockSpec((tr, LANES), lambda i: (i, 0)),
        out_shape=jax.ShapeDtypeStruct((d, LANES), BF16),
        compiler_params=_params("parallel"),
        name="cast_forget_weights",
    )(w)


def _quantize_kernel(w_ref, amax_ref, o_ref):
    amax = amax_ref[...]
    o_ref[...] = (w_ref[...].astype(F32) * jnp.where(amax > 0.0, FP8_MAX / amax, 0.0)).astype(o_ref.dtype)


def _quantize_cols(w, amax, col0, ncols):
    d = w.shape[0]
    tr, tc = _tile(2048, d), _tile(1024, math.gcd(col0, ncols))
    assert tc % LANES == 0
    off = col0 // tc
    return pl.pallas_call(
        _quantize_kernel,
        grid=(d // tr, ncols // tc),
        in_specs=[pl.BlockSpec((tr, tc), lambda i, j: (i, off + j)), pl.BlockSpec((1, tc), lambda i, j: (0, off + j))],
        out_specs=pl.BlockSpec((tr, tc), lambda i, j: (i, j)),
        out_shape=jax.ShapeDtypeStruct((d, ncols), FP8),
        compiler_params=_params("parallel", "parallel"),
        name="quantize_gate_weights",
    )(w, amax)


def _matmul(a, w, epilogue, out_dtype, extras=(), extra_specs=(), tm=1024, tn=1024, n_cols=None, name="matmul"):
    m, k = a.shape
    n = w.shape[1] if n_cols is None else n_cols
    tm, tn = _tile(tm, m), _tile(tn, n)

    def kern(a_ref, w_ref, *refs):
        acc = jnp.dot(a_ref[...], w_ref[...], preferred_element_type=F32)
        epilogue(acc, refs[:-1], refs[-1])

    return pl.pallas_call(
        kern,
        grid=(m // tm, n // tn),
        in_specs=[pl.BlockSpec((tm, k), lambda i, j: (i, 0)), pl.BlockSpec((k, tn), lambda i, j: (0, j)),
                  *[mk(tm, tn) for mk in extra_specs]],
        out_specs=pl.BlockSpec((tm, tn), lambda i, j: (i, j)),
        out_shape=jax.ShapeDtypeStruct((m, n), out_dtype),
        compiler_params=_params("parallel", "arbitrary"),
        name=name,
    )(a, w, *extras)


def _epi_scaled_sigmoid(acc, refs, o_ref):
    row_scale, col_scale = refs
    o_ref[...] = jax.nn.sigmoid(acc * row_scale[:, :1] * col_scale[...]).astype(o_ref.dtype)


def _epi_residual(acc, refs, o_ref):
    o_ref[...] = refs[0][...] + acc


PLAIN, NORM, NORM_ROPE = 0, 1, 2
SECTION_KINDS = (NORM, NORM, PLAIN, PLAIN, PLAIN, PLAIN, NORM, NORM, PLAIN, NORM_ROPE, NORM_ROPE, PLAIN)


def _qkv_epilogue(acc, refs, o_ref, *, sec_tiles):
    gain_ref, cos_ref, sin_ref = refs
    sec = pl.program_id(1) // sec_tiles
    g = gain_ref[pl.ds(sec, 1), :]
    tn = acc.shape[1]

    def run(kind):
        for hh in range(tn // HEAD_DIM):
            y = acc[:, hh * HEAD_DIM:(hh + 1) * HEAD_DIM]
            if kind != PLAIN:
                y = y * lax.rsqrt(jnp.mean(y * y, axis=-1, keepdims=True) + RMS_EPS)
            y = y * g
            if kind == NORM_ROPE:
                y = y * cos_ref[...] + pltpu.roll(y, HEAD_DIM // 2, 1) * sin_ref[...]
            o_ref[:, hh * HEAD_DIM:(hh + 1) * HEAD_DIM] = y.astype(o_ref.dtype)

    for kind in (PLAIN, NORM, NORM_ROPE):
        hit = functools.reduce(jnp.logical_or, [sec == s for s, kd in enumerate(SECTION_KINDS) if kd == kind])
        pl.when(hit)(functools.partial(run, kind))


def _qkv_proj(h, w_qkv, gains, cos_t, sin_t, seq, width):
    n = h.shape[0]
    tm = _tile(1024, seq)
    tn = _tile(1024, width)
    seq_tiles = seq // tm
    row_spec = lambda tm_, tn_: pl.BlockSpec((tm_, LANES), lambda i, j: (i % seq_tiles, 0))
    gain_spec = lambda tm_, tn_: pl.BlockSpec(gains.shape, lambda i, j: (0, 0))
    return _matmul(h, w_qkv, functools.partial(_qkv_epilogue, sec_tiles=width // tn), BF16,
                   extras=(gains, cos_t, sin_t), extra_specs=(gain_spec, row_spec, row_spec),
                   tm=tm, tn=tn, n_cols=len(SECTION_KINDS) * width, name="qkv_proj")


def _split3(x):
    hi = x.astype(BF16)
    r = x - hi.astype(F32)
    mid = r.astype(BF16)
    lo = (r - mid.astype(F32)).astype(BF16)
    return hi, mid, lo


def _fgate_kernel(h_ref, w_ref, b_ref, tri_ref, o_ref, carry_sc):
    @pl.when(pl.program_id(1) == 0)
    def _():
        carry_sc[...] = jnp.zeros_like(carry_sc)

    f = jnp.dot(h_ref[...], w_ref[...], preferred_element_type=F32) + b_ref[...]
    lf = _log_sigmoid(f)
    tri = tri_ref[...]
    c = carry_sc[...]
    for part in _split3(lf):
        c = c + jnp.dot(tri, part, preferred_element_type=F32)
    o_ref[...] = c
    carry_sc[...] = c[-1:, :]


def _forget_cumsum(h, w_f, bias, batch, seq):
    n, d = h.shape
    ts = _tile(512, seq)
    nt = seq // ts
    tri = (jnp.arange(ts)[:, None] >= jnp.arange(ts)[None, :]).astype(BF16)
    return pl.pallas_call(
        _fgate_kernel,
        grid=(batch, nt),
        in_specs=[pl.BlockSpec((ts, d), lambda b, t: (b * nt + t, 0)),
                  pl.BlockSpec((d, LANES), lambda b, t: (0, 0)),
                  pl.BlockSpec((1, LANES), lambda b, t: (0, 0)),
                  pl.BlockSpec((ts, ts), lambda b, t: (0, 0))],
        out_specs=pl.BlockSpec((ts, LANES), lambda b, t: (b * nt + t, 0)),
        out_shape=jax.ShapeDtypeStruct((n, LANES), F32),
        scratch_shapes=[pltpu.VMEM((1, LANES), F32)],
        compiler_params=_params("parallel", "arbitrary"),
        name="forget_cumsum",
    )(h, w_f, bias, tri)


def _softmax_step(s, v, m_sc, l_sc, acc_sc):
    m_prev = m_sc[...]
    m_new = jnp.maximum(m_prev, jnp.max(s, axis=1, keepdims=True))
    alpha = jnp.exp(m_prev - m_new)
    p = jnp.exp(s - m_new)
    l_sc[...] = alpha * l_sc[...] + jnp.sum(p, axis=1, keepdims=True)
    acc_sc[...] = alpha * acc_sc[...] + jnp.dot(p.astype(v.dtype), v, preferred_element_type=F32)
    m_sc[...] = m_new


def _softmax_init(m_sc, l_sc, acc_sc):
    m_sc[...] = jnp.full_like(m_sc, NEG_INF)
    l_sc[...] = jnp.zeros_like(l_sc)
    acc_sc[...] = jnp.zeros_like(acc_sc)


def _attn_specs(seq, tq, width, q_sec, k_sec, v_sec, v_mult=1, q_mult=1):
    nq = seq // tq
    sec = width // HEAD_DIM
    q_spec = lambda c=0: pl.BlockSpec((tq, HEAD_DIM), lambda b, h, i: (b * nq + i, q_sec * sec + h * q_mult + c))
    k_spec = lambda c=0: pl.BlockSpec((seq, HEAD_DIM), lambda b, h, i: (b, k_sec * sec + h * q_mult + c))
    v_spec = pl.BlockSpec((seq, HEAD_DIM * v_mult), lambda b, h, i: (b, v_sec * sec // v_mult + h))
    return q_spec, k_spec, v_spec


def _rep(x, n):
    return x if n == LANES else jnp.concatenate([x] * (n // LANES), axis=1)


def _lane_fold(x, op):
    out = x[:, :LANES]
    for c in range(1, x.shape[1] // LANES):
        out = op(out, x[:, c * LANES:(c + 1) * LANES])
    return out


def _row_max(s):
    return jnp.broadcast_to(jnp.max(_lane_fold(s, jnp.maximum), axis=1, keepdims=True), (s.shape[0], LANES))


def _softmax_update(s, mx, v, m_sc, l_sc, acc_sc):
    m_prev = m_sc[...]
    m_new = jnp.maximum(m_prev, mx)
    alpha = jnp.exp(m_prev - m_new)
    p = jnp.exp(s - _rep(m_new, s.shape[1]))
    l_sc[...] = alpha * l_sc[...] + _lane_fold(p, jnp.add)
    acc_sc[...] = (_rep(alpha, acc_sc.shape[1]) * acc_sc[...]
                   + jnp.dot(p.astype(v.dtype), v, preferred_element_type=F32))
    m_sc[...] = m_new


def _softmax_result(l_sc, acc_sc):
    return acc_sc[...] * (1.0 / jnp.sum(l_sc[...], axis=1, keepdims=True))


def _pipelined_causal(qi, score, update):
    @pl.when(qi == 0)
    def _():
        score(0, True)

    @pl.when(qi > 0)
    def _():
        score(0, False)

    def body(j, carry):
        update(j)
        score(j + 1, False)
        return carry

    lax.fori_loop(0, qi - 1, body, 0)

    @pl.when(qi > 0)
    def _():
        update(qi - 1)
        score(qi, True)

    update(qi)


def _tile_rows(i, t):
    return pl.ds(i * t, t) if isinstance(i, int) else pl.ds(pl.multiple_of(i * t, t), t)


def _flat_pipeline(n_pairs, advance, score, update, finish):
    score((0, 0), 0)

    def body(p, pair):
        nxt = advance(pair)
        update(pair, p % 2)
        score(nxt, (p + 1) % 2)
        finish(pair)
        return nxt

    last = lax.fori_loop(0, n_pairs - 1, body, (jnp.int32(0), jnp.int32(0)))
    update(last, (n_pairs - 1) % 2)
    finish(last)


def _causal_advance(pair):
    qi, j = pair
    more = j < qi
    return jnp.where(more, qi, qi + 1), jnp.where(more, j + 1, 0)


def _softmax_flat_update(pair, s, mx, v, m_sc, l_sc, acc_sc):
    m_prev = jnp.where(pair[1] == 0, NEG_INF, m_sc[...])
    m_new = jnp.maximum(m_prev, mx)
    alpha = jnp.exp(m_prev - m_new)
    p = jnp.exp(s - _rep(m_new, s.shape[1]))
    l_sc[...] = alpha * l_sc[...] + _lane_fold(p, jnp.add)
    acc_sc[...] = (_rep(alpha, acc_sc.shape[1]) * acc_sc[...]
                   + jnp.dot(p.astype(v.dtype), v, preferred_element_type=F32))
    m_sc[...] = m_new


def _diag_mask_init(mask_sc, visible):
    t = mask_sc.shape[-1]
    row = lax.broadcasted_iota(jnp.int32, (t, t), 0)
    col = lax.broadcasted_iota(jnp.int32, (t, t), 1)
    mask_sc[0] = jnp.zeros((t, t), F32)
    mask_sc[1] = jnp.where(visible(row, col), 0.0, NEG_INF)


def _is_diag(qi, j):
    return int(j == qi) if isinstance(j, int) else jnp.where(j < qi, 0, 1)


def _fox_kernel(q_ref, k_ref, v_ref, ct_ref, cs_ref, o_ref, s_buf, mx_buf, mask_sc, m_sc, l_sc, acc_sc, ct_sc,
                *, t, nq):
    h = pl.program_id(1)
    lane = lax.broadcasted_iota(jnp.int32, ct_ref.shape, 1)
    ct = jnp.sum(jnp.where(lane == h, ct_ref[...], 0.0), axis=1, keepdims=True)
    ct_sc[...] = jnp.broadcast_to(ct, ct_sc.shape)
    _softmax_init(m_sc, l_sc, acc_sc)
    _diag_mask_init(mask_sc, lambda row, col: col <= row)

    def score(pair, slot):
        qi, j = pair
        rows = _tile_rows(qi, t)
        s = _dot_nt(q_ref[rows, :], k_ref[_tile_rows(j, t), :])
        s = s + _rep(ct_sc[rows, :], t) - cs_ref[pl.ds(j, 1), :] + mask_sc[_is_diag(qi, j)]
        s_buf[slot] = s
        mx_buf[slot] = _row_max(s)

    def update(pair, slot):
        _softmax_flat_update(pair, s_buf[slot], mx_buf[slot], v_ref[_tile_rows(pair[1], t), :], m_sc, l_sc, acc_sc)

    def finish(pair):
        qi, j = pair

        @pl.when(j == qi)
        def _():
            o_ref[_tile_rows(qi, t), :] = _softmax_result(l_sc, acc_sc).astype(o_ref.dtype)

    _flat_pipeline(nq * (nq + 1) // 2, _causal_advance, score, update, finish)


def _fox_attention(qkv, cum, cum_t, batch, seq, width):
    n = qkv.shape[0]
    heads = width // HEAD_DIM
    t = _tile(512, seq)
    nq = seq // t
    head_cols = lambda s: pl.BlockSpec((seq, HEAD_DIM), lambda b, h: (b, s * heads + h))
    stat = pltpu.VMEM((t, LANES), F32)
    return pl.pallas_call(
        functools.partial(_fox_kernel, t=t, nq=nq),
        grid=(batch, heads),
        in_specs=[head_cols(0), head_cols(1), head_cols(2),
                  pl.BlockSpec((seq, LANES), lambda b, h: (b, 0)),
                  pl.BlockSpec((None, nq, t), lambda b, h: (h, b, 0))],
        out_specs=pl.BlockSpec((seq, HEAD_DIM), lambda b, h: (b, h)),
        out_shape=jax.ShapeDtypeStruct((n, width), BF16),
        scratch_shapes=[pltpu.VMEM((2, t, t), F32), pltpu.VMEM((2, t, LANES), F32), pltpu.VMEM((2, t, t), F32),
                        stat, stat, pltpu.VMEM((t, HEAD_DIM), F32), pltpu.VMEM((seq, LANES), F32)],
        compiler_params=_params("parallel", "parallel"),
        name="fox_attention",
    )(qkv, qkv, qkv, cum, cum_t.reshape(heads, n // t, t))


SB_DEAD_LOG = -110.0


def _sb_kernel(q_ref, k_ref, v_ref, u_ref, o_ref, e_buf, rs_buf, c_sc, acc_sc, *, t):
    qi = pl.program_id(2)
    c_sc[...] = jnp.zeros_like(c_sc)
    acc_sc[...] = jnp.zeros_like(acc_sc)

    def score(jj, diagonal):
        ks = pl.multiple_of((qi - jj) * t, t)
        z = _dot_nt(q_ref[...], k_ref[pl.ds(ks, t), :])
        softplus = jnp.log(1.0 + jnp.exp(-jnp.abs(z)))
        log_beta = jnp.minimum(z, 0.0) - softplus
        log_rest = jnp.minimum(-z, 0.0) - softplus
        if diagonal:
            row = lax.broadcasted_iota(jnp.int32, (t, t), 0)
            col = lax.broadcasted_iota(jnp.int32, (t, t), 1)
            strict = col < row
            log_rest = jnp.where(strict, log_rest, 0.0)
        hi = log_rest.astype(BF16)
        lo = (log_rest - hi.astype(F32)).astype(BF16)
        u = u_ref[...]
        cw = u.shape[0]
        later, tail = [], None
        for c in reversed(range(t // cw)):
            sl = slice(c * cw, (c + 1) * cw)
            inside = jnp.dot(hi[:, sl], u, preferred_element_type=F32) + jnp.dot(lo[:, sl], u, preferred_element_type=F32)
            later.insert(0, inside if tail is None else inside + _rep(tail, cw))
            chunk_sum = jnp.broadcast_to(
                jnp.sum(_lane_fold(log_rest[:, sl], jnp.add), axis=1, keepdims=True), (t, LANES))
            tail = chunk_sum if tail is None else tail + chunk_sum
        e = log_beta + jnp.concatenate(later, axis=1)
        if diagonal:
            e = jnp.where(strict, e, NEG_INF)
        e_buf[jj % 2] = e
        rs_buf[jj % 2] = tail

    def update(jj):
        ks = pl.multiple_of((qi - jj) * t, t)
        c = c_sc[...]
        w = jnp.exp(e_buf[jj % 2] + _rep(c, t))
        v = v_ref[pl.ds(ks, t), :]
        acc_sc[...] += jnp.dot(w.astype(v.dtype), v, preferred_element_type=F32)
        c_sc[...] = c + rs_buf[jj % 2]

    score(0, True)

    def body(state):
        jj, _ = state
        update(jj)
        score(jj + 1, False)
        return jj + 1, jnp.max(c_sc[...])

    jj, c_max = lax.while_loop(lambda st: (st[0] < qi) & (st[1] > SB_DEAD_LOG), body, (0, jnp.float32(0.0)))
    pl.when(c_max > SB_DEAD_LOG)(lambda: update(jj))
    o_ref[...] = acc_sc[...].astype(o_ref.dtype)


def _sb_attention(qkv, batch, seq, width):
    n = qkv.shape[0]
    heads = width // HEAD_DIM
    t = _tile(512, seq)
    nq = seq // t
    q_spec, k_spec, v_spec = _attn_specs(seq, t, width, 3, 4, 5)
    cw = min(256, t)
    upper = (jnp.arange(cw)[:, None] > jnp.arange(cw)[None, :]).astype(BF16)
    return pl.pallas_call(
        functools.partial(_sb_kernel, t=t),
        grid=(batch, heads, nq),
        in_specs=[q_spec(), k_spec(), v_spec, pl.BlockSpec((cw, cw), lambda b, h, i: (0, 0))],
        out_specs=pl.BlockSpec((t, HEAD_DIM), lambda b, h, i: (b * nq + i, h)),
        out_shape=jax.ShapeDtypeStruct((n, width), BF16),
        scratch_shapes=[pltpu.VMEM((2, t, t), F32), pltpu.VMEM((2, t, LANES), F32),
                        pltpu.VMEM((t, LANES), F32), pltpu.VMEM((t, HEAD_DIM), F32)],
        compiler_params=_params("parallel", "parallel", "arbitrary"),
        name="stick_breaking_attention",
    )(qkv, qkv, qkv, upper)


def _band_kernel(q_ref, k_ref, v_ref, tab_ref, o_ref, bias_sc, s_buf, mx_buf, m_sc, l_sc, acc_sc, *, t, nq, n_back):
    width = tab_ref.shape[-1]
    rolled = pltpu.roll(jnp.broadcast_to(tab_ref[...], (t, width)), 0, 1, stride=1, stride_axis=0)
    row = lax.broadcasted_iota(jnp.int32, (t, t), 0)
    col = lax.broadcasted_iota(jnp.int32, (t, t), 1)
    for d in range(n_back + 1):
        gap = (row + d * t) // CHUNK - col // CHUNK
        visible = (gap >= 0) & (gap <= CH_LEFT_CHUNKS)
        c0 = (n_back + 1 - d) * t
        bias_sc[d] = jnp.where(visible, rolled[:, c0:c0 + t], NEG_INF)
    _softmax_init(m_sc, l_sc, acc_sc)

    def advance(pair):
        qi, d = pair
        more = d < jnp.minimum(qi, n_back)
        return jnp.where(more, qi, qi + 1), jnp.where(more, d + 1, 0)

    def score(pair, slot):
        qi, d = pair
        s = _dot_nt(q_ref[_tile_rows(qi, t), :], k_ref[_tile_rows(qi - d, t), :]) + bias_sc[d]
        s_buf[slot] = s
        mx_buf[slot] = _row_max(s)

    def update(pair, slot):
        qi, d = pair
        _softmax_flat_update(pair, s_buf[slot], mx_buf[slot], v_ref[_tile_rows(qi - d, t), :], m_sc, l_sc, acc_sc)

    def finish(pair):
        qi, d = pair

        @pl.when(d == jnp.minimum(qi, n_back))
        def _():
            o_ref[_tile_rows(qi, t), :] = _softmax_result(l_sc, acc_sc).astype(o_ref.dtype)

    lead = min(nq, n_back)
    _flat_pipeline(nq * (n_back + 1) - lead * n_back + lead * (lead - 1) // 2, advance, score, update, finish)


def _band_table(rel_bias, t, n_back):
    rel = (n_back + 1) * t - jnp.arange((n_back + 2) * t)
    return rel_bias[:, None, jnp.clip(rel, -REL_CLIP, REL_CLIP) + REL_CLIP].astype(F32)


def _band_attention(qkv, rel_bias, batch, seq, width):
    n = qkv.shape[0]
    heads = width // HEAD_DIM
    t = _tile(512, seq)
    nq = seq // t
    n_back = -(-(CH_LEFT_CHUNKS * CHUNK) // t)
    table = _band_table(rel_bias, t, n_back)
    head_cols = lambda s: pl.BlockSpec((seq, HEAD_DIM), lambda b, h: (b, s * heads + h))
    stat = pltpu.VMEM((t, LANES), F32)
    return pl.pallas_call(
        functools.partial(_band_kernel, t=t, nq=nq, n_back=n_back),
        grid=(batch, heads),
        in_specs=[head_cols(6), head_cols(7), head_cols(8),
                  pl.BlockSpec((None, 1, table.shape[-1]), lambda b, h: (h, 0, 0))],
        out_specs=pl.BlockSpec((seq, HEAD_DIM), lambda b, h: (b, h)),
        out_shape=jax.ShapeDtypeStruct((n, width), BF16),
        scratch_shapes=[pltpu.VMEM((n_back + 1, t, t), F32), pltpu.VMEM((2, t, t), F32),
                        pltpu.VMEM((2, t, LANES), F32), stat, stat, pltpu.VMEM((t, HEAD_DIM), F32)],
        compiler_params=_params("parallel", "parallel"),
        name="chunk_band_attention",
    )(qkv, qkv, qkv, table)


def _diff_kernel(q1_ref, q2_ref, k1_ref, k2_ref, v_ref, lam_ref, g_ref, o_ref, s_buf, mx_buf, mask_sc,
                 m1_sc, l1_sc, a1_sc, m2_sc, l2_sc, a2_sc, *, t, nq, lam_init):
    streams = ((q1_ref, k1_ref, m1_sc, l1_sc, a1_sc), (q2_ref, k2_ref, m2_sc, l2_sc, a2_sc))
    _softmax_init(m1_sc, l1_sc, a1_sc)
    _softmax_init(m2_sc, l2_sc, a2_sc)
    _diag_mask_init(mask_sc, lambda row, col: col // CHUNK <= row // CHUNK)

    def score(pair, slot):
        qi, j = pair
        for i, (q_ref, k_ref, _, _, _) in enumerate(streams):
            s = _dot_nt(q_ref[_tile_rows(qi, t), :], k_ref[_tile_rows(j, t), :]) + mask_sc[_is_diag(qi, j)]
            s_buf[slot, i] = s
            mx_buf[slot, i] = _row_max(s)

    def update(pair, slot):
        v = v_ref[_tile_rows(pair[1], t), :]
        for i, (_, _, m_sc, l_sc, a_sc) in enumerate(streams):
            _softmax_flat_update(pair, s_buf[slot, i], mx_buf[slot, i], v, m_sc, l_sc, a_sc)

    def finish(pair):
        qi, j = pair

        @pl.when(j == qi)
        def _():
            lv = lam_ref[...]
            lam = (jnp.exp(jnp.sum(lv[0:1] * lv[1:2], axis=1, keepdims=True))
                   - jnp.exp(jnp.sum(lv[2:3] * lv[3:4], axis=1, keepdims=True)) + lam_init)
            o = _softmax_result(l1_sc, a1_sc) - lam * _softmax_result(l2_sc, a2_sc)
            y = o * lax.rsqrt(jnp.mean(o * o, axis=-1, keepdims=True) + RMS_EPS) * g_ref[...]
            o_ref[_tile_rows(qi, t), :] = (y * (1.0 - lam_init)).astype(o_ref.dtype)

    _flat_pipeline(nq * (nq + 1) // 2, _causal_advance, score, update, finish)


def _diff_attention(qkv, lam_vecs, subln, lam_init, batch, seq, width):
    n = qkv.shape[0]
    heads = width // (2 * HEAD_DIM)
    t = _tile(512, seq)
    nq = seq // t
    sec = width // HEAD_DIM
    half = lambda s, c: pl.BlockSpec((seq, HEAD_DIM), lambda b, h: (b, s * sec + 2 * h + c))
    scratch = [pltpu.VMEM((t, LANES), F32), pltpu.VMEM((t, LANES), F32), pltpu.VMEM((t, 2 * HEAD_DIM), F32)]
    stage = [pltpu.VMEM((2, 2, t, t), F32), pltpu.VMEM((2, 2, t, LANES), F32), pltpu.VMEM((2, t, t), F32)]
    return pl.pallas_call(
        functools.partial(_diff_kernel, t=t, nq=nq, lam_init=lam_init),
        grid=(batch, heads),
        in_specs=[half(9, 0), half(9, 1), half(10, 0), half(10, 1),
                  pl.BlockSpec((seq, 2 * HEAD_DIM), lambda b, h: (b, 11 * sec // 2 + h)),
                  pl.BlockSpec(lam_vecs.shape, lambda b, h: (0, 0)),
                  pl.BlockSpec((1, 2 * HEAD_DIM), lambda b, h: (0, 0))],
        out_specs=pl.BlockSpec((seq, 2 * HEAD_DIM), lambda b, h: (b, h)),
        out_shape=jax.ShapeDtypeStruct((n, width), BF16),
        scratch_shapes=stage + scratch + scratch,
        compiler_params=_params("parallel", "parallel"),
        name="differential_attention",
    )(qkv, qkv, qkv, qkv, qkv, lam_vecs, subln.reshape(1, -1))


def _merge_kernel(*refs):
    br, wb, gates, o_ref = refs[0:4], refs[4], refs[5:9], refs[9]
    acc = None
    for i in range(N_BRANCHES):
        term = gates[i][...].astype(F32) * jnp.dot(br[i][...], wb[i], preferred_element_type=F32)
        acc = term if acc is None else acc + term
    o_ref[...] = acc.astype(o_ref.dtype)


def _gated_merge(branches, w_branch, gates):
    n, width = branches[0].shape
    d = w_branch.shape[-1]
    tm, tn = _tile(1024, n), _tile(512, d)
    col_tiles = d // tn
    br_spec = pl.BlockSpec((tm, width), lambda i, j: (i, 0))
    gate_specs = [pl.BlockSpec((tm, tn), lambda i, j, g=g: (i, g * col_tiles + j)) for g in range(N_BRANCHES)]
    return pl.pallas_call(
        _merge_kernel,
        grid=(n // tm, col_tiles),
        in_specs=[br_spec] * N_BRANCHES + [pl.BlockSpec((N_BRANCHES, width, tn), lambda i, j: (0, 0, j))] + gate_specs,
        out_specs=pl.BlockSpec((tm, tn), lambda i, j: (i, j)),
        out_shape=jax.ShapeDtypeStruct((n, d), BF16),
        compiler_params=_params("parallel", "arbitrary"),
        name="gated_merge",
    )(*branches, w_branch, *([gates] * N_BRANCHES))


def _pack_halves(y):
    bits = lax.bitcast_convert_type(y.astype(BF16).astype(F32), jnp.uint32)
    half = y.shape[1] // 2
    return (bits[:, :half] >> 16) | bits[:, half:]


def _unpack_halves(p):
    lo = lax.bitcast_convert_type(p << 16, F32)
    hi = lax.bitcast_convert_type(p & jnp.uint32(0xFFFF0000), F32)
    return lo, hi


def _router_kernel(x_ref, g_ref, rw_ref, rb_ref, hp_ref, idx_ref, wt_ref, *, n_experts):
    x = x_ref[...]
    hn = x * lax.rsqrt(jnp.mean(x * x, axis=-1, keepdims=True) + RMS_EPS) * g_ref[...]
    hp_ref[...] = _pack_halves(hn)
    logits = None
    for wi, wp in enumerate(_split3(rw_ref[...])):
        for hi, hp in enumerate(_split3(hn)):
            if wi + hi <= 2:
                term = _dot_nt(wp, hp)
                logits = term if logits is None else logits + term
    score = jax.nn.sigmoid(logits)
    biased = score + rb_ref[...]
    per = n_experts // N_GROUPS
    rows = [biased[e:e + 1, :] for e in range(n_experts)]
    group_score = []
    for g in range(N_GROUPS):
        r = rows[g * per:(g + 1) * per]
        top1 = functools.reduce(jnp.maximum, r)
        first = []
        taken = jnp.zeros_like(top1, dtype=jnp.bool_)
        for v in r:
            is1 = (v == top1) & ~taken
            taken = taken | is1
            first.append(is1)
        rest = [jnp.where(f, NEG_INF, v) for f, v in zip(first, r)]
        top2 = functools.reduce(jnp.maximum, rest)
        second = []
        taken = jnp.zeros_like(top1, dtype=jnp.bool_)
        for v in rest:
            is2 = (v == top2) & ~taken
            taken = taken | is2
            second.append(is2)
        group_score.append((top1 + top2, [a | b for a, b in zip(first, second)]))
    best = functools.reduce(jnp.maximum, [gs for gs, _ in group_score])
    taken = jnp.zeros_like(best, dtype=jnp.bool_)
    picked = []
    for gs, sel in group_score:
        chosen = (gs == best) & ~taken
        taken = taken | chosen
        picked += [s & chosen for s in sel]
    w = [jnp.where(p, score[e:e + 1, :], 0.0) for e, p in enumerate(picked)]
    total = functools.reduce(jnp.add, w)
    inv = 1.0 / total
    lo = functools.reduce(jnp.minimum, [jnp.where(p, e, n_experts) for e, p in enumerate(picked)])
    hi = functools.reduce(jnp.maximum, [jnp.where(p, e, -1) for e, p in enumerate(picked)])
    idx_ref[0:1, :] = lo
    idx_ref[1:2, :] = hi
    wt_ref[0:1, :] = functools.reduce(jnp.add, [jnp.where(lo == e, we, 0.0) for e, we in enumerate(w)]) * inv
    wt_ref[1:2, :] = functools.reduce(jnp.add, [jnp.where(hi == e, we, 0.0) for e, we in enumerate(w)]) * inv


def _router(x, gain, router_w, router_bias):
    n, d = x.shape
    e = router_w.shape[1]
    tm = _tile(256, n)
    return pl.pallas_call(
        functools.partial(_router_kernel, n_experts=e),
        grid=(n // tm,),
        in_specs=[pl.BlockSpec((tm, d), lambda i: (i, 0)), pl.BlockSpec((1, d), lambda i: (0, 0)),
                  pl.BlockSpec((e, d), lambda i: (0, 0)), pl.BlockSpec((e, 1), lambda i: (0, 0))],
        out_specs=[pl.BlockSpec((tm, d // 2), lambda i: (i, 0)),
                   pl.BlockSpec((2, tm), lambda i: (0, i)), pl.BlockSpec((2, tm), lambda i: (0, i))],
        out_shape=[jax.ShapeDtypeStruct((n, d // 2), jnp.uint32),
                   jax.ShapeDtypeStruct((2, n), jnp.int32), jax.ShapeDtypeStruct((2, n), F32)],
        compiler_params=_params("parallel"),
        name="ffn_router",
    )(x, gain.reshape(1, d), router_w.T, router_bias.reshape(e, 1))


MOE_TILE = 256


def _moe_plan(idx, n_experts, tm):
    n = idx.shape[1]
    pairs = 2 * n
    n_rows = pairs + n_experts * tm
    expert = idx.reshape(pairs)
    onehot = (expert[:, None] == jnp.arange(n_experts)[None, :]).astype(jnp.int32)
    rank = jnp.sum((jnp.cumsum(onehot, axis=0) - onehot) * onehot, axis=1)
    group = -(-jnp.sum(onehot, axis=0) // tm) * tm
    group_end = jnp.cumsum(group)
    pos = (group_end - group)[expert] + rank
    pair_of_row = jnp.full((n_rows,), -1, jnp.int32).at[pos].set(jnp.arange(pairs, dtype=jnp.int32))
    is_pad = pair_of_row < 0
    spare = pairs + jnp.cumsum(is_pad.astype(jnp.int32)) - 1
    dst = jnp.where(is_pad, spare, pair_of_row)
    src = jnp.where(is_pad, 0, pair_of_row % n)
    tile_expert = jnp.minimum(
        jnp.searchsorted(group_end, jnp.arange(n_rows // tm, dtype=jnp.int32) * tm, side="right"), n_experts - 1)
    return src.reshape(-1, tm), dst.reshape(-1, tm), tile_expert.astype(jnp.int32)


def _expert_kernel(te_ref, src_ref, src_next_ref, dst_ref, x_hbm, wgu_ref, wd_ref, y_hbm,
                   xbuf, ybuf, gsem, ssem, *, tm, d_expert):
    t, n_tiles = pl.program_id(0), pl.num_programs(0)
    slot = t % 2

    def row_copy_in(ids_ref, s, r):
        return pltpu.make_async_copy(x_hbm.at[pl.ds(ids_ref[0, r], 1), :], xbuf.at[s, pl.ds(r, 1), :], gsem.at[s])

    def row_copy_out(s, r):
        return pltpu.make_async_copy(ybuf.at[s, pl.ds(r, 1), :], y_hbm.at[pl.ds(dst_ref[0, r], 1), :], ssem.at[s])

    def each_row(fn):
        lax.fori_loop(0, tm, lambda r, c: (fn(r), c)[1], 0, unroll=8)

    def gather(ids_ref, s):
        each_row(lambda r: row_copy_in(ids_ref, s, r).start())

    def gather_wait(s):
        each_row(lambda r: row_copy_in(src_ref, s, r).wait())

    def scatter_start(s):
        each_row(lambda r: row_copy_out(s, r).start(priority=1))

    def scatter_wait(s):
        each_row(lambda r: row_copy_out(s, r).wait())

    @pl.when(t == 0)
    def _():
        gather(src_ref, 0)

    @pl.when(t + 1 < n_tiles)
    def _():
        gather(src_next_ref, 1 - slot)

    gather_wait(slot)
    xn = jnp.concatenate(_unpack_halves(xbuf[slot]), axis=1).astype(BF16)
    h = jnp.dot(xn, wgu_ref[...], preferred_element_type=F32)
    hg, hu = h[:, :d_expert], h[:, d_expert:]
    act = (hg * jax.nn.sigmoid(hg) * hu).astype(BF16)

    @pl.when(t >= 2)
    def _():
        scatter_wait(slot)

    ybuf[slot] = _pack_halves(jnp.dot(act, wd_ref[...], preferred_element_type=F32))
    scatter_start(slot)

    @pl.when(t == n_tiles - 1)
    def _():
        scatter_wait(slot)

        @pl.when(t >= 1)
        def _():
            scatter_wait(1 - slot)


def _moe_experts(hp, idx, wgu, wd):
    n, half = hp.shape
    d = 2 * half
    n_experts, _, two_f = wgu.shape
    tm = MOE_TILE
    src, dst, tile_expert = _moe_plan(idx, n_experts, tm)
    n_tiles = src.shape[0]
    src, dst = src.reshape(n_tiles, 1, tm), dst.reshape(n_tiles, 1, tm)
    smem = lambda fn: pl.BlockSpec((None, 1, tm), fn, memory_space=pltpu.SMEM)
    grid_spec = pltpu.PrefetchScalarGridSpec(
        num_scalar_prefetch=1,
        grid=(n_tiles,),
        in_specs=[smem(lambda t, te: (t, 0, 0)),
                  smem(lambda t, te: (jnp.minimum(t + 1, n_tiles - 1), 0, 0)),
                  smem(lambda t, te: (t, 0, 0)),
                  pl.BlockSpec(memory_space=pl.ANY),
                  pl.BlockSpec((None, d, two_f), lambda t, te: (te[t], 0, 0)),
                  pl.BlockSpec((None, two_f // 2, d), lambda t, te: (te[t], 0, 0))],
        out_specs=pl.BlockSpec(memory_space=pl.ANY),
        scratch_shapes=[pltpu.VMEM((2, tm, half), jnp.uint32), pltpu.VMEM((2, tm, half), jnp.uint32),
                        pltpu.SemaphoreType.DMA((2,)), pltpu.SemaphoreType.DMA((2,))])
    return pl.pallas_call(
        functools.partial(_expert_kernel, tm=tm, d_expert=two_f // 2),
        grid_spec=grid_spec,
        out_shape=jax.ShapeDtypeStruct((n_tiles * tm, half), jnp.uint32),
        compiler_params=_params("arbitrary"),
        name="moe_experts",
    )(tile_expert, src, src, dst, hp, wgu, wd)


def _combine_kernel(x_ref, ya_ref, yb_ref, w_ref, o_ref):
    w = w_ref[...]
    half = ya_ref.shape[1]
    for c, (a, b) in enumerate(zip(_unpack_halves(ya_ref[...]), _unpack_halves(yb_ref[...]))):
        cols = slice(c * half, (c + 1) * half)
        o_ref[:, cols] = x_ref[:, cols] + w[:, 0:1] * a + w[:, 1:2] * b


def _moe_combine(x, y, wts):
    n, d = x.shape
    tm = _tile(256, n)
    nt = n // tm
    return pl.pallas_call(
        _combine_kernel,
        grid=(nt,),
        in_specs=[pl.BlockSpec((tm, d), lambda i: (i, 0)), pl.BlockSpec((tm, d // 2), lambda i: (i, 0)),
                  pl.BlockSpec((tm, d // 2), lambda i: (nt + i, 0)), pl.BlockSpec((tm, 2), lambda i: (i, 0))],
        out_specs=pl.BlockSpec((tm, d), lambda i: (i, 0)),
        out_shape=jax.ShapeDtypeStruct((n, d), F32),
        compiler_params=_params("parallel"),
        name="moe_combine",
    )(x, y, y, wts)


def _rope_tables(seq):
    half = HEAD_DIM // 2
    inv_freq = jnp.power(ROPE_THETA, -jnp.arange(half, dtype=F32) / half)
    ang = jnp.arange(seq, dtype=F32)[:, None] * inv_freq[None, :]
    cos, sin = jnp.cos(ang), jnp.sin(ang)
    return jnp.concatenate([cos, cos], axis=-1), jnp.concatenate([-sin, sin], axis=-1)


def kernel(x, attn_norm, w_in, fox_forget_bias, fox_q_norm, fox_k_norm, ch_q_norm, ch_k_norm, ch_rel_bias,
           diff_q_norm, diff_k_norm, diff_lambda, diff_subln, w_branch, w_out, ffn_norm, router_w, router_bias,
           w_gate, w_up, w_down):
    batch, seq, d_model = x.shape
    depth = w_in.shape[0]
    width = d_model // N_BRANCHES
    fox_heads = width // HEAD_DIM
    qkv_cols = 3 * width
    n_experts, _, d_expert = w_gate.shape[1:]
    n = batch * seq
    scale = HEAD_DIM ** -0.5
    assert fox_heads <= LANES and width % (2 * HEAD_DIM) == 0 and seq % CHUNK == 0

    cos_t, sin_t = _rope_tables(seq)
    ones = jnp.ones((HEAD_DIM,), F32)
    res_spec = lambda tm, tn: pl.BlockSpec((tm, tn), lambda i, j: (i, j))
    xf = x.reshape(n, d_model)

    for layer in range(depth):
        w_proj, col_amax = _stage_in_proj(w_in, layer, qkv_cols, fox_heads)
        w_qkv = w_proj
        w_f = _cast_lane_group(w_in, layer, qkv_cols, fox_heads)
        gate_col0 = 4 * qkv_cols
        w_g8 = _quantize_cols(w_proj, col_amax, gate_col0, N_BRANCHES * d_model)
        g_scale = col_amax[:, gate_col0:] * (1.0 / FP8_MAX)
        f_bias = jnp.pad(fox_forget_bias[layer], (0, LANES - fox_heads)).reshape(1, LANES)
        gains = jnp.stack([
            fox_q_norm[layer] * scale, fox_k_norm[layer], ones,
            ones * scale, ones, ones,
            ch_q_norm[layer] * scale, ch_k_norm[layer], ones,
            diff_q_norm[layer] * scale, diff_k_norm[layer], ones,
            ones, ones, ones, ones])

        h, h8, h_scale = _rmsnorm(xf, attn_norm[layer])
        qkv = _qkv_proj(h, w_qkv, gains, cos_t, sin_t, seq, width)
        gates = _matmul(h8, w_g8, _epi_scaled_sigmoid, BF16, extras=(h_scale, g_scale),
                        extra_specs=(lambda tm, tn: pl.BlockSpec((tm, LANES), lambda i, j: (i, 0)),
                                     lambda tm, tn: pl.BlockSpec((1, tn), lambda i, j: (0, j))),
                        name="gate_proj")
        cum = _forget_cumsum(h, w_f, f_bias, batch, seq)
        cum_t = cum[:, :fox_heads].T

        lam_init = 0.8 - 0.6 * math.exp(-0.3 * layer)
        branches = (
            _fox_attention(qkv, cum, cum_t, batch, seq, width),
            _sb_attention(qkv, batch, seq, width),
            _band_attention(qkv, ch_rel_bias[layer], batch, seq, width),
            _diff_attention(qkv, diff_lambda[layer], diff_subln[layer], lam_init, batch, seq, width),
        )
        merged = _gated_merge(branches, _cast_layer(w_branch, layer).reshape(w_branch.shape[1:]), gates)
        xf = _matmul(merged, _cast_layer(w_out, layer), _epi_residual, F32, extras=(xf,), extra_specs=(res_spec,),
                     tn=512, name="out_proj")

        hp, idx, wts = _router(xf, ffn_norm[layer], router_w, router_bias)
        wgu = _cast_gate_up(w_gate, w_up, layer)
        y = _moe_experts(hp, idx, wgu, _cast_layer(w_down, layer).reshape(w_down.shape[1:]))
        xf = _moe_combine(xf, y, wts.T)
    return xf.reshape(batch, seq, d_model)
```

```python
import functools
import math

import jax
import jax.numpy as jnp
from jax import lax
from jax.experimental import pallas as pl
from jax.experimental.pallas import tpu as pltpu

HEAD_DIM = 128
LANES = 128
SUBLANES = 8
CHUNK = 64
N_BRANCHES = 4
CH_LEFT_CHUNKS = 8
REL_CLIP = 128
ROPE_THETA = 10000.0
N_GROUPS = 4
RMS_EPS = 1e-6
NEG_INF = float("-inf")
VMEM_LIMIT_BYTES = 56 * 1024 * 1024

F32 = jnp.float32
BF16 = jnp.bfloat16


def _tile(pref, n):
    t = min(pref, n)
    while n % t:
        t //= 2
    return t


def _params(*sem):
    return pltpu.CompilerParams(dimension_semantics=sem, vmem_limit_bytes=VMEM_LIMIT_BYTES)


def _dot_nt(a, b):
    return lax.dot_general(a, b, (((1,), (1,)), ((), ())), preferred_element_type=F32)


def _log_sigmoid(x):
    return jnp.minimum(x, 0.0) - jnp.log1p(jnp.exp(-jnp.abs(x)))


FP8 = jnp.float8_e4m3fn
FP8_MAX = 448.0


def _rmsnorm_kernel(x_ref, g_ref, o_ref, o8_ref, s_ref):
    x = x_ref[...]
    ms = jnp.mean(x * x, axis=-1, keepdims=True)
    y = x * lax.rsqrt(ms + RMS_EPS) * g_ref[...]
    o_ref[...] = y.astype(o_ref.dtype)
    amax = jnp.max(jnp.abs(y), axis=-1, keepdims=True)
    o8_ref[...] = (y * jnp.where(amax > 0.0, FP8_MAX / amax, 0.0)).astype(o8_ref.dtype)
    s_ref[...] = jnp.broadcast_to(amax * (1.0 / FP8_MAX), s_ref.shape)


def _rmsnorm(x, gain):
    n, d = x.shape
    tm = _tile(256, n)
    row = pl.BlockSpec((tm, d), lambda i: (i, 0))
    return pl.pallas_call(
        _rmsnorm_kernel,
        grid=(n // tm,),
        in_specs=[row, pl.BlockSpec((1, d), lambda i: (0, 0))],
        out_specs=[row, row, pl.BlockSpec((tm, LANES), lambda i: (i, 0))],
        out_shape=[jax.ShapeDtypeStruct((n, d), BF16), jax.ShapeDtypeStruct((n, d), FP8),
                   jax.ShapeDtypeStruct((n, LANES), F32)],
        compiler_params=_params("parallel"),
        name="attn_rmsnorm",
    )(x, gain.reshape(1, d))


def _cast_kernel(w_ref, o_ref):
    o_ref[...] = w_ref[...].astype(o_ref.dtype)


def _cast_layer(w, layer):
    cols = w.shape[-1]
    rows = math.prod(w.shape[1:-1])
    w2 = w.reshape(w.shape[0] * rows, cols)
    tr, tc = _tile(1024, rows), _tile(2048, cols)
    return pl.pallas_call(
        _cast_kernel,
        grid=(rows // tr, cols // tc),
        in_specs=[pl.BlockSpec((tr, tc), lambda i, j: (layer * (rows // tr) + i, j))],
        out_specs=pl.BlockSpec((tr, tc), lambda i, j: (i, j)),
        out_shape=jax.ShapeDtypeStruct((rows, cols), BF16),
        compiler_params=_params("parallel", "parallel"),
        name="cast_weights",
    )(w2)


def _pair_cast_kernel(a_ref, b_ref, o_ref):
    f = a_ref.shape[-1]
    o_ref[:, :f] = a_ref[...].astype(o_ref.dtype)
    o_ref[:, f:] = b_ref[...].astype(o_ref.dtype)


def _cast_gate_up(w_gate, w_up, layer):
    _, e, d, f = w_gate.shape
    tr = _tile(1024, d)
    spec = pl.BlockSpec((None, None, tr, f), lambda x, i: (layer, x, i, 0))
    return pl.pallas_call(
        _pair_cast_kernel,
        grid=(e, d // tr),
        in_specs=[spec, spec],
        out_specs=pl.BlockSpec((None, tr, 2 * f), lambda x, i: (x, i, 0)),
        out_shape=jax.ShapeDtypeStruct((e, d, 2 * f), BF16),
        compiler_params=_params("parallel", "parallel"),
        name="cast_gate_up",
    )(w_gate, w_up)


def _in_proj_kernel(a_ref, b_ref, o_ref, amax_ref, *, first_shifted, shift):
    j, i = pl.program_id(0), pl.program_id(1)

    def emit(w):
        wb = w.astype(o_ref.dtype)
        o_ref[...] = wb
        col_max = jnp.max(jnp.abs(wb.astype(F32)), axis=0, keepdims=True)
        amax_ref[...] = jnp.where(i == 0, col_max, jnp.maximum(amax_ref[...], col_max))

    @pl.when(j < first_shifted)
    def _():
        emit(a_ref[...])

    @pl.when(j >= first_shifted)
    def _():
        tc = a_ref.shape[1]
        emit(jnp.concatenate([a_ref[...], b_ref[...]], axis=1)[:, shift:shift + tc])


def _stage_in_proj(w_in, layer, keep_cols, drop_cols):
    _, d, cols = w_in.shape
    out_cols = cols - drop_cols
    tc = _tile(1024, math.gcd(keep_cols, out_cols))
    tr = _tile(2048, d)
    assert tc % LANES == 0 and drop_cols < LANES
    lane_blocks = tc // LANES
    return pl.pallas_call(
        functools.partial(_in_proj_kernel, first_shifted=keep_cols // tc, shift=drop_cols),
        grid=(out_cols // tc, d // tr),
        in_specs=[pl.BlockSpec((None, tr, tc), lambda j, i: (layer, i, j)),
                  pl.BlockSpec((tr, LANES), lambda j, i: (layer * (d // tr) + i, (j + 1) * lane_blocks))],
        out_specs=[pl.BlockSpec((tr, tc), lambda j, i: (i, j)), pl.BlockSpec((1, tc), lambda j, i: (0, j))],
        out_shape=[jax.ShapeDtypeStruct((d, out_cols), BF16), jax.ShapeDtypeStruct((1, out_cols), F32)],
        compiler_params=_params("parallel", "arbitrary"),
        name="stage_in_proj",
    )(w_in, w_in.reshape(-1, cols))


def _lane_group_kernel(w_ref, o_ref, *, ncols):
    lane = lax.broadcasted_iota(jnp.int32, w_ref.shape, 1)
    o_ref[...] = jnp.where(lane < ncols, w_ref[...], 0.0).astype(o_ref.dtype)


def _cast_lane_group(w, layer, col0, ncols):
    _, d, _ = w.shape
    assert col0 % LANES == 0 and ncols <= LANES
    tr = _tile(2048, d)
    return pl.pallas_call(
        functools.partial(_lane_group_kernel, ncols=ncols),
        grid=(d // tr,),
        in_specs=[pl.BlockSpec((None, tr, LANES), lambda i: (layer, i, col0 // LANES))],
        out_specs=pl.BlockSpec((tr, LANES), lambda i: (i, 0)),
        out_shape=jax.ShapeDtypeStruct((d, LANES), BF16),
        compiler_params=_params("parallel"),
        name="cast_forget_weights",
    )(w)


def _quantize_kernel(w_ref, amax_ref, o_ref):
    amax = amax_ref[...]
    o_ref[...] = (w_ref[...].astype(F32) * jnp.where(amax > 0.0, FP8_MAX / amax, 0.0)).astype(o_ref.dtype)


def _quantize_cols(w, amax, col0, ncols):
    d = w.shape[0]
    tr, tc = _tile(2048, d), _tile(1024, math.gcd(col0, ncols))
    assert tc % LANES == 0
    off = col0 // tc
    return pl.pallas_call(
        _quantize_kernel,
        grid=(d // tr, ncols // tc),
        in_specs=[pl.BlockSpec((tr, tc), lambda i, j: (i, off + j)), pl.BlockSpec((1, tc), lambda i, j: (0, off + j))],
        out_specs=pl.BlockSpec((tr, tc), lambda i, j: (i, j)),
        out_shape=jax.ShapeDtypeStruct((d, ncols), FP8),
        compiler_params=_params("parallel", "parallel"),
        name="quantize_gate_weights",
    )(w, amax)


def _matmul(a, w, epilogue, out_dtype, extras=(), extra_specs=(), tm=1024, tn=1024, n_cols=None, name="matmul"):
    m, k = a.shape
    n = w.shape[1] if n_cols is None else n_cols
    tm, tn = _tile(tm, m), _tile(tn, n)

    def kern(a_ref, w_ref, *refs):
        acc = jnp.dot(a_ref[...], w_ref[...], preferred_element_type=F32)
        epilogue(acc, refs[:-1], refs[-1])

    return pl.pallas_call(
        kern,
        grid=(m // tm, n // tn),
        in_specs=[pl.BlockSpec((tm, k), lambda i, j: (i, 0)), pl.BlockSpec((k, tn), lambda i, j: (0, j)),
                  *[mk(tm, tn) for mk in extra_specs]],
        out_specs=pl.BlockSpec((tm, tn), lambda i, j: (i, j)),
        out_shape=jax.ShapeDtypeStruct((m, n), out_dtype),
        compiler_params=_params("parallel", "arbitrary"),
        name=name,
    )(a, w, *extras)


def _epi_scaled_sigmoid(acc, refs, o_ref):
    row_scale, col_scale = refs
    o_ref[...] = jax.nn.sigmoid(acc * row_scale[:, :1] * col_scale[...]).astype(o_ref.dtype)


def _epi_residual(acc, refs, o_ref):
    o_ref[...] = refs[0][...] + acc


PLAIN, NORM, NORM_ROPE = 0, 1, 2
SECTION_KINDS = (NORM, NORM, PLAIN, PLAIN, PLAIN, PLAIN, NORM, NORM, PLAIN, NORM_ROPE, NORM_ROPE, PLAIN)


def _qkv_epilogue(acc, refs, o_ref, *, sec_tiles):
    gain_ref, cos_ref, sin_ref = refs
    sec = pl.program_id(1) // sec_tiles
    g = gain_ref[pl.ds(sec, 1), :]
    tn = acc.shape[1]

    def run(kind):
        for hh in range(tn // HEAD_DIM):
            y = acc[:, hh * HEAD_DIM:(hh + 1) * HEAD_DIM]
            if kind != PLAIN:
                y = y * lax.rsqrt(jnp.mean(y * y, axis=-1, keepdims=True) + RMS_EPS)
            y = y * g
            if kind == NORM_ROPE:
                y = y * cos_ref[...] + pltpu.roll(y, HEAD_DIM // 2, 1) * sin_ref[...]
            o_ref[:, hh * HEAD_DIM:(hh + 1) * HEAD_DIM] = y.astype(o_ref.dtype)

    for kind in (PLAIN, NORM, NORM_ROPE):
        hit = functools.reduce(jnp.logical_or, [sec == s for s, kd in enumerate(SECTION_KINDS) if kd == kind])
        pl.when(hit)(functools.partial(run, kind))


def _qkv_proj(h, w_qkv, gains, cos_t, sin_t, seq, width):
    n = h.shape[0]
    tm = _tile(1024, seq)
    tn = _tile(1024, width)
    seq_tiles = seq // tm
    row_spec = lambda tm_, tn_: pl.BlockSpec((tm_, LANES), lambda i, j: (i % seq_tiles, 0))
    gain_spec = lambda tm_, tn_: pl.BlockSpec(gains.shape, lambda i, j: (0, 0))
    return _matmul(h, w_qkv, functools.partial(_qkv_epilogue, sec_tiles=width // tn), BF16,
                   extras=(gains, cos_t, sin_t), extra_specs=(gain_spec, row_spec, row_spec),
                   tm=tm, tn=tn, n_cols=len(SECTION_KINDS) * width, name="qkv_proj")


def _split3(x):
    hi = x.astype(BF16)
    r = x - hi.astype(F32)
    mid = r.astype(BF16)
    lo = (r - mid.astype(F32)).astype(BF16)
    return hi, mid, lo


def _fgate_kernel(h_ref, w_ref, b_ref, tri_ref, o_ref, carry_sc):
    @pl.when(pl.program_id(1) == 0)
    def _():
        carry_sc[...] = jnp.zeros_like(carry_sc)

    f = jnp.dot(h_ref[...], w_ref[...], preferred_element_type=F32) + b_ref[...]
    lf = _log_sigmoid(f)
    tri = tri_ref[...]
    c = carry_sc[...]
    for part in _split3(lf):
        c = c + jnp.dot(tri, part, preferred_element_type=F32)
    o_ref[...] = c
    carry_sc[...] = c[-1:, :]


def _forget_cumsum(h, w_f, bias, batch, seq):
    n, d = h.shape
    ts = _tile(512, seq)
    nt = seq // ts
    tri = (jnp.arange(ts)[:, None] >= jnp.arange(ts)[None, :]).astype(BF16)
    return pl.pallas_call(
        _fgate_kernel,
        grid=(batch, nt),
        in_specs=[pl.BlockSpec((ts, d), lambda b, t: (b * nt + t, 0)),
                  pl.BlockSpec((d, LANES), lambda b, t: (0, 0)),
                  pl.BlockSpec((1, LANES), lambda b, t: (0, 0)),
                  pl.BlockSpec((ts, ts), lambda b, t: (0, 0))],
        out_specs=pl.BlockSpec((ts, LANES), lambda b, t: (b * nt + t, 0)),
        out_shape=jax.ShapeDtypeStruct((n, LANES), F32),
        scratch_shapes=[pltpu.VMEM((1, LANES), F32)],
        compiler_params=_params("parallel", "arbitrary"),
        name="forget_cumsum",
    )(h, w_f, bias, tri)


def _softmax_step(s, v, m_sc, l_sc, acc_sc):
    m_prev = m_sc[...]
    m_new = jnp.maximum(m_prev, jnp.max(s, axis=1, keepdims=True))
    alpha = jnp.exp(m_prev - m_new)
    p = jnp.exp(s - m_new)
    l_sc[...] = alpha * l_sc[...] + jnp.sum(p, axis=1, keepdims=True)
    acc_sc[...] = alpha * acc_sc[...] + jnp.dot(p.astype(v.dtype), v, preferred_element_type=F32)
    m_sc[...] = m_new


def _softmax_init(m_sc, l_sc, acc_sc):
    m_sc[...] = jnp.full_like(m_sc, NEG_INF)
    l_sc[...] = jnp.zeros_like(l_sc)
    acc_sc[...] = jnp.zeros_like(acc_sc)


def _attn_specs(seq, tq, width, q_sec, k_sec, v_sec, v_mult=1, q_mult=1):
    nq = seq // tq
    sec = width // HEAD_DIM
    q_spec = lambda c=0: pl.BlockSpec((tq, HEAD_DIM), lambda b, h, i: (b * nq + i, q_sec * sec + h * q_mult + c))
    k_spec = lambda c=0: pl.BlockSpec((seq, HEAD_DIM), lambda b, h, i: (b, k_sec * sec + h * q_mult + c))
    v_spec = pl.BlockSpec((seq, HEAD_DIM * v_mult), lambda b, h, i: (b, v_sec * sec // v_mult + h))
    return q_spec, k_spec, v_spec


def _rep(x, n):
    return x if n == LANES else jnp.concatenate([x] * (n // LANES), axis=1)


def _lane_fold(x, op):
    out = x[:, :LANES]
    for c in range(1, x.shape[1] // LANES):
        out = op(out, x[:, c * LANES:(c + 1) * LANES])
    return out


def _row_max(s):
    return jnp.broadcast_to(jnp.max(_lane_fold(s, jnp.maximum), axis=1, keepdims=True), (s.shape[0], LANES))


def _softmax_update(s, mx, v, m_sc, l_sc, acc_sc):
    m_prev = m_sc[...]
    m_new = jnp.maximum(m_prev, mx)
    alpha = jnp.exp(m_prev - m_new)
    p = jnp.exp(s - _rep(m_new, s.shape[1]))
    l_sc[...] = alpha * l_sc[...] + _lane_fold(p, jnp.add)
    acc_sc[...] = (_rep(alpha, acc_sc.shape[1]) * acc_sc[...]
                   + jnp.dot(p.astype(v.dtype), v, preferred_element_type=F32))
    m_sc[...] = m_new


def _softmax_result(l_sc, acc_sc):
    return acc_sc[...] * (1.0 / jnp.sum(l_sc[...], axis=1, keepdims=True))


def _pipelined_causal(qi, score, update):
    @pl.when(qi == 0)
    def _():
        score(0, True)

    @pl.when(qi > 0)
    def _():
        score(0, False)

    def body(j, carry):
        update(j)
        score(j + 1, False)
        return carry

    lax.fori_loop(0, qi - 1, body, 0)

    @pl.when(qi > 0)
    def _():
        update(qi - 1)
        score(qi, True)

    update(qi)


def _tile_rows(i, t):
    return pl.ds(i * t, t) if isinstance(i, int) else pl.ds(pl.multiple_of(i * t, t), t)


def _flat_pipeline(n_pairs, advance, score, update, finish):
    score((0, 0), 0)

    def body(p, pair):
        nxt = advance(pair)
        update(pair, p % 2)
        score(nxt, (p + 1) % 2)
        finish(pair)
        return nxt

    last = lax.fori_loop(0, n_pairs - 1, body, (jnp.int32(0), jnp.int32(0)))
    update(last, (n_pairs - 1) % 2)
    finish(last)


def _causal_advance(pair):
    qi, j = pair
    more = j < qi
    return jnp.where(more, qi, qi + 1), jnp.where(more, j + 1, 0)


def _softmax_flat_update(pair, s, mx, v, m_sc, l_sc, acc_sc):
    m_prev = jnp.where(pair[1] == 0, NEG_INF, m_sc[...])
    m_new = jnp.maximum(m_prev, mx)
    alpha = jnp.exp(m_prev - m_new)
    p = jnp.exp(s - _rep(m_new, s.shape[1]))
    l_sc[...] = alpha * l_sc[...] + _lane_fold(p, jnp.add)
    acc_sc[...] = (_rep(alpha, acc_sc.shape[1]) * acc_sc[...]
                   + jnp.dot(p.astype(v.dtype), v, preferred_element_type=F32))
    m_sc[...] = m_new


def _diag_mask_init(mask_sc, visible):
    t = mask_sc.shape[-1]
    row = lax.broadcasted_iota(jnp.int32, (t, t), 0)
    col = lax.broadcasted_iota(jnp.int32, (t, t), 1)
    mask_sc[0] = jnp.zeros((t, t), F32)
    mask_sc[1] = jnp.where(visible(row, col), 0.0, NEG_INF)


def _is_diag(qi, j):
    return int(j == qi) if isinstance(j, int) else jnp.where(j < qi, 0, 1)


def _fox_kernel(q_ref, k_ref, v_ref, ct_ref, cs_ref, o_ref, s_buf, mx_buf, mask_sc, m_sc, l_sc, acc_sc, ct_sc,
                *, t, nq):
    h = pl.program_id(1)
    lane = lax.broadcasted_iota(jnp.int32, ct_ref.shape, 1)
    ct = jnp.sum(jnp.where(lane == h, ct_ref[...], 0.0), axis=1, keepdims=True)
    ct_sc[...] = jnp.broadcast_to(ct, ct_sc.shape)
    _softmax_init(m_sc, l_sc, acc_sc)
    _diag_mask_init(mask_sc, lambda row, col: col <= row)

    def score(pair, slot):
        qi, j = pair
        rows = _tile_rows(qi, t)
        s = _dot_nt(q_ref[rows, :], k_ref[_tile_rows(j, t), :])
        s = s + _rep(ct_sc[rows, :], t) - cs_ref[pl.ds(j, 1), :] + mask_sc[_is_diag(qi, j)]
        s_buf[slot] = s
        mx_buf[slot] = _row_max(s)

    def update(pair, slot):
        _softmax_flat_update(pair, s_buf[slot], mx_buf[slot], v_ref[_tile_rows(pair[1], t), :], m_sc, l_sc, acc_sc)

    def finish(pair):
        qi, j = pair

        @pl.when(j == qi)
        def _():
            o_ref[_tile_rows(qi, t), :] = _softmax_result(l_sc, acc_sc).astype(o_ref.dtype)

    _flat_pipeline(nq * (nq + 1) // 2, _causal_advance, score, update, finish)


def _fox_attention(qkv, cum, cum_t, batch, seq, width):
    n = qkv.shape[0]
    heads = width // HEAD_DIM
    t = _tile(512, seq)
    nq = seq // t
    head_cols = lambda s: pl.BlockSpec((seq, HEAD_DIM), lambda b, h: (b, s * heads + h))
    stat = pltpu.VMEM((t, LANES), F32)
    return pl.pallas_call(
        functools.partial(_fox_kernel, t=t, nq=nq),
        grid=(batch, heads),
        in_specs=[head_cols(0), head_cols(1), head_cols(2),
                  pl.BlockSpec((seq, LANES), lambda b, h: (b, 0)),
                  pl.BlockSpec((None, nq, t), lambda b, h: (h, b, 0))],
        out_specs=pl.BlockSpec((seq, HEAD_DIM), lambda b, h: (b, h)),
        out_shape=jax.ShapeDtypeStruct((n, width), BF16),
        scratch_shapes=[pltpu.VMEM((2, t, t), F32), pltpu.VMEM((2, t, LANES), F32), pltpu.VMEM((2, t, t), F32),
                        stat, stat, pltpu.VMEM((t, HEAD_DIM), F32), pltpu.VMEM((seq, LANES), F32)],
        compiler_params=_params("parallel", "parallel"),
        name="fox_attention",
    )(qkv, qkv, qkv, cum, cum_t.reshape(heads, n // t, t))


SB_DEAD_LOG = -110.0


def _sb_kernel(q_ref, k_ref, v_ref, u_ref, o_ref, e_buf, rs_buf, c_sc, acc_sc, *, t):
    qi = pl.program_id(2)
    c_sc[...] = jnp.zeros_like(c_sc)
    acc_sc[...] = jnp.zeros_like(acc_sc)

    def score(jj, diagonal):
        ks = pl.multiple_of((qi - jj) * t, t)
        z = _dot_nt(q_ref[...], k_ref[pl.ds(ks, t), :])
        softplus = jnp.log(1.0 + jnp.exp(-jnp.abs(z)))
        log_beta = jnp.minimum(z, 0.0) - softplus
        log_rest = jnp.minimum(-z, 0.0) - softplus
        if diagonal:
            row = lax.broadcasted_iota(jnp.int32, (t, t), 0)
            col = lax.broadcasted_iota(jnp.int32, (t, t), 1)
            strict = col < row
            log_rest = jnp.where(strict, log_rest, 0.0)
        hi = log_rest.astype(BF16)
        lo = (log_rest - hi.astype(F32)).astype(BF16)
        u = u_ref[...]
        cw = u.shape[0]
        later, tail = [], None
        for c in reversed(range(t // cw)):
            sl = slice(c * cw, (c + 1) * cw)
            inside = jnp.dot(hi[:, sl], u, preferred_element_type=F32) + jnp.dot(lo[:, sl], u, preferred_element_type=F32)
            later.insert(0, inside if tail is None else inside + _rep(tail, cw))
            chunk_sum = jnp.broadcast_to(
                jnp.sum(_lane_fold(log_rest[:, sl], jnp.add), axis=1, keepdims=True), (t, LANES))
            tail = chunk_sum if tail is None else tail + chunk_sum
        e = log_beta + jnp.concatenate(later, axis=1)
        if diagonal:
            e = jnp.where(strict, e, NEG_INF)
        e_buf[jj % 2] = e
        rs_buf[jj % 2] = tail

    def update(jj):
        ks = pl.multiple_of((qi - jj) * t, t)
        c = c_sc[...]
        w = jnp.exp(e_buf[jj % 2] + _rep(c, t))
        v = v_ref[pl.ds(ks, t), :]
        acc_sc[...] += jnp.dot(w.astype(v.dtype), v, preferred_element_type=F32)
        c_sc[...] = c + rs_buf[jj % 2]

    score(0, True)

    def body(state):
        jj, _ = state
        update(jj)
        score(jj + 1, False)
        return jj + 1, jnp.max(c_sc[...])

    jj, c_max = lax.while_loop(lambda st: (st[0] < qi) & (st[1] > SB_DEAD_LOG), body, (0, jnp.float32(0.0)))
    pl.when(c_max > SB_DEAD_LOG)(lambda: update(jj))
    o_ref[...] = acc_sc[...].astype(o_ref.dtype)


def _sb_attention(qkv, batch, seq, width):
    n = qkv.shape[0]
    heads = width // HEAD_DIM
    t = _tile(256, seq)
    nq = seq // t
    q_spec, k_spec, v_spec = _attn_specs(seq, t, width, 3, 4, 5)
    cw = min(256, t)
    upper = (jnp.arange(cw)[:, None] > jnp.arange(cw)[None, :]).astype(BF16)
    return pl.pallas_call(
        functools.partial(_sb_kernel, t=t),
        grid=(batch, heads, nq),
        in_specs=[q_spec(), k_spec(), v_spec, pl.BlockSpec((cw, cw), lambda b, h, i: (0, 0))],
        out_specs=pl.BlockSpec((t, HEAD_DIM), lambda b, h, i: (b * nq + i, h)),
        out_shape=jax.ShapeDtypeStruct((n, width), BF16),
        scratch_shapes=[pltpu.VMEM((2, t, t), F32), pltpu.VMEM((2, t, LANES), F32),
                        pltpu.VMEM((t, LANES), F32), pltpu.VMEM((t, HEAD_DIM), F32)],
        compiler_params=_params("parallel", "parallel", "arbitrary"),
        name="stick_breaking_attention",
    )(qkv, qkv, qkv, upper)


def _band_kernel(q_ref, k_ref, v_ref, tab_ref, o_ref, bias_sc, s_buf, mx_buf, m_sc, l_sc, acc_sc, *, t, nq, n_back):
    width = tab_ref.shape[-1]
    rolled = pltpu.roll(jnp.broadcast_to(tab_ref[...], (t, width)), 0, 1, stride=1, stride_axis=0)
    row = lax.broadcasted_iota(jnp.int32, (t, t), 0)
    col = lax.broadcasted_iota(jnp.int32, (t, t), 1)
    for d in range(n_back + 1):
        gap = (row + d * t) // CHUNK - col // CHUNK
        visible = (gap >= 0) & (gap <= CH_LEFT_CHUNKS)
        c0 = (n_back + 1 - d) * t
        bias_sc[d] = jnp.where(visible, rolled[:, c0:c0 + t], NEG_INF)
    _softmax_init(m_sc, l_sc, acc_sc)

    def advance(pair):
        qi, d = pair
        more = d < jnp.minimum(qi, n_back)
        return jnp.where(more, qi, qi + 1), jnp.where(more, d + 1, 0)

    def score(pair, slot):
        qi, d = pair
        s = _dot_nt(q_ref[_tile_rows(qi, t), :], k_ref[_tile_rows(qi - d, t), :]) + bias_sc[d]
        s_buf[slot] = s
        mx_buf[slot] = _row_max(s)

    def update(pair, slot):
        qi, d = pair
        _softmax_flat_update(pair, s_buf[slot], mx_buf[slot], v_ref[_tile_rows(qi - d, t), :], m_sc, l_sc, acc_sc)

    def finish(pair):
        qi, d = pair

        @pl.when(d == jnp.minimum(qi, n_back))
        def _():
            o_ref[_tile_rows(qi, t), :] = _softmax_result(l_sc, acc_sc).astype(o_ref.dtype)

    lead = min(nq, n_back)
    _flat_pipeline(nq * (n_back + 1) - lead * n_back + lead * (lead - 1) // 2, advance, score, update, finish)


def _band_table(rel_bias, t, n_back):
    rel = (n_back + 1) * t - jnp.arange((n_back + 2) * t)
    return rel_bias[:, None, jnp.clip(rel, -REL_CLIP, REL_CLIP) + REL_CLIP].astype(F32)


def _band_attention(qkv, rel_bias, batch, seq, width):
    n = qkv.shape[0]
    heads = width // HEAD_DIM
    t = _tile(512, seq)
    nq = seq // t
    n_back = -(-(CH_LEFT_CHUNKS * CHUNK) // t)
    table = _band_table(rel_bias, t, n_back)
    head_cols = lambda s: pl.BlockSpec((seq, HEAD_DIM), lambda b, h: (b, s * heads + h))
    stat = pltpu.VMEM((t, LANES), F32)
    return pl.pallas_call(
        functools.partial(_band_kernel, t=t, nq=nq, n_back=n_back),
        grid=(batch, heads),
        in_specs=[head_cols(6), head_cols(7), head_cols(8),
                  pl.BlockSpec((None, 1, table.shape[-1]), lambda b, h: (h, 0, 0))],
        out_specs=pl.BlockSpec((seq, HEAD_DIM), lambda b, h: (b, h)),
        out_shape=jax.ShapeDtypeStruct((n, width), BF16),
        scratch_shapes=[pltpu.VMEM((n_back + 1, t, t), F32), pltpu.VMEM((2, t, t), F32),
                        pltpu.VMEM((2, t, LANES), F32), stat, stat, pltpu.VMEM((t, HEAD_DIM), F32)],
        compiler_params=_params("parallel", "parallel"),
        name="chunk_band_attention",
    )(qkv, qkv, qkv, table)


def _diff_kernel(q1_ref, q2_ref, k1_ref, k2_ref, v_ref, lam_ref, g_ref, o_ref, s_buf, mx_buf, mask_sc,
                 m1_sc, l1_sc, a1_sc, m2_sc, l2_sc, a2_sc, *, t, nq, lam_init):
    streams = ((q1_ref, k1_ref, m1_sc, l1_sc, a1_sc), (q2_ref, k2_ref, m2_sc, l2_sc, a2_sc))
    _softmax_init(m1_sc, l1_sc, a1_sc)
    _softmax_init(m2_sc, l2_sc, a2_sc)
    _diag_mask_init(mask_sc, lambda row, col: col // CHUNK <= row // CHUNK)

    def score(pair, slot):
        qi, j = pair
        for i, (q_ref, k_ref, _, _, _) in enumerate(streams):
            s = _dot_nt(q_ref[_tile_rows(qi, t), :], k_ref[_tile_rows(j, t), :]) + mask_sc[_is_diag(qi, j)]
            s_buf[slot, i] = s
            mx_buf[slot, i] = _row_max(s)

    def update(pair, slot):
        v = v_ref[_tile_rows(pair[1], t), :]
        for i, (_, _, m_sc, l_sc, a_sc) in enumerate(streams):
            _softmax_flat_update(pair, s_buf[slot, i], mx_buf[slot, i], v, m_sc, l_sc, a_sc)

    def finish(pair):
        qi, j = pair

        @pl.when(j == qi)
        def _():
            lv = lam_ref[...]
            lam = (jnp.exp(jnp.sum(lv[0:1] * lv[1:2], axis=1, keepdims=True))
                   - jnp.exp(jnp.sum(lv[2:3] * lv[3:4], axis=1, keepdims=True)) + lam_init)
            o = _softmax_result(l1_sc, a1_sc) - lam * _softmax_result(l2_sc, a2_sc)
            y = o * lax.rsqrt(jnp.mean(o * o, axis=-1, keepdims=True) + RMS_EPS) * g_ref[...]
            o_ref[_tile_rows(qi, t), :] = (y * (1.0 - lam_init)).astype(o_ref.dtype)

    _flat_pipeline(nq * (nq + 1) // 2, _causal_advance, score, update, finish)


def _diff_attention(qkv, lam_vecs, subln, lam_init, batch, seq, width):
    n = qkv.shape[0]
    heads = width // (2 * HEAD_DIM)
    t = _tile(512, seq)
    nq = seq // t
    sec = width // HEAD_DIM
    half = lambda s, c: pl.BlockSpec((seq, HEAD_DIM), lambda b, h: (b, s * sec + 2 * h + c))
    scratch = [pltpu.VMEM((t, LANES), F32), pltpu.VMEM((t, LANES), F32), pltpu.VMEM((t, 2 * HEAD_DIM), F32)]
    stage = [pltpu.VMEM((2, 2, t, t), F32), pltpu.VMEM((2, 2, t, LANES), F32), pltpu.VMEM((2, t, t), F32)]
    return pl.pallas_call(
        functools.partial(_diff_kernel, t=t, nq=nq, lam_init=lam_init),
        grid=(batch, heads),
        in_specs=[half(9, 0), half(9, 1), half(10, 0), half(10, 1),
                  pl.BlockSpec((seq, 2 * HEAD_DIM), lambda b, h: (b, 11 * sec // 2 + h)),
                  pl.BlockSpec(lam_vecs.shape, lambda b, h: (0, 0)),
                  pl.BlockSpec((1, 2 * HEAD_DIM), lambda b, h: (0, 0))],
        out_specs=pl.BlockSpec((seq, 2 * HEAD_DIM), lambda b, h: (b, h)),
        out_shape=jax.ShapeDtypeStruct((n, width), BF16),
        scratch_shapes=stage + scratch + scratch,
        compiler_params=_params("parallel", "parallel"),
        name="differential_attention",
    )(qkv, qkv, qkv, qkv, qkv, lam_vecs, subln.reshape(1, -1))


def _merge_kernel(*refs):
    br, wb, gates, o_ref = refs[0:4], refs[4], refs[5:9], refs[9]
    acc = None
    for i in range(N_BRANCHES):
        term = gates[i][...].astype(F32) * jnp.dot(br[i][...], wb[i], preferred_element_type=F32)
        acc = term if acc is None else acc + term
    o_ref[...] = acc.astype(o_ref.dtype)


def _gated_merge(branches, w_branch, gates):
    n, width = branches[0].shape
    d = w_branch.shape[-1]
    tm, tn = _tile(1024, n), _tile(512, d)
    col_tiles = d // tn
    br_spec = pl.BlockSpec((tm, width), lambda i, j: (i, 0))
    gate_specs = [pl.BlockSpec((tm, tn), lambda i, j, g=g: (i, g * col_tiles + j)) for g in range(N_BRANCHES)]
    return pl.pallas_call(
        _merge_kernel,
        grid=(n // tm, col_tiles),
        in_specs=[br_spec] * N_BRANCHES + [pl.BlockSpec((N_BRANCHES, width, tn), lambda i, j: (0, 0, j))] + gate_specs,
        out_specs=pl.BlockSpec((tm, tn), lambda i, j: (i, j)),
        out_shape=jax.ShapeDtypeStruct((n, d), BF16),
        compiler_params=_params("parallel", "arbitrary"),
        name="gated_merge",
    )(*branches, w_branch, *([gates] * N_BRANCHES))


def _pack_halves(y):
    bits = lax.bitcast_convert_type(y.astype(BF16).astype(F32), jnp.uint32)
    half = y.shape[1] // 2
    return (bits[:, :half] >> 16) | bits[:, half:]


def _unpack_halves(p):
    lo = lax.bitcast_convert_type(p << 16, F32)
    hi = lax.bitcast_convert_type(p & jnp.uint32(0xFFFF0000), F32)
    return lo, hi


def _router_kernel(x_ref, g_ref, rw_ref, rb_ref, hp_ref, idx_ref, wt_ref, *, n_experts):
    x = x_ref[...]
    hn = x * lax.rsqrt(jnp.mean(x * x, axis=-1, keepdims=True) + RMS_EPS) * g_ref[...]
    hp_ref[...] = _pack_halves(hn)
    logits = None
    for wi, wp in enumerate(_split3(rw_ref[...])):
        for hi, hp in enumerate(_split3(hn)):
            if wi + hi <= 2:
                term = _dot_nt(wp, hp)
                logits = term if logits is None else logits + term
    score = jax.nn.sigmoid(logits)
    biased = score + rb_ref[...]
    per = n_experts // N_GROUPS
    rows = [biased[e:e + 1, :] for e in range(n_experts)]
    group_score = []
    for g in range(N_GROUPS):
        r = rows[g * per:(g + 1) * per]
        top1 = functools.reduce(jnp.maximum, r)
        first = []
        taken = jnp.zeros_like(top1, dtype=jnp.bool_)
        for v in r:
            is1 = (v == top1) & ~taken
            taken = taken | is1
            first.append(is1)
        rest = [jnp.where(f, NEG_INF, v) for f, v in zip(first, r)]
        top2 = functools.reduce(jnp.maximum, rest)
        second = []
        taken = jnp.zeros_like(top1, dtype=jnp.bool_)
        for v in rest:
            is2 = (v == top2) & ~taken
            taken = taken | is2
            second.append(is2)
        group_score.append((top1 + top2, [a | b for a, b in zip(first, second)]))
    best = functools.reduce(jnp.maximum, [gs for gs, _ in group_score])
    taken = jnp.zeros_like(best, dtype=jnp.bool_)
    picked = []
    for gs, sel in group_score:
        chosen = (gs == best) & ~taken
        taken = taken | chosen
        picked += [s & chosen for s in sel]
    w = [jnp.where(p, score[e:e + 1, :], 0.0) for e, p in enumerate(picked)]
    total = functools.reduce(jnp.add, w)
    inv = 1.0 / total
    lo = functools.reduce(jnp.minimum, [jnp.where(p, e, n_experts) for e, p in enumerate(picked)])
    hi = functools.reduce(jnp.maximum, [jnp.where(p, e, -1) for e, p in enumerate(picked)])
    idx_ref[0:1, :] = lo
    idx_ref[1:2, :] = hi
    wt_ref[0:1, :] = functools.reduce(jnp.add, [jnp.where(lo == e, we, 0.0) for e, we in enumerate(w)]) * inv
    wt_ref[1:2, :] = functools.reduce(jnp.add, [jnp.where(hi == e, we, 0.0) for e, we in enumerate(w)]) * inv


def _router(x, gain, router_w, router_bias):
    n, d = x.shape
    e = router_w.shape[1]
    tm = _tile(256, n)
    return pl.pallas_call(
        functools.partial(_router_kernel, n_experts=e),
        grid=(n // tm,),
        in_specs=[pl.BlockSpec((tm, d), lambda i: (i, 0)), pl.BlockSpec((1, d), lambda i: (0, 0)),
                  pl.BlockSpec((e, d), lambda i: (0, 0)), pl.BlockSpec((e, 1), lambda i: (0, 0))],
        out_specs=[pl.BlockSpec((tm, d // 2), lambda i: (i, 0)),
                   pl.BlockSpec((2, tm), lambda i: (0, i)), pl.BlockSpec((2, tm), lambda i: (0, i))],
        out_shape=[jax.ShapeDtypeStruct((n, d // 2), jnp.uint32),
                   jax.ShapeDtypeStruct((2, n), jnp.int32), jax.ShapeDtypeStruct((2, n), F32)],
        compiler_params=_params("parallel"),
        name="ffn_router",
    )(x, gain.reshape(1, d), router_w.T, router_bias.reshape(e, 1))


MOE_TILE = 256


def _moe_plan(idx, n_experts, tm):
    n = idx.shape[1]
    pairs = 2 * n
    n_rows = pairs + n_experts * tm
    expert = idx.reshape(pairs)
    onehot = (expert[:, None] == jnp.arange(n_experts)[None, :]).astype(jnp.int32)
    rank = jnp.sum((jnp.cumsum(onehot, axis=0) - onehot) * onehot, axis=1)
    group = -(-jnp.sum(onehot, axis=0) // tm) * tm
    group_end = jnp.cumsum(group)
    pos = (group_end - group)[expert] + rank
    pair_of_row = jnp.full((n_rows,), -1, jnp.int32).at[pos].set(jnp.arange(pairs, dtype=jnp.int32))
    is_pad = pair_of_row < 0
    spare = pairs + jnp.cumsum(is_pad.astype(jnp.int32)) - 1
    dst = jnp.where(is_pad, spare, pair_of_row)
    src = jnp.where(is_pad, 0, pair_of_row % n)
    tile_expert = jnp.minimum(
        jnp.searchsorted(group_end, jnp.arange(n_rows // tm, dtype=jnp.int32) * tm, side="right"), n_experts - 1)
    return src.reshape(-1, tm), dst.reshape(-1, tm), tile_expert.astype(jnp.int32)


def _expert_kernel(te_ref, src_ref, src_next_ref, dst_ref, x_hbm, wgu_ref, wd_ref, y_hbm,
                   xbuf, ybuf, gsem, ssem, *, tm, d_expert):
    t, n_tiles = pl.program_id(0), pl.num_programs(0)
    slot = t % 2

    def row_copy_in(ids_ref, s, g, k):
        return pltpu.make_async_copy(x_hbm.at[pl.ds(ids_ref[0, g * SUBLANES + k], 1), :],
                                     xbuf.at[s, g, pl.ds(k, 1), :], gsem.at[s])

    def row_copy_out(s, g, k):
        return pltpu.make_async_copy(ybuf.at[s, g, pl.ds(k, 1), :],
                                     y_hbm.at[pl.ds(dst_ref[0, g * SUBLANES + k], 1), :], ssem.at[s])

    def each_row(fn):
        def group(g, c):
            for k in range(SUBLANES):
                fn(g, k)
            return c
        lax.fori_loop(0, tm // SUBLANES, group, 0)

    def gather(ids_ref, s):
        each_row(lambda g, k: row_copy_in(ids_ref, s, g, k).start())

    def gather_wait(s):
        each_row(lambda g, k: row_copy_in(src_ref, s, g, k).wait())

    def scatter_start(s):
        each_row(lambda g, k: row_copy_out(s, g, k).start(priority=1))

    def scatter_wait(s):
        each_row(lambda g, k: row_copy_out(s, g, k).wait())

    @pl.when(t == 0)
    def _():
        gather(src_ref, 0)

    @pl.when(t + 1 < n_tiles)
    def _():
        gather(src_next_ref, 1 - slot)

    gather_wait(slot)
    xn = jnp.concatenate(_unpack_halves(xbuf[slot].reshape(tm, -1)), axis=1).astype(BF16)
    h = jnp.dot(xn, wgu_ref[...], preferred_element_type=F32)
    hg, hu = h[:, :d_expert], h[:, d_expert:]
    act = (hg * jax.nn.sigmoid(hg) * hu).astype(BF16)

    @pl.when(t >= 2)
    def _():
        scatter_wait(slot)

    ybuf[slot] = _pack_halves(jnp.dot(act, wd_ref[...], preferred_element_type=F32)).reshape(ybuf.shape[1:])
    scatter_start(slot)

    @pl.when(t == n_tiles - 1)
    def _():
        scatter_wait(slot)

        @pl.when(t >= 1)
        def _():
            scatter_wait(1 - slot)


def _moe_experts(hp, idx, wgu, wd):
    n, half = hp.shape
    d = 2 * half
    n_experts, _, two_f = wgu.shape
    tm = MOE_TILE
    src, dst, tile_expert = _moe_plan(idx, n_experts, tm)
    n_tiles = src.shape[0]
    src, dst = src.reshape(n_tiles, 1, tm), dst.reshape(n_tiles, 1, tm)
    smem = lambda fn: pl.BlockSpec((None, 1, tm), fn, memory_space=pltpu.SMEM)
    grid_spec = pltpu.PrefetchScalarGridSpec(
        num_scalar_prefetch=1,
        grid=(n_tiles,),
        in_specs=[smem(lambda t, te: (t, 0, 0)),
                  smem(lambda t, te: (jnp.minimum(t + 1, n_tiles - 1), 0, 0)),
                  smem(lambda t, te: (t, 0, 0)),
                  pl.BlockSpec(memory_space=pl.ANY),
                  pl.BlockSpec((None, d, two_f), lambda t, te: (te[t], 0, 0)),
                  pl.BlockSpec((None, two_f // 2, d), lambda t, te: (te[t], 0, 0))],
        out_specs=pl.BlockSpec(memory_space=pl.ANY),
        scratch_shapes=[pltpu.VMEM((2, tm // SUBLANES, SUBLANES, half), jnp.uint32),
                        pltpu.VMEM((2, tm // SUBLANES, SUBLANES, half), jnp.uint32),
                        pltpu.SemaphoreType.DMA((2,)), pltpu.SemaphoreType.DMA((2,))])
    return pl.pallas_call(
        functools.partial(_expert_kernel, tm=tm, d_expert=two_f // 2),
        grid_spec=grid_spec,
        out_shape=jax.ShapeDtypeStruct((n_tiles * tm, half), jnp.uint32),
        compiler_params=_params("arbitrary"),
        name="moe_experts",
    )(tile_expert, src, src, dst, hp, wgu, wd)


def _combine_kernel(x_ref, ya_ref, yb_ref, w_ref, o_ref):
    w = w_ref[...]
    half = ya_ref.shape[1]
    for c, (a, b) in enumerate(zip(_unpack_halves(ya_ref[...]), _unpack_halves(yb_ref[...]))):
        cols = slice(c * half, (c + 1) * half)
        o_ref[:, cols] = x_ref[:, cols] + w[:, 0:1] * a + w[:, 1:2] * b


def _moe_combine(x, y, wts):
    n, d = x.shape
    tm = _tile(256, n)
    nt = n // tm
    return pl.pallas_call(
        _combine_kernel,
        grid=(nt,),
        in_specs=[pl.BlockSpec((tm, d), lambda i: (i, 0)), pl.BlockSpec((tm, d // 2), lambda i: (i, 0)),
                  pl.BlockSpec((tm, d // 2), lambda i: (nt + i, 0)), pl.BlockSpec((tm, 2), lambda i: (i, 0))],
        out_specs=pl.BlockSpec((tm, d), lambda i: (i, 0)),
        out_shape=jax.ShapeDtypeStruct((n, d), F32),
        compiler_params=_params("parallel"),
        name="moe_combine",
    )(x, y, y, wts)


def _rope_tables(seq):
    half = HEAD_DIM // 2
    inv_freq = jnp.power(ROPE_THETA, -jnp.arange(half, dtype=F32) / half)
    ang = jnp.arange(seq, dtype=F32)[:, None] * inv_freq[None, :]
    cos, sin = jnp.cos(ang), jnp.sin(ang)
    return jnp.concatenate([cos, cos], axis=-1), jnp.concatenate([-sin, sin], axis=-1)


def kernel(x, attn_norm, w_in, fox_forget_bias, fox_q_norm, fox_k_norm, ch_q_norm, ch_k_norm, ch_rel_bias,
           diff_q_norm, diff_k_norm, diff_lambda, diff_subln, w_branch, w_out, ffn_norm, router_w, router_bias,
           w_gate, w_up, w_down):
    batch, seq, d_model = x.shape
    depth = w_in.shape[0]
    width = d_model // N_BRANCHES
    fox_heads = width // HEAD_DIM
    qkv_cols = 3 * width
    n_experts, _, d_expert = w_gate.shape[1:]
    n = batch * seq
    scale = HEAD_DIM ** -0.5
    assert fox_heads <= LANES and width % (2 * HEAD_DIM) == 0 and seq % CHUNK == 0

    cos_t, sin_t = _rope_tables(seq)
    ones = jnp.ones((HEAD_DIM,), F32)
    res_spec = lambda tm, tn: pl.BlockSpec((tm, tn), lambda i, j: (i, j))
    xf = x.reshape(n, d_model)

    for layer in range(depth):
        w_proj, col_amax = _stage_in_proj(w_in, layer, qkv_cols, fox_heads)
        w_qkv = w_proj
        w_f = _cast_lane_group(w_in, layer, qkv_cols, fox_heads)
        gate_col0 = 4 * qkv_cols
        w_g8 = _quantize_cols(w_proj, col_amax, gate_col0, N_BRANCHES * d_model)
        g_scale = col_amax[:, gate_col0:] * (1.0 / FP8_MAX)
        f_bias = jnp.pad(fox_forget_bias[layer], (0, LANES - fox_heads)).reshape(1, LANES)
        gains = jnp.stack([
            fox_q_norm[layer] * scale, fox_k_norm[layer], ones,
            ones * scale, ones, ones,
            ch_q_norm[layer] * scale, ch_k_norm[layer], ones,
            diff_q_norm[layer] * scale, diff_k_norm[layer], ones,
            ones, ones, ones, ones])

        h, h8, h_scale = _rmsnorm(xf, attn_norm[layer])
        qkv = _qkv_proj(h, w_qkv, gains, cos_t, sin_t, seq, width)
        gates = _matmul(h8, w_g8, _epi_scaled_sigmoid, BF16, extras=(h_scale, g_scale),
                        extra_specs=(lambda tm, tn: pl.BlockSpec((tm, LANES), lambda i, j: (i, 0)),
                                     lambda tm, tn: pl.BlockSpec((1, tn), lambda i, j: (0, j))),
                        name="gate_proj")
        cum = _forget_cumsum(h, w_f, f_bias, batch, seq)
        cum_t = cum[:, :fox_heads].T

        lam_init = 0.8 - 0.6 * math.exp(-0.3 * layer)
        branches = (
            _fox_attention(qkv, cum, cum_t, batch, seq, width),
            _sb_attention(qkv, batch, seq, width),
            _band_attention(qkv, ch_rel_bias[layer], batch, seq, width),
            _diff_attention(qkv, diff_lambda[layer], diff_subln[layer], lam_init, batch, seq, width),
        )
        merged = _gated_merge(branches, _cast_layer(w_branch, layer).reshape(w_branch.shape[1:]), gates)
        xf = _matmul(merged, _cast_layer(w_out, layer), _epi_residual, F32, extras=(xf,), extra_specs=(res_spec,),
                     tn=512, name="out_proj")

        hp, idx, wts = _router(xf, ffn_norm[layer], router_w, router_bias)
        wgu = _cast_gate_up(w_gate, w_up, layer)
        y = _moe_experts(hp, idx, wgu, _cast_layer(w_down, layer).reshape(w_down.shape[1:]))
        xf = _moe_combine(xf, y, wts.T)
    return xf.reshape(batch, seq, d_model)
```

```python
import functools
import math

import jax
import jax.numpy as jnp
from jax import lax
from jax.experimental import pallas as pl
from jax.experimental.pallas import tpu as pltpu

HEAD_DIM = 128
LANES = 128
SUBLANES = 8
CHUNK = 64
N_BRANCHES = 4
CH_LEFT_CHUNKS = 8
REL_CLIP = 128
ROPE_THETA = 10000.0
N_GROUPS = 4
RMS_EPS = 1e-6
NEG_INF = float("-inf")
VMEM_LIMIT_BYTES = 56 * 1024 * 1024

F32 = jnp.float32
BF16 = jnp.bfloat16


def _tile(pref, n):
    t = min(pref, n)
    while n % t:
        t //= 2
    return t


def _params(*sem):
    return pltpu.CompilerParams(dimension_semantics=sem, vmem_limit_bytes=VMEM_LIMIT_BYTES)


def _dot_nt(a, b):
    return lax.dot_general(a, b, (((1,), (1,)), ((), ())), preferred_element_type=F32)


def _log_sigmoid(x):
    return jnp.minimum(x, 0.0) - jnp.log1p(jnp.exp(-jnp.abs(x)))


FP8 = jnp.float8_e4m3fn
FP8_MAX = 448.0


def _rmsnorm_kernel(x_ref, g_ref, o_ref, o8_ref, s_ref):
    x = x_ref[...]
    ms = jnp.mean(x * x, axis=-1, keepdims=True)
    y = x * lax.rsqrt(ms + RMS_EPS) * g_ref[...]
    o_ref[...] = y.astype(o_ref.dtype)
    amax = jnp.max(jnp.abs(y), axis=-1, keepdims=True)
    o8_ref[...] = (y * jnp.where(amax > 0.0, FP8_MAX / amax, 0.0)).astype(o8_ref.dtype)
    s_ref[...] = jnp.broadcast_to(amax * (1.0 / FP8_MAX), s_ref.shape)


def _rmsnorm(x, gain):
    n, d = x.shape
    tm = _tile(256, n)
    row = pl.BlockSpec((tm, d), lambda i: (i, 0))
    return pl.pallas_call(
        _rmsnorm_kernel,
        grid=(n // tm,),
        in_specs=[row, pl.BlockSpec((1, d), lambda i: (0, 0))],
        out_specs=[row, row, pl.BlockSpec((tm, LANES), lambda i: (i, 0))],
        out_shape=[jax.ShapeDtypeStruct((n, d), BF16), jax.ShapeDtypeStruct((n, d), FP8),
                   jax.ShapeDtypeStruct((n, LANES), F32)],
        compiler_params=_params("parallel"),
        name="attn_rmsnorm",
    )(x, gain.reshape(1, d))


def _cast_kernel(w_ref, o_ref):
    o_ref[...] = w_ref[...].astype(o_ref.dtype)


def _cast_layer(w, layer):
    cols = w.shape[-1]
    rows = math.prod(w.shape[1:-1])
    w2 = w.reshape(w.shape[0] * rows, cols)
    tr, tc = _tile(1024, rows), _tile(2048, cols)
    return pl.pallas_call(
        _cast_kernel,
        grid=(rows // tr, cols // tc),
        in_specs=[pl.BlockSpec((tr, tc), lambda i, j: (layer * (rows // tr) + i, j))],
        out_specs=pl.BlockSpec((tr, tc), lambda i, j: (i, j)),
        out_shape=jax.ShapeDtypeStruct((rows, cols), BF16),
        compiler_params=_params("parallel", "parallel"),
        name="cast_weights",
    )(w2)


def _pair_cast_kernel(a_ref, b_ref, o_ref):
    f = a_ref.shape[-1]
    o_ref[:, :f] = a_ref[...].astype(o_ref.dtype)
    o_ref[:, f:] = b_ref[...].astype(o_ref.dtype)


def _cast_gate_up(w_gate, w_up, layer):
    _, e, d, f = w_gate.shape
    tr = _tile(1024, d)
    spec = pl.BlockSpec((None, None, tr, f), lambda x, i: (layer, x, i, 0))
    return pl.pallas_call(
        _pair_cast_kernel,
        grid=(e, d // tr),
        in_specs=[spec, spec],
        out_specs=pl.BlockSpec((None, tr, 2 * f), lambda x, i: (x, i, 0)),
        out_shape=jax.ShapeDtypeStruct((e, d, 2 * f), BF16),
        compiler_params=_params("parallel", "parallel"),
        name="cast_gate_up",
    )(w_gate, w_up)


def _in_proj_kernel(a_ref, b_ref, o_ref, amax_ref, *, first_shifted, shift):
    j, i = pl.program_id(0), pl.program_id(1)

    def emit(w):
        wb = w.astype(o_ref.dtype)
        o_ref[...] = wb
        col_max = jnp.max(jnp.abs(wb.astype(F32)), axis=0, keepdims=True)
        amax_ref[...] = jnp.where(i == 0, col_max, jnp.maximum(amax_ref[...], col_max))

    @pl.when(j < first_shifted)
    def _():
        emit(a_ref[...])

    @pl.when(j >= first_shifted)
    def _():
        tc = a_ref.shape[1]
        emit(jnp.concatenate([a_ref[...], b_ref[...]], axis=1)[:, shift:shift + tc])


def _stage_in_proj(w_in, layer, keep_cols, drop_cols):
    _, d, cols = w_in.shape
    out_cols = cols - drop_cols
    tc = _tile(1024, math.gcd(keep_cols, out_cols))
    tr = _tile(2048, d)
    assert tc % LANES == 0 and drop_cols < LANES
    lane_blocks = tc // LANES
    return pl.pallas_call(
        functools.partial(_in_proj_kernel, first_shifted=keep_cols // tc, shift=drop_cols),
        grid=(out_cols // tc, d // tr),
        in_specs=[pl.BlockSpec((None, tr, tc), lambda j, i: (layer, i, j)),
                  pl.BlockSpec((tr, LANES), lambda j, i: (layer * (d // tr) + i, (j + 1) * lane_blocks))],
        out_specs=[pl.BlockSpec((tr, tc), lambda j, i: (i, j)), pl.BlockSpec((1, tc), lambda j, i: (0, j))],
        out_shape=[jax.ShapeDtypeStruct((d, out_cols), BF16), jax.ShapeDtypeStruct((1, out_cols), F32)],
        compiler_params=_params("parallel", "arbitrary"),
        name="stage_in_proj",
    )(w_in, w_in.reshape(-1, cols))


def _lane_group_kernel(w_ref, o_ref, *, ncols):
    lane = lax.broadcasted_iota(jnp.int32, w_ref.shape, 1)
    o_ref[...] = jnp.where(lane < ncols, w_ref[...], 0.0).astype(o_ref.dtype)


def _cast_lane_group(w, layer, col0, ncols):
    _, d, _ = w.shape
    assert col0 % LANES == 0 and ncols <= LANES
    tr = _tile(2048, d)
    return pl.pallas_call(
        functools.partial(_lane_group_kernel, ncols=ncols),
        grid=(d // tr,),
        in_specs=[pl.BlockSpec((None, tr, LANES), lambda i: (layer, i, col0 // LANES))],
        out_specs=pl.BlockSpec((tr, LANES), lambda i: (i, 0)),
        out_shape=jax.ShapeDtypeStruct((d, LANES), BF16),
        compiler_params=_params("parallel"),
        name="cast_forget_weights",
    )(w)


def _quantize_kernel(w_ref, amax_ref, o_ref):
    amax = amax_ref[...]
    o_ref[...] = (w_ref[...].astype(F32) * jnp.where(amax > 0.0, FP8_MAX / amax, 0.0)).astype(o_ref.dtype)


def _quantize_cols(w, amax, col0, ncols):
    d = w.shape[0]
    tr, tc = _tile(2048, d), _tile(1024, math.gcd(col0, ncols))
    assert tc % LANES == 0
    off = col0 // tc
    return pl.pallas_call(
        _quantize_kernel,
        grid=(d // tr, ncols // tc),
        in_specs=[pl.BlockSpec((tr, tc), lambda i, j: (i, off + j)), pl.BlockSpec((1, tc), lambda i, j: (0, off + j))],
        out_specs=pl.BlockSpec((tr, tc), lambda i, j: (i, j)),
        out_shape=jax.ShapeDtypeStruct((d, ncols), FP8),
        compiler_params=_params("parallel", "parallel"),
        name="quantize_gate_weights",
    )(w, amax)


def _matmul(a, w, epilogue, out_dtype, extras=(), extra_specs=(), tm=1024, tn=1024, n_cols=None, name="matmul"):
    m, k = a.shape
    n = w.shape[1] if n_cols is None else n_cols
    tm, tn = _tile(tm, m), _tile(tn, n)

    def kern(a_ref, w_ref, *refs):
        acc = jnp.dot(a_ref[...], w_ref[...], preferred_element_type=F32)
        epilogue(acc, refs[:-1], refs[-1])

    return pl.pallas_call(
        kern,
        grid=(m // tm, n // tn),
        in_specs=[pl.BlockSpec((tm, k), lambda i, j: (i, 0)), pl.BlockSpec((k, tn), lambda i, j: (0, j)),
                  *[mk(tm, tn) for mk in extra_specs]],
        out_specs=pl.BlockSpec((tm, tn), lambda i, j: (i, j)),
        out_shape=jax.ShapeDtypeStruct((m, n), out_dtype),
        compiler_params=_params("parallel", "arbitrary"),
        name=name,
    )(a, w, *extras)


def _epi_scaled_sigmoid(acc, refs, o_ref):
    row_scale, col_scale = refs
    o_ref[...] = jax.nn.sigmoid(acc * row_scale[:, :1] * col_scale[...]).astype(o_ref.dtype)


def _epi_residual(acc, refs, o_ref):
    o_ref[...] = refs[0][...] + acc


PLAIN, NORM, NORM_ROPE = 0, 1, 2
SECTION_KINDS = (NORM, NORM, PLAIN, PLAIN, PLAIN, PLAIN, NORM, NORM, PLAIN, NORM_ROPE, NORM_ROPE, PLAIN)


def _qkv_epilogue(acc, refs, o_ref, *, sec_tiles):
    gain_ref, cos_ref, sin_ref = refs
    sec = pl.program_id(1) // sec_tiles
    g = gain_ref[pl.ds(sec, 1), :]
    tn = acc.shape[1]

    def run(kind):
        for hh in range(tn // HEAD_DIM):
            y = acc[:, hh * HEAD_DIM:(hh + 1) * HEAD_DIM]
            if kind != PLAIN:
                y = y * lax.rsqrt(jnp.mean(y * y, axis=-1, keepdims=True) + RMS_EPS)
            y = y * g
            if kind == NORM_ROPE:
                y = y * cos_ref[...] + pltpu.roll(y, HEAD_DIM // 2, 1) * sin_ref[...]
            o_ref[:, hh * HEAD_DIM:(hh + 1) * HEAD_DIM] = y.astype(o_ref.dtype)

    for kind in (PLAIN, NORM, NORM_ROPE):
        hit = functools.reduce(jnp.logical_or, [sec == s for s, kd in enumerate(SECTION_KINDS) if kd == kind])
        pl.when(hit)(functools.partial(run, kind))


def _qkv_proj(h, w_qkv, gains, cos_t, sin_t, seq, width):
    n = h.shape[0]
    tm = _tile(1024, seq)
    tn = _tile(1024, width)
    seq_tiles = seq // tm
    row_spec = lambda tm_, tn_: pl.BlockSpec((tm_, LANES), lambda i, j: (i % seq_tiles, 0))
    gain_spec = lambda tm_, tn_: pl.BlockSpec(gains.shape, lambda i, j: (0, 0))
    return _matmul(h, w_qkv, functools.partial(_qkv_epilogue, sec_tiles=width // tn), BF16,
                   extras=(gains, cos_t, sin_t), extra_specs=(gain_spec, row_spec, row_spec),
                   tm=tm, tn=tn, n_cols=len(SECTION_KINDS) * width, name="qkv_proj")


def _split3(x):
    hi = x.astype(BF16)
    r = x - hi.astype(F32)
    mid = r.astype(BF16)
    lo = (r - mid.astype(F32)).astype(BF16)
    return hi, mid, lo


def _fgate_kernel(h_ref, w_ref, b_ref, tri_ref, o_ref, carry_sc):
    @pl.when(pl.program_id(1) == 0)
    def _():
        carry_sc[...] = jnp.zeros_like(carry_sc)

    f = jnp.dot(h_ref[...], w_ref[...], preferred_element_type=F32) + b_ref[...]
    lf = _log_sigmoid(f)
    tri = tri_ref[...]
    c = carry_sc[...]
    for part in _split3(lf):
        c = c + jnp.dot(tri, part, preferred_element_type=F32)
    o_ref[...] = c
    carry_sc[...] = c[-1:, :]


def _forget_cumsum(h, w_f, bias, batch, seq):
    n, d = h.shape
    ts = _tile(512, seq)
    nt = seq // ts
    tri = (jnp.arange(ts)[:, None] >= jnp.arange(ts)[None, :]).astype(BF16)
    return pl.pallas_call(
        _fgate_kernel,
        grid=(batch, nt),
        in_specs=[pl.BlockSpec((ts, d), lambda b, t: (b * nt + t, 0)),
                  pl.BlockSpec((d, LANES), lambda b, t: (0, 0)),
                  pl.BlockSpec((1, LANES), lambda b, t: (0, 0)),
                  pl.BlockSpec((ts, ts), lambda b, t: (0, 0))],
        out_specs=pl.BlockSpec((ts, LANES), lambda b, t: (b * nt + t, 0)),
        out_shape=jax.ShapeDtypeStruct((n, LANES), F32),
        scratch_shapes=[pltpu.VMEM((1, LANES), F32)],
        compiler_params=_params("parallel", "arbitrary"),
        name="forget_cumsum",
    )(h, w_f, bias, tri)


def _softmax_step(s, v, m_sc, l_sc, acc_sc):
    m_prev = m_sc[...]
    m_new = jnp.maximum(m_prev, jnp.max(s, axis=1, keepdims=True))
    alpha = jnp.exp(m_prev - m_new)
    p = jnp.exp(s - m_new)
    l_sc[...] = alpha * l_sc[...] + jnp.sum(p, axis=1, keepdims=True)
    acc_sc[...] = alpha * acc_sc[...] + jnp.dot(p.astype(v.dtype), v, preferred_element_type=F32)
    m_sc[...] = m_new


def _softmax_init(m_sc, l_sc, acc_sc):
    m_sc[...] = jnp.full_like(m_sc, NEG_INF)
    l_sc[...] = jnp.zeros_like(l_sc)
    acc_sc[...] = jnp.zeros_like(acc_sc)


def _attn_specs(seq, tq, width, q_sec, k_sec, v_sec, v_mult=1, q_mult=1):
    nq = seq // tq
    sec = width // HEAD_DIM
    q_spec = lambda c=0: pl.BlockSpec((tq, HEAD_DIM), lambda b, h, i: (b * nq + i, q_sec * sec + h * q_mult + c))
    k_spec = lambda c=0: pl.BlockSpec((seq, HEAD_DIM), lambda b, h, i: (b, k_sec * sec + h * q_mult + c))
    v_spec = pl.BlockSpec((seq, HEAD_DIM * v_mult), lambda b, h, i: (b, v_sec * sec // v_mult + h))
    return q_spec, k_spec, v_spec


def _rep(x, n):
    return x if n == LANES else jnp.concatenate([x] * (n // LANES), axis=1)


def _lane_fold(x, op):
    out = x[:, :LANES]
    for c in range(1, x.shape[1] // LANES):
        out = op(out, x[:, c * LANES:(c + 1) * LANES])
    return out


def _row_max(s):
    return jnp.broadcast_to(jnp.max(_lane_fold(s, jnp.maximum), axis=1, keepdims=True), (s.shape[0], LANES))


def _softmax_update(s, mx, v, m_sc, l_sc, acc_sc):
    m_prev = m_sc[...]
    m_new = jnp.maximum(m_prev, mx)
    alpha = jnp.exp(m_prev - m_new)
    p = jnp.exp(s - _rep(m_new, s.shape[1]))
    l_sc[...] = alpha * l_sc[...] + _lane_fold(p, jnp.add)
    acc_sc[...] = (_rep(alpha, acc_sc.shape[1]) * acc_sc[...]
                   + jnp.dot(p.astype(v.dtype), v, preferred_element_type=F32))
    m_sc[...] = m_new


def _softmax_result(l_sc, acc_sc):
    return acc_sc[...] * (1.0 / jnp.sum(l_sc[...], axis=1, keepdims=True))


def _pipelined_causal(qi, score, update):
    @pl.when(qi == 0)
    def _():
        score(0, True)

    @pl.when(qi > 0)
    def _():
        score(0, False)

    def body(j, carry):
        update(j)
        score(j + 1, False)
        return carry

    lax.fori_loop(0, qi - 1, body, 0)

    @pl.when(qi > 0)
    def _():
        update(qi - 1)
        score(qi, True)

    update(qi)


def _tile_rows(i, t):
    return pl.ds(i * t, t) if isinstance(i, int) else pl.ds(pl.multiple_of(i * t, t), t)


def _flat_pipeline(n_pairs, advance, score, update, finish):
    score((0, 0), 0)

    def body(p, pair):
        nxt = advance(pair)
        update(pair, p % 2)
        score(nxt, (p + 1) % 2)
        finish(pair)
        return nxt

    last = lax.fori_loop(0, n_pairs - 1, body, (jnp.int32(0), jnp.int32(0)))
    update(last, (n_pairs - 1) % 2)
    finish(last)


def _causal_advance(pair):
    qi, j = pair
    more = j < qi
    return jnp.where(more, qi, qi + 1), jnp.where(more, j + 1, 0)


def _softmax_flat_update(pair, s, mx, v, m_sc, l_sc, acc_sc):
    m_prev = jnp.where(pair[1] == 0, NEG_INF, m_sc[...])
    m_new = jnp.maximum(m_prev, mx)
    alpha = jnp.exp(m_prev - m_new)
    p = jnp.exp(s - _rep(m_new, s.shape[1]))
    l_sc[...] = alpha * l_sc[...] + _lane_fold(p, jnp.add)
    acc_sc[...] = (_rep(alpha, acc_sc.shape[1]) * acc_sc[...]
                   + jnp.dot(p.astype(v.dtype), v, preferred_element_type=F32))
    m_sc[...] = m_new


def _diag_mask_init(mask_sc, visible):
    t = mask_sc.shape[-1]
    row = lax.broadcasted_iota(jnp.int32, (t, t), 0)
    col = lax.broadcasted_iota(jnp.int32, (t, t), 1)
    mask_sc[0] = jnp.zeros((t, t), F32)
    mask_sc[1] = jnp.where(visible(row, col), 0.0, NEG_INF)


def _is_diag(qi, j):
    return int(j == qi) if isinstance(j, int) else jnp.where(j < qi, 0, 1)


def _fox_kernel(q_ref, k_ref, v_ref, ct_ref, cs_ref, o_ref, s_buf, mx_buf, mask_sc, m_sc, l_sc, acc_sc, ct_sc,
                *, t, nq):
    h = pl.program_id(1)
    lane = lax.broadcasted_iota(jnp.int32, ct_ref.shape, 1)
    ct = jnp.sum(jnp.where(lane == h, ct_ref[...], 0.0), axis=1, keepdims=True)
    ct_sc[...] = jnp.broadcast_to(ct, ct_sc.shape)
    _softmax_init(m_sc, l_sc, acc_sc)
    _diag_mask_init(mask_sc, lambda row, col: col <= row)

    def score(pair, slot):
        qi, j = pair
        rows = _tile_rows(qi, t)
        s = _dot_nt(q_ref[rows, :], k_ref[_tile_rows(j, t), :])
        s = s + _rep(ct_sc[rows, :], t) - cs_ref[pl.ds(j, 1), :] + mask_sc[_is_diag(qi, j)]
        s_buf[slot] = s
        mx_buf[slot] = _row_max(s)

    def update(pair, slot):
        _softmax_flat_update(pair, s_buf[slot], mx_buf[slot], v_ref[_tile_rows(pair[1], t), :], m_sc, l_sc, acc_sc)

    def finish(pair):
        qi, j = pair

        @pl.when(j == qi)
        def _():
            o_ref[_tile_rows(qi, t), :] = _softmax_result(l_sc, acc_sc).astype(o_ref.dtype)

    _flat_pipeline(nq * (nq + 1) // 2, _causal_advance, score, update, finish)


def _fox_attention(qkv, cum, cum_t, batch, seq, width):
    n = qkv.shape[0]
    heads = width // HEAD_DIM
    t = _tile(512, seq)
    nq = seq // t
    head_cols = lambda s: pl.BlockSpec((seq, HEAD_DIM), lambda b, h: (b, s * heads + h))
    stat = pltpu.VMEM((t, LANES), F32)
    return pl.pallas_call(
        functools.partial(_fox_kernel, t=t, nq=nq),
        grid=(batch, heads),
        in_specs=[head_cols(0), head_cols(1), head_cols(2),
                  pl.BlockSpec((seq, LANES), lambda b, h: (b, 0)),
                  pl.BlockSpec((None, nq, t), lambda b, h: (h, b, 0))],
        out_specs=pl.BlockSpec((seq, HEAD_DIM), lambda b, h: (b, h)),
        out_shape=jax.ShapeDtypeStruct((n, width), BF16),
        scratch_shapes=[pltpu.VMEM((2, t, t), F32), pltpu.VMEM((2, t, LANES), F32), pltpu.VMEM((2, t, t), F32),
                        stat, stat, pltpu.VMEM((t, HEAD_DIM), F32), pltpu.VMEM((seq, LANES), F32)],
        compiler_params=_params("parallel", "parallel"),
        name="fox_attention",
    )(qkv, qkv, qkv, cum, cum_t.reshape(heads, n // t, t))


SB_DEAD_LOG = -110.0


def _sb_kernel(q_ref, k_ref, v_ref, u_ref, o_ref, e_buf, rs_buf, c_sc, acc_sc, *, t):
    qi = pl.program_id(2)
    c_sc[...] = jnp.zeros_like(c_sc)
    acc_sc[...] = jnp.zeros_like(acc_sc)

    def score(jj, diagonal):
        ks = pl.multiple_of((qi - jj) * t, t)
        z = _dot_nt(q_ref[...], k_ref[pl.ds(ks, t), :])
        softplus = jnp.log(1.0 + jnp.exp(-jnp.abs(z)))
        log_beta = jnp.minimum(z, 0.0) - softplus
        log_rest = jnp.minimum(-z, 0.0) - softplus
        if diagonal:
            row = lax.broadcasted_iota(jnp.int32, (t, t), 0)
            col = lax.broadcasted_iota(jnp.int32, (t, t), 1)
            strict = col < row
            log_rest = jnp.where(strict, log_rest, 0.0)
        hi = log_rest.astype(BF16)
        lo = (log_rest - hi.astype(F32)).astype(BF16)
        u = u_ref[...]
        cw = u.shape[0]
        later, tail = [], None
        for c in reversed(range(t // cw)):
            sl = slice(c * cw, (c + 1) * cw)
            inside = jnp.dot(hi[:, sl], u, preferred_element_type=F32) + jnp.dot(lo[:, sl], u, preferred_element_type=F32)
            later.insert(0, inside if tail is None else inside + _rep(tail, cw))
            chunk_sum = jnp.broadcast_to(
                jnp.sum(_lane_fold(log_rest[:, sl], jnp.add), axis=1, keepdims=True), (t, LANES))
            tail = chunk_sum if tail is None else tail + chunk_sum
        e = log_beta + jnp.concatenate(later, axis=1)
        if diagonal:
            e = jnp.where(strict, e, NEG_INF)
        e_buf[jj % 2] = e
        rs_buf[jj % 2] = tail

    def update(jj):
        ks = pl.multiple_of((qi - jj) * t, t)
        c = c_sc[...]
        w = jnp.exp(e_buf[jj % 2] + _rep(c, t))
        v = v_ref[pl.ds(ks, t), :]
        acc_sc[...] += jnp.dot(w.astype(v.dtype), v, preferred_element_type=F32)
        c_sc[...] = c + rs_buf[jj % 2]

    score(0, True)

    def body(state):
        jj, _ = state
        update(jj)
        score(jj + 1, False)
        return jj + 1, jnp.max(c_sc[...])

    jj, c_max = lax.while_loop(lambda st: (st[0] < qi) & (st[1] > SB_DEAD_LOG), body, (0, jnp.float32(0.0)))
    pl.when(c_max > SB_DEAD_LOG)(lambda: update(jj))
    o_ref[...] = acc_sc[...].astype(o_ref.dtype)


def _sb_attention(qkv, batch, seq, width):
    n = qkv.shape[0]
    heads = width // HEAD_DIM
    t = _tile(256, seq)
    nq = seq // t
    q_spec, k_spec, v_spec = _attn_specs(seq, t, width, 3, 4, 5)
    cw = min(256, t)
    upper = (jnp.arange(cw)[:, None] > jnp.arange(cw)[None, :]).astype(BF16)
    return pl.pallas_call(
        functools.partial(_sb_kernel, t=t),
        grid=(batch, heads, nq),
        in_specs=[q_spec(), k_spec(), v_spec, pl.BlockSpec((cw, cw), lambda b, h, i: (0, 0))],
        out_specs=pl.BlockSpec((t, HEAD_DIM), lambda b, h, i: (b * nq + i, h)),
        out_shape=jax.ShapeDtypeStruct((n, width), BF16),
        scratch_shapes=[pltpu.VMEM((2, t, t), F32), pltpu.VMEM((2, t, LANES), F32),
                        pltpu.VMEM((t, LANES), F32), pltpu.VMEM((t, HEAD_DIM), F32)],
        compiler_params=_params("parallel", "parallel", "arbitrary"),
        name="stick_breaking_attention",
    )(qkv, qkv, qkv, upper)


def _band_kernel(q_ref, k_ref, v_ref, tab_ref, o_ref, bias_sc, s_buf, mx_buf, m_sc, l_sc, acc_sc, *, t, nq, n_back):
    width = tab_ref.shape[-1]
    rolled = pltpu.roll(jnp.broadcast_to(tab_ref[...], (t, width)), 0, 1, stride=1, stride_axis=0)
    row = lax.broadcasted_iota(jnp.int32, (t, t), 0)
    col = lax.broadcasted_iota(jnp.int32, (t, t), 1)
    for d in range(n_back + 1):
        gap = (row + d * t) // CHUNK - col // CHUNK
        visible = (gap >= 0) & (gap <= CH_LEFT_CHUNKS)
        c0 = (n_back + 1 - d) * t
        bias_sc[d] = jnp.where(visible, rolled[:, c0:c0 + t], NEG_INF)
    _softmax_init(m_sc, l_sc, acc_sc)

    def advance(pair):
        qi, d = pair
        more = d < jnp.minimum(qi, n_back)
        return jnp.where(more, qi, qi + 1), jnp.where(more, d + 1, 0)

    def score(pair, slot):
        qi, d = pair
        s = _dot_nt(q_ref[_tile_rows(qi, t), :], k_ref[_tile_rows(qi - d, t), :]) + bias_sc[d]
        s_buf[slot] = s
        mx_buf[slot] = _row_max(s)

    def update(pair, slot):
        qi, d = pair
        _softmax_flat_update(pair, s_buf[slot], mx_buf[slot], v_ref[_tile_rows(qi - d, t), :], m_sc, l_sc, acc_sc)

    def finish(pair):
        qi, d = pair

        @pl.when(d == jnp.minimum(qi, n_back))
        def _():
            o_ref[_tile_rows(qi, t), :] = _softmax_result(l_sc, acc_sc).astype(o_ref.dtype)

    lead = min(nq, n_back)
    _flat_pipeline(nq * (n_back + 1) - lead * n_back + lead * (lead - 1) // 2, advance, score, update, finish)


def _band_table(rel_bias, t, n_back):
    rel = (n_back + 1) * t - jnp.arange((n_back + 2) * t)
    return rel_bias[:, None, jnp.clip(rel, -REL_CLIP, REL_CLIP) + REL_CLIP].astype(F32)


def _band_attention(qkv, rel_bias, batch, seq, width):
    n = qkv.shape[0]
    heads = width // HEAD_DIM
    t = _tile(512, seq)
    nq = seq // t
    n_back = -(-(CH_LEFT_CHUNKS * CHUNK) // t)
    table = _band_table(rel_bias, t, n_back)
    head_cols = lambda s: pl.BlockSpec((seq, HEAD_DIM), lambda b, h: (b, s * heads + h))
    stat = pltpu.VMEM((t, LANES), F32)
    return pl.pallas_call(
        functools.partial(_band_kernel, t=t, nq=nq, n_back=n_back),
        grid=(batch, heads),
        in_specs=[head_cols(6), head_cols(7), head_cols(8),
                  pl.BlockSpec((None, 1, table.shape[-1]), lambda b, h: (h, 0, 0))],
        out_specs=pl.BlockSpec((seq, HEAD_DIM), lambda b, h: (b, h)),
        out_shape=jax.ShapeDtypeStruct((n, width), BF16),
        scratch_shapes=[pltpu.VMEM((n_back + 1, t, t), F32), pltpu.VMEM((2, t, t), F32),
                        pltpu.VMEM((2, t, LANES), F32), stat, stat, pltpu.VMEM((t, HEAD_DIM), F32)],
        compiler_params=_params("parallel", "parallel"),
        name="chunk_band_attention",
    )(qkv, qkv, qkv, table)


def _diff_kernel(q1_ref, q2_ref, k1_ref, k2_ref, v_ref, lam_ref, g_ref, o_ref, s_buf, mx_buf, mask_sc,
                 m1_sc, l1_sc, a1_sc, m2_sc, l2_sc, a2_sc, *, t, nq, lam_init):
    streams = ((q1_ref, k1_ref, m1_sc, l1_sc, a1_sc), (q2_ref, k2_ref, m2_sc, l2_sc, a2_sc))
    _softmax_init(m1_sc, l1_sc, a1_sc)
    _softmax_init(m2_sc, l2_sc, a2_sc)
    _diag_mask_init(mask_sc, lambda row, col: col // CHUNK <= row // CHUNK)

    def score(pair, slot):
        qi, j = pair
        for i, (q_ref, k_ref, _, _, _) in enumerate(streams):
            s = _dot_nt(q_ref[_tile_rows(qi, t), :], k_ref[_tile_rows(j, t), :]) + mask_sc[_is_diag(qi, j)]
            s_buf[slot, i] = s
            mx_buf[slot, i] = _row_max(s)

    def update(pair, slot):
        v = v_ref[_tile_rows(pair[1], t), :]
        for i, (_, _, m_sc, l_sc, a_sc) in enumerate(streams):
            _softmax_flat_update(pair, s_buf[slot, i], mx_buf[slot, i], v, m_sc, l_sc, a_sc)

    def finish(pair):
        qi, j = pair

        @pl.when(j == qi)
        def _():
            lv = lam_ref[...]
            lam = (jnp.exp(jnp.sum(lv[0:1] * lv[1:2], axis=1, keepdims=True))
                   - jnp.exp(jnp.sum(lv[2:3] * lv[3:4], axis=1, keepdims=True)) + lam_init)
            o = _softmax_result(l1_sc, a1_sc) - lam * _softmax_result(l2_sc, a2_sc)
            y = o * lax.rsqrt(jnp.mean(o * o, axis=-1, keepdims=True) + RMS_EPS) * g_ref[...]
            o_ref[_tile_rows(qi, t), :] = (y * (1.0 - lam_init)).astype(o_ref.dtype)

    _flat_pipeline(nq * (nq + 1) // 2, _causal_advance, score, update, finish)


def _diff_attention(qkv, lam_vecs, subln, lam_init, batch, seq, width):
    n = qkv.shape[0]
    heads = width // (2 * HEAD_DIM)
    t = _tile(512, seq)
    nq = seq // t
    sec = width // HEAD_DIM
    half = lambda s, c: pl.BlockSpec((seq, HEAD_DIM), lambda b, h: (b, s * sec + 2 * h + c))
    scratch = [pltpu.VMEM((t, LANES), F32), pltpu.VMEM((t, LANES), F32), pltpu.VMEM((t, 2 * HEAD_DIM), F32)]
    stage = [pltpu.VMEM((2, 2, t, t), F32), pltpu.VMEM((2, 2, t, LANES), F32), pltpu.VMEM((2, t, t), F32)]
    return pl.pallas_call(
        functools.partial(_diff_kernel, t=t, nq=nq, lam_init=lam_init),
        grid=(batch, heads),
        in_specs=[half(9, 0), half(9, 1), half(10, 0), half(10, 1),
                  pl.BlockSpec((seq, 2 * HEAD_DIM), lambda b, h: (b, 11 * sec // 2 + h)),
                  pl.BlockSpec(lam_vecs.shape, lambda b, h: (0, 0)),
                  pl.BlockSpec((1, 2 * HEAD_DIM), lambda b, h: (0, 0))],
        out_specs=pl.BlockSpec((seq, 2 * HEAD_DIM), lambda b, h: (b, h)),
        out_shape=jax.ShapeDtypeStruct((n, width), BF16),
        scratch_shapes=stage + scratch + scratch,
        compiler_params=_params("parallel", "parallel"),
        name="differential_attention",
    )(qkv, qkv, qkv, qkv, qkv, lam_vecs, subln.reshape(1, -1))


def _merge_kernel(*refs):
    br, wb, gates, o_ref = refs[0:4], refs[4], refs[5:9], refs[9]
    acc = None
    for i in range(N_BRANCHES):
        term = gates[i][...].astype(F32) * jnp.dot(br[i][...], wb[i], preferred_element_type=F32)
        acc = term if acc is None else acc + term
    o_ref[...] = acc.astype(o_ref.dtype)


def _gated_merge(branches, w_branch, gates):
    n, width = branches[0].shape
    d = w_branch.shape[-1]
    tm, tn = _tile(1024, n), _tile(512, d)
    col_tiles = d // tn
    br_spec = pl.BlockSpec((tm, width), lambda i, j: (i, 0))
    gate_specs = [pl.BlockSpec((tm, tn), lambda i, j, g=g: (i, g * col_tiles + j)) for g in range(N_BRANCHES)]
    return pl.pallas_call(
        _merge_kernel,
        grid=(n // tm, col_tiles),
        in_specs=[br_spec] * N_BRANCHES + [pl.BlockSpec((N_BRANCHES, width, tn), lambda i, j: (0, 0, j))] + gate_specs,
        out_specs=pl.BlockSpec((tm, tn), lambda i, j: (i, j)),
        out_shape=jax.ShapeDtypeStruct((n, d), BF16),
        compiler_params=_params("parallel", "arbitrary"),
        name="gated_merge",
    )(*branches, w_branch, *([gates] * N_BRANCHES))


def _pack_halves(y):
    bits = lax.bitcast_convert_type(y.astype(BF16).astype(F32), jnp.uint32)
    half = y.shape[1] // 2
    return (bits[:, :half] >> 16) | bits[:, half:]


def _unpack_halves(p):
    lo = lax.bitcast_convert_type(p << 16, F32)
    hi = lax.bitcast_convert_type(p & jnp.uint32(0xFFFF0000), F32)
    return lo, hi


def _router_kernel(x_ref, g_ref, rw_ref, rb_ref, hp_ref, idx_ref, wt_ref, *, n_experts):
    x = x_ref[...]
    hn = x * lax.rsqrt(jnp.mean(x * x, axis=-1, keepdims=True) + RMS_EPS) * g_ref[...]
    hp_ref[...] = _pack_halves(hn)
    logits = None
    for wi, wp in enumerate(_split3(rw_ref[...])):
        for hi, hp in enumerate(_split3(hn)):
            if wi + hi <= 1:
                term = jnp.dot(hp, wp, preferred_element_type=F32)
                logits = term if logits is None else logits + term
    score = jax.nn.sigmoid(logits.T[:n_experts, :])
    biased = score + rb_ref[...]
    per = n_experts // N_GROUPS
    rows = [biased[e:e + 1, :] for e in range(n_experts)]
    group_score = []
    for g in range(N_GROUPS):
        r = rows[g * per:(g + 1) * per]
        top1 = functools.reduce(jnp.maximum, r)
        first = []
        taken = jnp.zeros_like(top1, dtype=jnp.bool_)
        for v in r:
            is1 = (v == top1) & ~taken
            taken = taken | is1
            first.append(is1)
        rest = [jnp.where(f, NEG_INF, v) for f, v in zip(first, r)]
        top2 = functools.reduce(jnp.maximum, rest)
        second = []
        taken = jnp.zeros_like(top1, dtype=jnp.bool_)
        for v in rest:
            is2 = (v == top2) & ~taken
            taken = taken | is2
            second.append(is2)
        group_score.append((top1 + top2, [a | b for a, b in zip(first, second)]))
    best = functools.reduce(jnp.maximum, [gs for gs, _ in group_score])
    taken = jnp.zeros_like(best, dtype=jnp.bool_)
    picked = []
    for gs, sel in group_score:
        chosen = (gs == best) & ~taken
        taken = taken | chosen
        picked += [s & chosen for s in sel]
    w = [jnp.where(p, score[e:e + 1, :], 0.0) for e, p in enumerate(picked)]
    total = functools.reduce(jnp.add, w)
    inv = 1.0 / total
    lo = functools.reduce(jnp.minimum, [jnp.where(p, e, n_experts) for e, p in enumerate(picked)])
    hi = functools.reduce(jnp.maximum, [jnp.where(p, e, -1) for e, p in enumerate(picked)])
    idx_ref[0:1, :] = lo
    idx_ref[1:2, :] = hi
    wt_ref[0:1, :] = functools.reduce(jnp.add, [jnp.where(lo == e, we, 0.0) for e, we in enumerate(w)]) * inv
    wt_ref[1:2, :] = functools.reduce(jnp.add, [jnp.where(hi == e, we, 0.0) for e, we in enumerate(w)]) * inv


def _router(x, gain, router_w, router_bias):
    n, d = x.shape
    e = router_w.shape[1]
    tm = _tile(256, n)
    return pl.pallas_call(
        functools.partial(_router_kernel, n_experts=e),
        grid=(n // tm,),
        in_specs=[pl.BlockSpec((tm, d), lambda i: (i, 0)), pl.BlockSpec((1, d), lambda i: (0, 0)),
                  pl.BlockSpec((d, LANES), lambda i: (0, 0)), pl.BlockSpec((e, 1), lambda i: (0, 0))],
        out_specs=[pl.BlockSpec((tm, d // 2), lambda i: (i, 0)),
                   pl.BlockSpec((2, tm), lambda i: (0, i)), pl.BlockSpec((2, tm), lambda i: (0, i))],
        out_shape=[jax.ShapeDtypeStruct((n, d // 2), jnp.uint32),
                   jax.ShapeDtypeStruct((2, n), jnp.int32), jax.ShapeDtypeStruct((2, n), F32)],
        compiler_params=_params("parallel"),
        name="ffn_router",
    )(x, gain.reshape(1, d), jnp.pad(router_w, ((0, 0), (0, LANES - e))), router_bias.reshape(e, 1))


MOE_TILE = 256


def _moe_plan(idx, n_experts, tm):
    n = idx.shape[1]
    pairs = 2 * n
    n_rows = pairs + n_experts * tm
    expert = idx.reshape(pairs)
    onehot = (expert[:, None] == jnp.arange(n_experts)[None, :]).astype(jnp.int32)
    rank = jnp.sum((jnp.cumsum(onehot, axis=0) - onehot) * onehot, axis=1)
    group = -(-jnp.sum(onehot, axis=0) // tm) * tm
    group_end = jnp.cumsum(group)
    pos = (group_end - group)[expert] + rank
    pair_of_row = jnp.full((n_rows,), -1, jnp.int32).at[pos].set(jnp.arange(pairs, dtype=jnp.int32))
    is_pad = pair_of_row < 0
    spare = pairs + jnp.cumsum(is_pad.astype(jnp.int32)) - 1
    dst = jnp.where(is_pad, spare, pair_of_row)
    src = jnp.where(is_pad, 0, pair_of_row % n)
    tile_expert = jnp.minimum(
        jnp.searchsorted(group_end, jnp.arange(n_rows // tm, dtype=jnp.int32) * tm, side="right"), n_experts - 1)
    return src.reshape(-1, tm), dst.reshape(-1, tm), tile_expert.astype(jnp.int32)


def _expert_kernel(te_ref, src_ref, src_next_ref, dst_ref, x_hbm, wgu_ref, wd_ref, y_hbm,
                   xbuf, ybuf, gsem, ssem, *, tm, d_expert):
    t, n_tiles = pl.program_id(0), pl.num_programs(0)
    slot = t % 2

    def row_copy_in(ids_ref, s, g, k):
        return pltpu.make_async_copy(x_hbm.at[pl.ds(ids_ref[0, g * SUBLANES + k], 1), :],
                                     xbuf.at[s, g, pl.ds(k, 1), :], gsem.at[s])

    def row_copy_out(s, g, k):
        return pltpu.make_async_copy(ybuf.at[s, g, pl.ds(k, 1), :],
                                     y_hbm.at[pl.ds(dst_ref[0, g * SUBLANES + k], 1), :], ssem.at[s])

    def each_row(fn):
        def group(g, c):
            for k in range(SUBLANES):
                fn(g, k)
            return c
        lax.fori_loop(0, tm // SUBLANES, group, 0)

    def gather(ids_ref, s):
        each_row(lambda g, k: row_copy_in(ids_ref, s, g, k).start())

    def gather_wait(s):
        each_row(lambda g, k: row_copy_in(src_ref, s, g, k).wait())

    def scatter_start(s):
        each_row(lambda g, k: row_copy_out(s, g, k).start(priority=1))

    def scatter_wait(s):
        each_row(lambda g, k: row_copy_out(s, g, k).wait())

    @pl.when(t == 0)
    def _():
        gather(src_ref, 0)

    @pl.when(t + 1 < n_tiles)
    def _():
        gather(src_next_ref, 1 - slot)

    gather_wait(slot)
    xn = jnp.concatenate(_unpack_halves(xbuf[slot].reshape(tm, -1)), axis=1).astype(BF16)
    h = jnp.dot(xn, wgu_ref[...], preferred_element_type=F32)
    hg, hu = h[:, :d_expert], h[:, d_expert:]
    act = (hg * jax.nn.sigmoid(hg) * hu).astype(BF16)

    @pl.when(t >= 2)
    def _():
        scatter_wait(slot)

    ybuf[slot] = _pack_halves(jnp.dot(act, wd_ref[...], preferred_element_type=F32)).reshape(ybuf.shape[1:])
    scatter_start(slot)

    @pl.when(t == n_tiles - 1)
    def _():
        scatter_wait(slot)

        @pl.when(t >= 1)
        def _():
            scatter_wait(1 - slot)


def _moe_experts(hp, idx, wgu, wd):
    n, half = hp.shape
    d = 2 * half
    n_experts, _, two_f = wgu.shape
    tm = MOE_TILE
    src, dst, tile_expert = _moe_plan(idx, n_experts, tm)
    n_tiles = src.shape[0]
    src, dst = src.reshape(n_tiles, 1, tm), dst.reshape(n_tiles, 1, tm)
    smem = lambda fn: pl.BlockSpec((None, 1, tm), fn, memory_space=pltpu.SMEM)
    grid_spec = pltpu.PrefetchScalarGridSpec(
        num_scalar_prefetch=1,
        grid=(n_tiles,),
        in_specs=[smem(lambda t, te: (t, 0, 0)),
                  smem(lambda t, te: (jnp.minimum(t + 1, n_tiles - 1), 0, 0)),
                  smem(lambda t, te: (t, 0, 0)),
                  pl.BlockSpec(memory_space=pl.ANY),
                  pl.BlockSpec((None, d, two_f), lambda t, te: (te[t], 0, 0)),
                  pl.BlockSpec((None, two_f // 2, d), lambda t, te: (te[t], 0, 0))],
        out_specs=pl.BlockSpec(memory_space=pl.ANY),
        scratch_shapes=[pltpu.VMEM((2, tm // SUBLANES, SUBLANES, half), jnp.uint32),
                        pltpu.VMEM((2, tm // SUBLANES, SUBLANES, half), jnp.uint32),
                        pltpu.SemaphoreType.DMA((2,)), pltpu.SemaphoreType.DMA((2,))])
    return pl.pallas_call(
        functools.partial(_expert_kernel, tm=tm, d_expert=two_f // 2),
        grid_spec=grid_spec,
        out_shape=jax.ShapeDtypeStruct((n_tiles * tm, half), jnp.uint32),
        compiler_params=_params("arbitrary"),
        name="moe_experts",
    )(tile_expert, src, src, dst, hp, wgu, wd)


def _combine_kernel(x_ref, ya_ref, yb_ref, w_ref, o_ref):
    w = w_ref[...]
    half = ya_ref.shape[1]
    for c, (a, b) in enumerate(zip(_unpack_halves(ya_ref[...]), _unpack_halves(yb_ref[...]))):
        cols = slice(c * half, (c + 1) * half)
        o_ref[:, cols] = x_ref[:, cols] + w[:, 0:1] * a + w[:, 1:2] * b


def _combine_norm_kernel(x_ref, ya_ref, yb_ref, w_ref, g_ref, xo_ref, o_ref, o8_ref, s_ref):
    _combine_kernel(x_ref, ya_ref, yb_ref, w_ref, xo_ref)
    _rmsnorm_kernel(xo_ref, g_ref, o_ref, o8_ref, s_ref)


def _moe_combine(x, y, wts, next_gain=None):
    n, d = x.shape
    tm = _tile(256, n)
    nt = n // tm
    row = pl.BlockSpec((tm, d), lambda i: (i, 0))
    in_specs = [row, pl.BlockSpec((tm, d // 2), lambda i: (i, 0)),
                pl.BlockSpec((tm, d // 2), lambda i: (nt + i, 0)), pl.BlockSpec((tm, 2), lambda i: (i, 0))]
    if next_gain is None:
        return pl.pallas_call(
            _combine_kernel, grid=(nt,), in_specs=in_specs, out_specs=row,
            out_shape=jax.ShapeDtypeStruct((n, d), F32), compiler_params=_params("parallel"), name="moe_combine",
        )(x, y, y, wts)
    return pl.pallas_call(
        _combine_norm_kernel,
        grid=(nt,),
        in_specs=in_specs + [pl.BlockSpec((1, d), lambda i: (0, 0))],
        out_specs=[row, row, row, pl.BlockSpec((tm, LANES), lambda i: (i, 0))],
        out_shape=[jax.ShapeDtypeStruct((n, d), F32), jax.ShapeDtypeStruct((n, d), BF16),
                   jax.ShapeDtypeStruct((n, d), FP8), jax.ShapeDtypeStruct((n, LANES), F32)],
        compiler_params=_params("parallel"),
        name="moe_combine_norm",
    )(x, y, y, wts, next_gain.reshape(1, d))


def _rope_tables(seq):
    half = HEAD_DIM // 2
    inv_freq = jnp.power(ROPE_THETA, -jnp.arange(half, dtype=F32) / half)
    ang = jnp.arange(seq, dtype=F32)[:, None] * inv_freq[None, :]
    cos, sin = jnp.cos(ang), jnp.sin(ang)
    return jnp.concatenate([cos, cos], axis=-1), jnp.concatenate([-sin, sin], axis=-1)


def kernel(x, attn_norm, w_in, fox_forget_bias, fox_q_norm, fox_k_norm, ch_q_norm, ch_k_norm, ch_rel_bias,
           diff_q_norm, diff_k_norm, diff_lambda, diff_subln, w_branch, w_out, ffn_norm, router_w, router_bias,
           w_gate, w_up, w_down):
    batch, seq, d_model = x.shape
    depth = w_in.shape[0]
    width = d_model // N_BRANCHES
    fox_heads = width // HEAD_DIM
    qkv_cols = 3 * width
    n_experts, _, d_expert = w_gate.shape[1:]
    n = batch * seq
    scale = HEAD_DIM ** -0.5
    assert fox_heads <= LANES and width % (2 * HEAD_DIM) == 0 and seq % CHUNK == 0

    cos_t, sin_t = _rope_tables(seq)
    ones = jnp.ones((HEAD_DIM,), F32)
    res_spec = lambda tm, tn: pl.BlockSpec((tm, tn), lambda i, j: (i, j))
    xf = x.reshape(n, d_model)

    for layer in range(depth):
        w_proj, col_amax = _stage_in_proj(w_in, layer, qkv_cols, fox_heads)
        w_qkv = w_proj
        w_f = _cast_lane_group(w_in, layer, qkv_cols, fox_heads)
        gate_col0 = 4 * qkv_cols
        w_g8 = _quantize_cols(w_proj, col_amax, gate_col0, N_BRANCHES * d_model)
        g_scale = col_amax[:, gate_col0:] * (1.0 / FP8_MAX)
        f_bias = jnp.pad(fox_forget_bias[layer], (0, LANES - fox_heads)).reshape(1, LANES)
        gains = jnp.stack([
            fox_q_norm[layer] * scale, fox_k_norm[layer], ones,
            ones * scale, ones, ones,
            ch_q_norm[layer] * scale, ch_k_norm[layer], ones,
            diff_q_norm[layer] * scale, diff_k_norm[layer], ones,
            ones, ones, ones, ones])

        if layer == 0:
            h, h8, h_scale = _rmsnorm(xf, attn_norm[layer])
        qkv = _qkv_proj(h, w_qkv, gains, cos_t, sin_t, seq, width)
        gates = _matmul(h8, w_g8, _epi_scaled_sigmoid, BF16, extras=(h_scale, g_scale),
                        extra_specs=(lambda tm, tn: pl.BlockSpec((tm, LANES), lambda i, j: (i, 0)),
                                     lambda tm, tn: pl.BlockSpec((1, tn), lambda i, j: (0, j))),
                        name="gate_proj")
        cum = _forget_cumsum(h, w_f, f_bias, batch, seq)
        cum_t = cum[:, :fox_heads].T

        lam_init = 0.8 - 0.6 * math.exp(-0.3 * layer)
        branches = (
            _fox_attention(qkv, cum, cum_t, batch, seq, width),
            _sb_attention(qkv, batch, seq, width),
            _band_attention(qkv, ch_rel_bias[layer], batch, seq, width),
            _diff_attention(qkv, diff_lambda[layer], diff_subln[layer], lam_init, batch, seq, width),
        )
        merged = _gated_merge(branches, _cast_layer(w_branch, layer).reshape(w_branch.shape[1:]), gates)
        xf = _matmul(merged, _cast_layer(w_out, layer), _epi_residual, F32, extras=(xf,), extra_specs=(res_spec,),
                     tn=512, name="out_proj")

        hp, idx, wts = _router(xf, ffn_norm[layer], router_w, router_bias)
        wgu = _cast_gate_up(w_gate, w_up, layer)
        y = _moe_experts(hp, idx, wgu, _cast_layer(w_down, layer).reshape(w_down.shape[1:]))
        if layer + 1 < depth:
            xf, h, h8, h_scale = _moe_combine(xf, y, wts.T, attn_norm[layer + 1])
        else:
            xf = _moe_combine(xf, y, wts.T)
    return xf.reshape(batch, seq, d_model)
```

```python
import functools
import math

import jax
import jax.numpy as jnp
from jax import lax
from jax.experimental import pallas as pl
from jax.experimental.pallas import tpu as pltpu

HEAD_DIM = 128
LANES = 128
SUBLANES = 8
CHUNK = 64
N_BRANCHES = 4
CH_LEFT_CHUNKS = 8
REL_CLIP = 128
ROPE_THETA = 10000.0
N_GROUPS = 4
RMS_EPS = 1e-6
NEG_INF = float("-inf")
VMEM_LIMIT_BYTES = 56 * 1024 * 1024

F32 = jnp.float32
BF16 = jnp.bfloat16


def _tile(pref, n):
    t = min(pref, n)
    while n % t:
        t //= 2
    return t


def _params(*sem):
    return pltpu.CompilerParams(dimension_semantics=sem, vmem_limit_bytes=VMEM_LIMIT_BYTES)


def _dot_nt(a, b):
    return lax.dot_general(a, b, (((1,), (1,)), ((), ())), preferred_element_type=F32)


def _log_sigmoid(x):
    return jnp.minimum(x, 0.0) - jnp.log1p(jnp.exp(-jnp.abs(x)))


FP8 = jnp.float8_e4m3fn
FP8_MAX = 448.0


def _rmsnorm_kernel(x_ref, g_ref, o_ref, o8_ref, s_ref):
    x = x_ref[...]
    ms = jnp.mean(x * x, axis=-1, keepdims=True)
    y = x * lax.rsqrt(ms + RMS_EPS) * g_ref[...]
    o_ref[...] = y.astype(o_ref.dtype)
    amax = jnp.max(jnp.abs(y), axis=-1, keepdims=True)
    o8_ref[...] = (y * jnp.where(amax > 0.0, FP8_MAX / amax, 0.0)).astype(o8_ref.dtype)
    s_ref[...] = jnp.broadcast_to(amax * (1.0 / FP8_MAX), s_ref.shape)


def _rmsnorm(x, gain):
    n, d = x.shape
    tm = _tile(256, n)
    row = pl.BlockSpec((tm, d), lambda i: (i, 0))
    return pl.pallas_call(
        _rmsnorm_kernel,
        grid=(n // tm,),
        in_specs=[row, pl.BlockSpec((1, d), lambda i: (0, 0))],
        out_specs=[row, row, pl.BlockSpec((tm, LANES), lambda i: (i, 0))],
        out_shape=[jax.ShapeDtypeStruct((n, d), BF16), jax.ShapeDtypeStruct((n, d), FP8),
                   jax.ShapeDtypeStruct((n, LANES), F32)],
        compiler_params=_params("parallel"),
        name="attn_rmsnorm",
    )(x, gain.reshape(1, d))


def _cast_kernel(w_ref, o_ref):
    o_ref[...] = w_ref[...].astype(o_ref.dtype)


def _cast_layer(w, layer):
    cols = w.shape[-1]
    rows = math.prod(w.shape[1:-1])
    w2 = w.reshape(w.shape[0] * rows, cols)
    tr, tc = _tile(1024, rows), _tile(2048, cols)
    return pl.pallas_call(
        _cast_kernel,
        grid=(rows // tr, cols // tc),
        in_specs=[pl.BlockSpec((tr, tc), lambda i, j: (layer * (rows // tr) + i, j))],
        out_specs=pl.BlockSpec((tr, tc), lambda i, j: (i, j)),
        out_shape=jax.ShapeDtypeStruct((rows, cols), BF16),
        compiler_params=_params("parallel", "parallel"),
        name="cast_weights",
    )(w2)


def _pair_cast_kernel(a_ref, b_ref, o_ref):
    f = a_ref.shape[-1]
    o_ref[:, :f] = a_ref[...].astype(o_ref.dtype)
    o_ref[:, f:] = b_ref[...].astype(o_ref.dtype)


def _cast_gate_up(w_gate, w_up, layer):
    _, e, d, f = w_gate.shape
    tr = _tile(1024, d)
    spec = pl.BlockSpec((None, None, tr, f), lambda x, i: (layer, x, i, 0))
    return pl.pallas_call(
        _pair_cast_kernel,
        grid=(e, d // tr),
        in_specs=[spec, spec],
        out_specs=pl.BlockSpec((None, tr, 2 * f), lambda x, i: (x, i, 0)),
        out_shape=jax.ShapeDtypeStruct((e, d, 2 * f), BF16),
        compiler_params=_params("parallel", "parallel"),
        name="cast_gate_up",
    )(w_gate, w_up)


def _in_proj_kernel(a_ref, b_ref, o_ref, amax_ref, *, first_shifted, shift):
    j, i = pl.program_id(0), pl.program_id(1)

    def emit(w):
        wb = w.astype(o_ref.dtype)
        o_ref[...] = wb
        col_max = jnp.max(jnp.abs(wb.astype(F32)), axis=0, keepdims=True)
        amax_ref[...] = jnp.where(i == 0, col_max, jnp.maximum(amax_ref[...], col_max))

    @pl.when(j < first_shifted)
    def _():
        emit(a_ref[...])

    @pl.when(j >= first_shifted)
    def _():
        tc = a_ref.shape[1]
        emit(jnp.concatenate([a_ref[...], b_ref[...]], axis=1)[:, shift:shift + tc])


def _stage_in_proj(w_in, layer, keep_cols, drop_cols):
    _, d, cols = w_in.shape
    out_cols = cols - drop_cols
    tc = _tile(1024, math.gcd(keep_cols, out_cols))
    tr = _tile(2048, d)
    assert tc % LANES == 0 and drop_cols < LANES
    lane_blocks = tc // LANES
    return pl.pallas_call(
        functools.partial(_in_proj_kernel, first_shifted=keep_cols // tc, shift=drop_cols),
        grid=(out_cols // tc, d // tr),
        in_specs=[pl.BlockSpec((tr, tc), lambda j, i: (layer * (d // tr) + i, j)),
                  pl.BlockSpec((tr, LANES), lambda j, i: (layer * (d // tr) + i, (j + 1) * lane_blocks))],
        out_specs=[pl.BlockSpec((tr, tc), lambda j, i: (i, j)), pl.BlockSpec((1, tc), lambda j, i: (0, j))],
        out_shape=[jax.ShapeDtypeStruct((d, out_cols), BF16), jax.ShapeDtypeStruct((1, out_cols), F32)],
        compiler_params=_params("parallel", "arbitrary"),
        name="stage_in_proj",
    )(w_in.reshape(-1, cols), w_in.reshape(-1, cols))


def _lane_group_kernel(w_ref, o_ref, *, ncols):
    lane = lax.broadcasted_iota(jnp.int32, w_ref.shape, 1)
    o_ref[...] = jnp.where(lane < ncols, w_ref[...], 0.0).astype(o_ref.dtype)


def _cast_lane_group(w, layer, col0, ncols):
    _, d, _ = w.shape
    assert col0 % LANES == 0 and ncols <= LANES
    tr = _tile(2048, d)
    return pl.pallas_call(
        functools.partial(_lane_group_kernel, ncols=ncols),
        grid=(d // tr,),
        in_specs=[pl.BlockSpec((tr, LANES), lambda i: (layer * (d // tr) + i, col0 // LANES))],
        out_specs=pl.BlockSpec((tr, LANES), lambda i: (i, 0)),
        out_shape=jax.ShapeDtypeStruct((d, LANES), BF16),
        compiler_params=_params("parallel"),
        name="cast_forget_weights",
    )(w.reshape(-1, w.shape[-1]))


def _quantize_kernel(w_ref, amax_ref, o_ref):
    amax = amax_ref[...]
    o_ref[...] = (w_ref[...].astype(F32) * jnp.where(amax > 0.0, FP8_MAX / amax, 0.0)).astype(o_ref.dtype)


def _quantize_cols(w, amax, col0, ncols):
    d = w.shape[0]
    tr, tc = _tile(2048, d), _tile(1024, math.gcd(col0, ncols))
    assert tc % LANES == 0
    off = col0 // tc
    return pl.pallas_call(
        _quantize_kernel,
        grid=(d // tr, ncols // tc),
        in_specs=[pl.BlockSpec((tr, tc), lambda i, j: (i, off + j)), pl.BlockSpec((1, tc), lambda i, j: (0, off + j))],
        out_specs=pl.BlockSpec((tr, tc), lambda i, j: (i, j)),
        out_shape=jax.ShapeDtypeStruct((d, ncols), FP8),
        compiler_params=_params("parallel", "parallel"),
        name="quantize_gate_weights",
    )(w, amax)


def _matmul(a, w, epilogue, out_dtype, extras=(), extra_specs=(), tm=1024, tn=1024, n_cols=None, name="matmul"):
    m, k = a.shape
    n = w.shape[1] if n_cols is None else n_cols
    tm, tn = _tile(tm, m), _tile(tn, n)

    def kern(a_ref, w_ref, *refs):
        acc = jnp.dot(a_ref[...], w_ref[...], preferred_element_type=F32)
        epilogue(acc, refs[:-1], refs[-1])

    return pl.pallas_call(
        kern,
        grid=(m // tm, n // tn),
        in_specs=[pl.BlockSpec((tm, k), lambda i, j: (i, 0)), pl.BlockSpec((k, tn), lambda i, j: (0, j)),
                  *[mk(tm, tn) for mk in extra_specs]],
        out_specs=pl.BlockSpec((tm, tn), lambda i, j: (i, j)),
        out_shape=jax.ShapeDtypeStruct((m, n), out_dtype),
        compiler_params=_params("parallel", "arbitrary"),
        name=name,
    )(a, w, *extras)


def _epi_scaled_sigmoid(acc, refs, o_ref):
    row_scale, col_scale = refs
    o_ref[...] = jax.nn.sigmoid(acc * row_scale[:, :1] * col_scale[...]).astype(o_ref.dtype)


def _epi_residual(acc, refs, o_ref):
    o_ref[...] = refs[0][...] + acc


PLAIN, NORM, NORM_ROPE = 0, 1, 2
SECTION_KINDS = (NORM, NORM, PLAIN, PLAIN, PLAIN, PLAIN, NORM, NORM, PLAIN, NORM_ROPE, NORM_ROPE, PLAIN)


def _qkv_epilogue(acc, refs, o_ref, *, sec_tiles):
    gain_ref, cos_ref, sin_ref = refs
    sec = pl.program_id(1) // sec_tiles
    g = gain_ref[pl.ds(sec, 1), :]
    tn = acc.shape[1]

    def run(kind):
        for hh in range(tn // HEAD_DIM):
            y = acc[:, hh * HEAD_DIM:(hh + 1) * HEAD_DIM]
            if kind != PLAIN:
                y = y * lax.rsqrt(jnp.mean(y * y, axis=-1, keepdims=True) + RMS_EPS)
            y = y * g
            if kind == NORM_ROPE:
                y = y * cos_ref[...] + pltpu.roll(y, HEAD_DIM // 2, 1) * sin_ref[...]
            o_ref[:, hh * HEAD_DIM:(hh + 1) * HEAD_DIM] = y.astype(o_ref.dtype)

    for kind in (PLAIN, NORM, NORM_ROPE):
        hit = functools.reduce(jnp.logical_or, [sec == s for s, kd in enumerate(SECTION_KINDS) if kd == kind])
        pl.when(hit)(functools.partial(run, kind))


def _qkv_proj(h, w_qkv, gains, cos_t, sin_t, seq, width):
    n = h.shape[0]
    tm = _tile(1024, seq)
    tn = _tile(1024, width)
    seq_tiles = seq // tm
    row_spec = lambda tm_, tn_: pl.BlockSpec((tm_, LANES), lambda i, j: (i % seq_tiles, 0))
    gain_spec = lambda tm_, tn_: pl.BlockSpec(gains.shape, lambda i, j: (0, 0))
    return _matmul(h, w_qkv, functools.partial(_qkv_epilogue, sec_tiles=width // tn), BF16,
                   extras=(gains, cos_t, sin_t), extra_specs=(gain_spec, row_spec, row_spec),
                   tm=tm, tn=tn, n_cols=len(SECTION_KINDS) * width, name="qkv_proj")


def _split3(x):
    hi = x.astype(BF16)
    r = x - hi.astype(F32)
    mid = r.astype(BF16)
    lo = (r - mid.astype(F32)).astype(BF16)
    return hi, mid, lo


def _fgate_kernel(h_ref, w_ref, b_ref, tri_ref, o_ref, carry_sc):
    @pl.when(pl.program_id(1) == 0)
    def _():
        carry_sc[...] = jnp.zeros_like(carry_sc)

    f = jnp.dot(h_ref[...], w_ref[...], preferred_element_type=F32) + b_ref[...]
    lf = _log_sigmoid(f)
    tri = tri_ref[...]
    c = carry_sc[...]
    for part in _split3(lf):
        c = c + jnp.dot(tri, part, preferred_element_type=F32)
    o_ref[...] = c
    carry_sc[...] = c[-1:, :]


def _forget_cumsum(h, w_f, bias, batch, seq):
    n, d = h.shape
    ts = _tile(512, seq)
    nt = seq // ts
    tri = (jnp.arange(ts)[:, None] >= jnp.arange(ts)[None, :]).astype(BF16)
    return pl.pallas_call(
        _fgate_kernel,
        grid=(batch, nt),
        in_specs=[pl.BlockSpec((ts, d), lambda b, t: (b * nt + t, 0)),
                  pl.BlockSpec((d, LANES), lambda b, t: (0, 0)),
                  pl.BlockSpec((1, LANES), lambda b, t: (0, 0)),
                  pl.BlockSpec((ts, ts), lambda b, t: (0, 0))],
        out_specs=pl.BlockSpec((ts, LANES), lambda b, t: (b * nt + t, 0)),
        out_shape=jax.ShapeDtypeStruct((n, LANES), F32),
        scratch_shapes=[pltpu.VMEM((1, LANES), F32)],
        compiler_params=_params("parallel", "arbitrary"),
        name="forget_cumsum",
    )(h, w_f, bias, tri)


def _softmax_step(s, v, m_sc, l_sc, acc_sc):
    m_prev = m_sc[...]
    m_new = jnp.maximum(m_prev, jnp.max(s, axis=1, keepdims=True))
    alpha = jnp.exp(m_prev - m_new)
    p = jnp.exp(s - m_new)
    l_sc[...] = alpha * l_sc[...] + jnp.sum(p, axis=1, keepdims=True)
    acc_sc[...] = alpha * acc_sc[...] + jnp.dot(p.astype(v.dtype), v, preferred_element_type=F32)
    m_sc[...] = m_new


def _softmax_init(m_sc, l_sc, acc_sc):
    m_sc[...] = jnp.full_like(m_sc, NEG_INF)
    l_sc[...] = jnp.zeros_like(l_sc)
    acc_sc[...] = jnp.zeros_like(acc_sc)


def _attn_specs(seq, tq, width, q_sec, k_sec, v_sec, v_mult=1, q_mult=1):
    nq = seq // tq
    sec = width // HEAD_DIM
    q_spec = lambda c=0: pl.BlockSpec((tq, HEAD_DIM), lambda b, h, i: (b * nq + i, q_sec * sec + h * q_mult + c))
    k_spec = lambda c=0: pl.BlockSpec((seq, HEAD_DIM), lambda b, h, i: (b, k_sec * sec + h * q_mult + c))
    v_spec = pl.BlockSpec((seq, HEAD_DIM * v_mult), lambda b, h, i: (b, v_sec * sec // v_mult + h))
    return q_spec, k_spec, v_spec


def _rep(x, n):
    return x if n == LANES else jnp.concatenate([x] * (n // LANES), axis=1)


def _lane_fold(x, op):
    out = x[:, :LANES]
    for c in range(1, x.shape[1] // LANES):
        out = op(out, x[:, c * LANES:(c + 1) * LANES])
    return out


def _row_max(s):
    return jnp.broadcast_to(jnp.max(_lane_fold(s, jnp.maximum), axis=1, keepdims=True), (s.shape[0], LANES))


def _softmax_update(s, mx, v, m_sc, l_sc, acc_sc):
    m_prev = m_sc[...]
    m_new = jnp.maximum(m_prev, mx)
    alpha = jnp.exp(m_prev - m_new)
    p = jnp.exp(s - _rep(m_new, s.shape[1]))
    l_sc[...] = alpha * l_sc[...] + _lane_fold(p, jnp.add)
    acc_sc[...] = (_rep(alpha, acc_sc.shape[1]) * acc_sc[...]
                   + jnp.dot(p.astype(v.dtype), v, preferred_element_type=F32))
    m_sc[...] = m_new


def _softmax_result(l_sc, acc_sc):
    return acc_sc[...] * (1.0 / jnp.sum(l_sc[...], axis=1, keepdims=True))


def _pipelined_causal(qi, score, update):
    @pl.when(qi == 0)
    def _():
        score(0, True)

    @pl.when(qi > 0)
    def _():
        score(0, False)

    def body(j, carry):
        update(j)
        score(j + 1, False)
        return carry

    lax.fori_loop(0, qi - 1, body, 0)

    @pl.when(qi > 0)
    def _():
        update(qi - 1)
        score(qi, True)

    update(qi)


def _tile_rows(i, t):
    return pl.ds(i * t, t) if isinstance(i, int) else pl.ds(pl.multiple_of(i * t, t), t)


def _flat_pipeline(n_pairs, advance, score, update, finish):
    score((0, 0), 0)

    def body(p, pair):
        nxt = advance(pair)
        update(pair, p % 2)
        score(nxt, (p + 1) % 2)
        finish(pair)
        return nxt

    last = lax.fori_loop(0, n_pairs - 1, body, (jnp.int32(0), jnp.int32(0)))
    update(last, (n_pairs - 1) % 2)
    finish(last)


def _causal_advance(pair):
    qi, j = pair
    more = j < qi
    return jnp.where(more, qi, qi + 1), jnp.where(more, j + 1, 0)


def _softmax_flat_update(pair, s, mx, v, m_sc, l_sc, acc_sc):
    m_prev = jnp.where(pair[1] == 0, NEG_INF, m_sc[...])
    m_new = jnp.maximum(m_prev, mx)
    alpha = jnp.exp(m_prev - m_new)
    p = jnp.exp(s - _rep(m_new, s.shape[1]))
    l_sc[...] = alpha * l_sc[...] + _lane_fold(p, jnp.add)
    acc_sc[...] = (_rep(alpha, acc_sc.shape[1]) * acc_sc[...]
                   + jnp.dot(p.astype(v.dtype), v, preferred_element_type=F32))
    m_sc[...] = m_new


def _diag_mask_init(mask_sc, visible):
    t = mask_sc.shape[-1]
    row = lax.broadcasted_iota(jnp.int32, (t, t), 0)
    col = lax.broadcasted_iota(jnp.int32, (t, t), 1)
    mask_sc[0] = jnp.zeros((t, t), F32)
    mask_sc[1] = jnp.where(visible(row, col), 0.0, NEG_INF)


def _is_diag(qi, j):
    return int(j == qi) if isinstance(j, int) else jnp.where(j < qi, 0, 1)


def _fox_kernel(q_ref, k_ref, v_ref, ct_ref, cs_ref, o_ref, s_buf, mx_buf, mask_sc, m_sc, l_sc, acc_sc, ct_sc,
                *, t, nq):
    h = pl.program_id(1)
    lane = lax.broadcasted_iota(jnp.int32, ct_ref.shape, 1)
    ct = jnp.sum(jnp.where(lane == h, ct_ref[...], 0.0), axis=1, keepdims=True)
    ct_sc[...] = jnp.broadcast_to(ct, ct_sc.shape)
    _softmax_init(m_sc, l_sc, acc_sc)
    _diag_mask_init(mask_sc, lambda row, col: col <= row)

    def score(pair, slot):
        qi, j = pair
        rows = _tile_rows(qi, t)
        s = _dot_nt(q_ref[rows, :], k_ref[_tile_rows(j, t), :])
        s = s + _rep(ct_sc[rows, :], t) - cs_ref[pl.ds(j, 1), :] + mask_sc[_is_diag(qi, j)]
        s_buf[slot] = s
        mx_buf[slot] = _row_max(s)

    def update(pair, slot):
        _softmax_flat_update(pair, s_buf[slot], mx_buf[slot], v_ref[_tile_rows(pair[1], t), :], m_sc, l_sc, acc_sc)

    def finish(pair):
        qi, j = pair

        @pl.when(j == qi)
        def _():
            o_ref[_tile_rows(qi, t), :] = _softmax_result(l_sc, acc_sc).astype(o_ref.dtype)

    _flat_pipeline(nq * (nq + 1) // 2, _causal_advance, score, update, finish)


def _fox_attention(qkv, cum, cum_t, batch, seq, width):
    n = qkv.shape[0]
    heads = width // HEAD_DIM
    t = _tile(512, seq)
    nq = seq // t
    head_cols = lambda s: pl.BlockSpec((seq, HEAD_DIM), lambda b, h: (b, s * heads + h))
    stat = pltpu.VMEM((t, LANES), F32)
    return pl.pallas_call(
        functools.partial(_fox_kernel, t=t, nq=nq),
        grid=(batch, heads),
        in_specs=[head_cols(0), head_cols(1), head_cols(2),
                  pl.BlockSpec((seq, LANES), lambda b, h: (b, 0)),
                  pl.BlockSpec((None, nq, t), lambda b, h: (h, b, 0))],
        out_specs=pl.BlockSpec((seq, HEAD_DIM), lambda b, h: (b, h)),
        out_shape=jax.ShapeDtypeStruct((n, width), BF16),
        scratch_shapes=[pltpu.VMEM((2, t, t), F32), pltpu.VMEM((2, t, LANES), F32), pltpu.VMEM((2, t, t), F32),
                        stat, stat, pltpu.VMEM((t, HEAD_DIM), F32), pltpu.VMEM((seq, LANES), F32)],
        compiler_params=_params("parallel", "parallel"),
        name="fox_attention",
    )(qkv, qkv, qkv, cum, cum_t.reshape(heads, n // t, t))


SB_DEAD_LOG = -110.0


def _sb_kernel(q_ref, k_ref, v_ref, u_ref, o_ref, e_buf, rs_buf, c_sc, acc_sc, *, t):
    qi = pl.program_id(2)
    c_sc[...] = jnp.zeros_like(c_sc)
    acc_sc[...] = jnp.zeros_like(acc_sc)

    def score(jj, diagonal):
        ks = pl.multiple_of((qi - jj) * t, t)
        z = _dot_nt(q_ref[...], k_ref[pl.ds(ks, t), :])
        softplus = jnp.log(1.0 + jnp.exp(-jnp.abs(z)))
        log_beta = jnp.minimum(z, 0.0) - softplus
        log_rest = jnp.minimum(-z, 0.0) - softplus
        if diagonal:
            row = lax.broadcasted_iota(jnp.int32, (t, t), 0)
            col = lax.broadcasted_iota(jnp.int32, (t, t), 1)
            strict = col < row
            log_rest = jnp.where(strict, log_rest, 0.0)
        hi = log_rest.astype(BF16)
        lo = (log_rest - hi.astype(F32)).astype(BF16)
        u = u_ref[...]
        cw = u.shape[0]
        later, tail = [], None
        for c in reversed(range(t // cw)):
            sl = slice(c * cw, (c + 1) * cw)
            inside = jnp.dot(hi[:, sl], u, preferred_element_type=F32) + jnp.dot(lo[:, sl], u, preferred_element_type=F32)
            later.insert(0, inside if tail is None else inside + _rep(tail, cw))
            chunk_sum = jnp.broadcast_to(
                jnp.sum(_lane_fold(log_rest[:, sl], jnp.add), axis=1, keepdims=True), (t, LANES))
            tail = chunk_sum if tail is None else tail + chunk_sum
        e = log_beta + jnp.concatenate(later, axis=1)
        if diagonal:
            e = jnp.where(strict, e, NEG_INF)
        e_buf[jj % 2] = e
        rs_buf[jj % 2] = tail

    def update(jj):
        ks = pl.multiple_of((qi - jj) * t, t)
        c = c_sc[...]
        w = jnp.exp(e_buf[jj % 2] + _rep(c, t))
        v = v_ref[pl.ds(ks, t), :]
        acc_sc[...] += jnp.dot(w.astype(v.dtype), v, preferred_element_type=F32)
        c_sc[...] = c + rs_buf[jj % 2]

    score(0, True)

    def body(state):
        jj, _ = state
        update(jj)
        score(jj + 1, False)
        return jj + 1, jnp.max(c_sc[...])

    jj, c_max = lax.while_loop(lambda st: (st[0] < qi) & (st[1] > SB_DEAD_LOG), body, (0, jnp.float32(0.0)))
    pl.when(c_max > SB_DEAD_LOG)(lambda: update(jj))
    o_ref[...] = acc_sc[...].astype(o_ref.dtype)


def _sb_attention(qkv, batch, seq, width):
    n = qkv.shape[0]
    heads = width // HEAD_DIM
    t = _tile(256, seq)
    nq = seq // t
    q_spec, k_spec, v_spec = _attn_specs(seq, t, width, 3, 4, 5)
    cw = min(256, t)
    upper = (jnp.arange(cw)[:, None] > jnp.arange(cw)[None, :]).astype(BF16)
    return pl.pallas_call(
        functools.partial(_sb_kernel, t=t),
        grid=(batch, heads, nq),
        in_specs=[q_spec(), k_spec(), v_spec, pl.BlockSpec((cw, cw), lambda b, h, i: (0, 0))],
        out_specs=pl.BlockSpec((t, HEAD_DIM), lambda b, h, i: (b * nq + i, h)),
        out_shape=jax.ShapeDtypeStruct((n, width), BF16),
        scratch_shapes=[pltpu.VMEM((2, t, t), F32), pltpu.VMEM((2, t, LANES), F32),
                        pltpu.VMEM((t, LANES), F32), pltpu.VMEM((t, HEAD_DIM), F32)],
        compiler_params=_params("parallel", "parallel", "arbitrary"),
        name="stick_breaking_attention",
    )(qkv, qkv, qkv, upper)


def _band_kernel(q_ref, k_ref, v_ref, tab_ref, o_ref, bias_sc, s_buf, mx_buf, m_sc, l_sc, acc_sc, *, t, nq, n_back):
    width = tab_ref.shape[-1]
    rolled = pltpu.roll(jnp.broadcast_to(tab_ref[...], (t, width)), 0, 1, stride=1, stride_axis=0)
    row = lax.broadcasted_iota(jnp.int32, (t, t), 0)
    col = lax.broadcasted_iota(jnp.int32, (t, t), 1)
    for d in range(n_back + 1):
        gap = (row + d * t) // CHUNK - col // CHUNK
        visible = (gap >= 0) & (gap <= CH_LEFT_CHUNKS)
        c0 = (n_back + 1 - d) * t
        bias_sc[d] = jnp.where(visible, rolled[:, c0:c0 + t], NEG_INF)
    _softmax_init(m_sc, l_sc, acc_sc)

    def advance(pair):
        qi, d = pair
        more = d < jnp.minimum(qi, n_back)
        return jnp.where(more, qi, qi + 1), jnp.where(more, d + 1, 0)

    def score(pair, slot):
        qi, d = pair
        s = _dot_nt(q_ref[_tile_rows(qi, t), :], k_ref[_tile_rows(qi - d, t), :]) + bias_sc[d]
        s_buf[slot] = s
        mx_buf[slot] = _row_max(s)

    def update(pair, slot):
        qi, d = pair
        _softmax_flat_update(pair, s_buf[slot], mx_buf[slot], v_ref[_tile_rows(qi - d, t), :], m_sc, l_sc, acc_sc)

    def finish(pair):
        qi, d = pair

        @pl.when(d == jnp.minimum(qi, n_back))
        def _():
            o_ref[_tile_rows(qi, t), :] = _softmax_result(l_sc, acc_sc).astype(o_ref.dtype)

    lead = min(nq, n_back)
    _flat_pipeline(nq * (n_back + 1) - lead * n_back + lead * (lead - 1) // 2, advance, score, update, finish)


def _band_table(rel_bias, t, n_back):
    rel = (n_back + 1) * t - jnp.arange((n_back + 2) * t)
    return rel_bias[:, None, jnp.clip(rel, -REL_CLIP, REL_CLIP) + REL_CLIP].astype(F32)


def _band_attention(qkv, rel_bias, batch, seq, width):
    n = qkv.shape[0]
    heads = width // HEAD_DIM
    t = _tile(512, seq)
    nq = seq // t
    n_back = -(-(CH_LEFT_CHUNKS * CHUNK) // t)
    table = _band_table(rel_bias, t, n_back)
    head_cols = lambda s: pl.BlockSpec((seq, HEAD_DIM), lambda b, h: (b, s * heads + h))
    stat = pltpu.VMEM((t, LANES), F32)
    return pl.pallas_call(
        functools.partial(_band_kernel, t=t, nq=nq, n_back=n_back),
        grid=(batch, heads),
        in_specs=[head_cols(6), head_cols(7), head_cols(8),
                  pl.BlockSpec((None, 1, table.shape[-1]), lambda b, h: (h, 0, 0))],
        out_specs=pl.BlockSpec((seq, HEAD_DIM), lambda b, h: (b, h)),
        out_shape=jax.ShapeDtypeStruct((n, width), BF16),
        scratch_shapes=[pltpu.VMEM((n_back + 1, t, t), F32), pltpu.VMEM((2, t, t), F32),
                        pltpu.VMEM((2, t, LANES), F32), stat, stat, pltpu.VMEM((t, HEAD_DIM), F32)],
        compiler_params=_params("parallel", "parallel"),
        name="chunk_band_attention",
    )(qkv, qkv, qkv, table)


def _diff_kernel(q1_ref, q2_ref, k1_ref, k2_ref, v_ref, lam_ref, g_ref, o_ref, s_buf, mx_buf, mask_sc,
                 m1_sc, l1_sc, a1_sc, m2_sc, l2_sc, a2_sc, *, t, nq, lam_init):
    streams = ((q1_ref, k1_ref, m1_sc, l1_sc, a1_sc), (q2_ref, k2_ref, m2_sc, l2_sc, a2_sc))
    _softmax_init(m1_sc, l1_sc, a1_sc)
    _softmax_init(m2_sc, l2_sc, a2_sc)
    _diag_mask_init(mask_sc, lambda row, col: col // CHUNK <= row // CHUNK)

    def score(pair, slot):
        qi, j = pair
        for i, (q_ref, k_ref, _, _, _) in enumerate(streams):
            s = _dot_nt(q_ref[_tile_rows(qi, t), :], k_ref[_tile_rows(j, t), :]) + mask_sc[_is_diag(qi, j)]
            s_buf[slot, i] = s
            mx_buf[slot, i] = _row_max(s)

    def update(pair, slot):
        v = v_ref[_tile_rows(pair[1], t), :]
        for i, (_, _, m_sc, l_sc, a_sc) in enumerate(streams):
            _softmax_flat_update(pair, s_buf[slot, i], mx_buf[slot, i], v, m_sc, l_sc, a_sc)

    def finish(pair):
        qi, j = pair

        @pl.when(j == qi)
        def _():
            lv = lam_ref[...]
            lam = (jnp.exp(jnp.sum(lv[0:1] * lv[1:2], axis=1, keepdims=True))
                   - jnp.exp(jnp.sum(lv[2:3] * lv[3:4], axis=1, keepdims=True)) + lam_init)
            o = _softmax_result(l1_sc, a1_sc) - lam * _softmax_result(l2_sc, a2_sc)
            y = o * lax.rsqrt(jnp.mean(o * o, axis=-1, keepdims=True) + RMS_EPS) * g_ref[...]
            o_ref[_tile_rows(qi, t), :] = (y * (1.0 - lam_init)).astype(o_ref.dtype)

    _flat_pipeline(nq * (nq + 1) // 2, _causal_advance, score, update, finish)


def _diff_attention(qkv, lam_vecs, subln, lam_init, batch, seq, width):
    n = qkv.shape[0]
    heads = width // (2 * HEAD_DIM)
    t = _tile(512, seq)
    nq = seq // t
    sec = width // HEAD_DIM
    half = lambda s, c: pl.BlockSpec((seq, HEAD_DIM), lambda b, h: (b, s * sec + 2 * h + c))
    scratch = [pltpu.VMEM((t, LANES), F32), pltpu.VMEM((t, LANES), F32), pltpu.VMEM((t, 2 * HEAD_DIM), F32)]
    stage = [pltpu.VMEM((2, 2, t, t), F32), pltpu.VMEM((2, 2, t, LANES), F32), pltpu.VMEM((2, t, t), F32)]
    return pl.pallas_call(
        functools.partial(_diff_kernel, t=t, nq=nq, lam_init=lam_init),
        grid=(batch, heads),
        in_specs=[half(9, 0), half(9, 1), half(10, 0), half(10, 1),
                  pl.BlockSpec((seq, 2 * HEAD_DIM), lambda b, h: (b, 11 * sec // 2 + h)),
                  pl.BlockSpec(lam_vecs.shape, lambda b, h: (0, 0)),
                  pl.BlockSpec((1, 2 * HEAD_DIM), lambda b, h: (0, 0))],
        out_specs=pl.BlockSpec((seq, 2 * HEAD_DIM), lambda b, h: (b, h)),
        out_shape=jax.ShapeDtypeStruct((n, width), BF16),
        scratch_shapes=stage + scratch + scratch,
        compiler_params=_params("parallel", "parallel"),
        name="differential_attention",
    )(qkv, qkv, qkv, qkv, qkv, lam_vecs, subln.reshape(1, -1))


def _merge_kernel(*refs):
    br, wb, gates, o_ref = refs[0:4], refs[4], refs[5:9], refs[9]
    acc = None
    for i in range(N_BRANCHES):
        term = gates[i][...].astype(F32) * jnp.dot(br[i][...], wb[i], preferred_element_type=F32)
        acc = term if acc is None else acc + term
    o_ref[...] = acc.astype(o_ref.dtype)


def _gated_merge(branches, w_branch, gates):
    n, width = branches[0].shape
    d = w_branch.shape[-1]
    tm, tn = _tile(1024, n), _tile(512, d)
    col_tiles = d // tn
    br_spec = pl.BlockSpec((tm, width), lambda i, j: (i, 0))
    gate_specs = [pl.BlockSpec((tm, tn), lambda i, j, g=g: (i, g * col_tiles + j)) for g in range(N_BRANCHES)]
    return pl.pallas_call(
        _merge_kernel,
        grid=(n // tm, col_tiles),
        in_specs=[br_spec] * N_BRANCHES + [pl.BlockSpec((N_BRANCHES, width, tn), lambda i, j: (0, 0, j))] + gate_specs,
        out_specs=pl.BlockSpec((tm, tn), lambda i, j: (i, j)),
        out_shape=jax.ShapeDtypeStruct((n, d), BF16),
        compiler_params=_params("parallel", "arbitrary"),
        name="gated_merge",
    )(*branches, w_branch, *([gates] * N_BRANCHES))


def _pack_halves(y):
    bits = lax.bitcast_convert_type(y.astype(BF16).astype(F32), jnp.uint32)
    half = y.shape[1] // 2
    return (bits[:, :half] >> 16) | bits[:, half:]


def _unpack_halves(p):
    lo = lax.bitcast_convert_type(p << 16, F32)
    hi = lax.bitcast_convert_type(p & jnp.uint32(0xFFFF0000), F32)
    return lo, hi


def _router_kernel(x_ref, g_ref, rw_ref, rb_ref, hp_ref, idx_ref, wt_ref, *, n_experts):
    x = x_ref[...]
    hn = x * lax.rsqrt(jnp.mean(x * x, axis=-1, keepdims=True) + RMS_EPS) * g_ref[...]
    hp_ref[...] = _pack_halves(hn)
    logits = None
    for wi, wp in enumerate(_split3(rw_ref[...])):
        for hi, hp in enumerate(_split3(hn)):
            if wi + hi <= 1:
                term = jnp.dot(hp, wp, preferred_element_type=F32)
                logits = term if logits is None else logits + term
    score = jax.nn.sigmoid(logits.T[:n_experts, :])
    biased = score + rb_ref[...]
    per = n_experts // N_GROUPS
    rows = [biased[e:e + 1, :] for e in range(n_experts)]
    group_score = []
    for g in range(N_GROUPS):
        r = rows[g * per:(g + 1) * per]
        top1 = functools.reduce(jnp.maximum, r)
        first = []
        taken = jnp.zeros_like(top1, dtype=jnp.bool_)
        for v in r:
            is1 = (v == top1) & ~taken
            taken = taken | is1
            first.append(is1)
        rest = [jnp.where(f, NEG_INF, v) for f, v in zip(first, r)]
        top2 = functools.reduce(jnp.maximum, rest)
        second = []
        taken = jnp.zeros_like(top1, dtype=jnp.bool_)
        for v in rest:
            is2 = (v == top2) & ~taken
            taken = taken | is2
            second.append(is2)
        group_score.append((top1 + top2, [a | b for a, b in zip(first, second)]))
    best = functools.reduce(jnp.maximum, [gs for gs, _ in group_score])
    taken = jnp.zeros_like(best, dtype=jnp.bool_)
    picked = []
    for gs, sel in group_score:
        chosen = (gs == best) & ~taken
        taken = taken | chosen
        picked += [s & chosen for s in sel]
    w = [jnp.where(p, score[e:e + 1, :], 0.0) for e, p in enumerate(picked)]
    total = functools.reduce(jnp.add, w)
    inv = 1.0 / total
    lo = functools.reduce(jnp.minimum, [jnp.where(p, e, n_experts) for e, p in enumerate(picked)])
    hi = functools.reduce(jnp.maximum, [jnp.where(p, e, -1) for e, p in enumerate(picked)])
    idx_ref[0:1, :] = lo
    idx_ref[1:2, :] = hi
    wt_ref[0:1, :] = functools.reduce(jnp.add, [jnp.where(lo == e, we, 0.0) for e, we in enumerate(w)]) * inv
    wt_ref[1:2, :] = functools.reduce(jnp.add, [jnp.where(hi == e, we, 0.0) for e, we in enumerate(w)]) * inv


def _router(x, gain, router_w, router_bias):
    n, d = x.shape
    e = router_w.shape[1]
    tm = _tile(256, n)
    return pl.pallas_call(
        functools.partial(_router_kernel, n_experts=e),
        grid=(n // tm,),
        in_specs=[pl.BlockSpec((tm, d), lambda i: (i, 0)), pl.BlockSpec((1, d), lambda i: (0, 0)),
                  pl.BlockSpec((d, LANES), lambda i: (0, 0)), pl.BlockSpec((e, 1), lambda i: (0, 0))],
        out_specs=[pl.BlockSpec((tm, d // 2), lambda i: (i, 0)),
                   pl.BlockSpec((2, tm), lambda i: (0, i)), pl.BlockSpec((2, tm), lambda i: (0, i))],
        out_shape=[jax.ShapeDtypeStruct((n, d // 2), jnp.uint32),
                   jax.ShapeDtypeStruct((2, n), jnp.int32), jax.ShapeDtypeStruct((2, n), F32)],
        compiler_params=_params("parallel"),
        name="ffn_router",
    )(x, gain.reshape(1, d), jnp.pad(router_w, ((0, 0), (0, LANES - e))), router_bias.reshape(e, 1))


MOE_TILE = 256


def _moe_plan(idx, n_experts, tm):
    n = idx.shape[1]
    pairs = 2 * n
    n_rows = pairs + n_experts * tm
    expert = idx.reshape(pairs)
    onehot = (expert[:, None] == jnp.arange(n_experts)[None, :]).astype(jnp.int32)
    rank = jnp.sum((jnp.cumsum(onehot, axis=0) - onehot) * onehot, axis=1)
    group = -(-jnp.sum(onehot, axis=0) // tm) * tm
    group_end = jnp.cumsum(group)
    pos = (group_end - group)[expert] + rank
    pair_of_row = jnp.full((n_rows,), -1, jnp.int32).at[pos].set(jnp.arange(pairs, dtype=jnp.int32))
    is_pad = pair_of_row < 0
    spare = pairs + jnp.cumsum(is_pad.astype(jnp.int32)) - 1
    dst = jnp.where(is_pad, spare, pair_of_row)
    src = jnp.where(is_pad, 0, pair_of_row % n)
    tile_expert = jnp.minimum(
        jnp.searchsorted(group_end, jnp.arange(n_rows // tm, dtype=jnp.int32) * tm, side="right"), n_experts - 1)
    return src.reshape(-1, tm), dst.reshape(-1, tm), tile_expert.astype(jnp.int32)


def _expert_kernel(te_ref, src_ref, src_next_ref, dst_ref, x_hbm, wgu_ref, wd_ref, y_hbm,
                   xbuf, ybuf, gsem, ssem, *, tm, d_expert):
    t, n_tiles = pl.program_id(0), pl.num_programs(0)
    slot = t % 2

    def row_copy_in(ids_ref, s, g, k):
        return pltpu.make_async_copy(x_hbm.at[pl.ds(ids_ref[0, g * SUBLANES + k], 1), :],
                                     xbuf.at[s, g, pl.ds(k, 1), :], gsem.at[s])

    def row_copy_out(s, g, k):
        return pltpu.make_async_copy(ybuf.at[s, g, pl.ds(k, 1), :],
                                     y_hbm.at[pl.ds(dst_ref[0, g * SUBLANES + k], 1), :], ssem.at[s])

    def each_row(fn):
        def group(g, c):
            for k in range(SUBLANES):
                fn(g, k)
            return c
        lax.fori_loop(0, tm // SUBLANES, group, 0)

    def gather(ids_ref, s):
        each_row(lambda g, k: row_copy_in(ids_ref, s, g, k).start())

    def gather_wait(s):
        each_row(lambda g, k: row_copy_in(src_ref, s, g, k).wait())

    def scatter_start(s):
        each_row(lambda g, k: row_copy_out(s, g, k).start(priority=1))

    def scatter_wait(s):
        each_row(lambda g, k: row_copy_out(s, g, k).wait())

    @pl.when(t == 0)
    def _():
        gather(src_ref, 0)

    @pl.when(t + 1 < n_tiles)
    def _():
        gather(src_next_ref, 1 - slot)

    gather_wait(slot)
    xn = jnp.concatenate(_unpack_halves(xbuf[slot].reshape(tm, -1)), axis=1).astype(BF16)
    h = jnp.dot(xn, wgu_ref[...], preferred_element_type=F32)
    hg, hu = h[:, :d_expert], h[:, d_expert:]
    act = (hg * jax.nn.sigmoid(hg) * hu).astype(BF16)

    @pl.when(t >= 2)
    def _():
        scatter_wait(slot)

    ybuf[slot] = _pack_halves(jnp.dot(act, wd_ref[...], preferred_element_type=F32)).reshape(ybuf.shape[1:])
    scatter_start(slot)

    @pl.when(t == n_tiles - 1)
    def _():
        scatter_wait(slot)

        @pl.when(t >= 1)
        def _():
            scatter_wait(1 - slot)


def _moe_experts(hp, idx, wgu, wd):
    n, half = hp.shape
    d = 2 * half
    n_experts, _, two_f = wgu.shape
    tm = MOE_TILE
    src, dst, tile_expert = _moe_plan(idx, n_experts, tm)
    n_tiles = src.shape[0]
    src, dst = src.reshape(n_tiles, 1, tm), dst.reshape(n_tiles, 1, tm)
    smem = lambda fn: pl.BlockSpec((None, 1, tm), fn, memory_space=pltpu.SMEM)
    grid_spec = pltpu.PrefetchScalarGridSpec(
        num_scalar_prefetch=1,
        grid=(n_tiles,),
        in_specs=[smem(lambda t, te: (t, 0, 0)),
                  smem(lambda t, te: (jnp.minimum(t + 1, n_tiles - 1), 0, 0)),
                  smem(lambda t, te: (t, 0, 0)),
                  pl.BlockSpec(memory_space=pl.ANY),
                  pl.BlockSpec((None, d, two_f), lambda t, te: (te[t], 0, 0)),
                  pl.BlockSpec((None, two_f // 2, d), lambda t, te: (te[t], 0, 0))],
        out_specs=pl.BlockSpec(memory_space=pl.ANY),
        scratch_shapes=[pltpu.VMEM((2, tm // SUBLANES, SUBLANES, half), jnp.uint32),
                        pltpu.VMEM((2, tm // SUBLANES, SUBLANES, half), jnp.uint32),
                        pltpu.SemaphoreType.DMA((2,)), pltpu.SemaphoreType.DMA((2,))])
    return pl.pallas_call(
        functools.partial(_expert_kernel, tm=tm, d_expert=two_f // 2),
        grid_spec=grid_spec,
        out_shape=jax.ShapeDtypeStruct((n_tiles * tm, half), jnp.uint32),
        compiler_params=_params("arbitrary"),
        name="moe_experts",
    )(tile_expert, src, src, dst, hp, wgu, wd)


def _combine_kernel(x_ref, ya_ref, yb_ref, w_ref, o_ref):
    w = w_ref[...]
    half = ya_ref.shape[1]
    for c, (a, b) in enumerate(zip(_unpack_halves(ya_ref[...]), _unpack_halves(yb_ref[...]))):
        cols = slice(c * half, (c + 1) * half)
        o_ref[:, cols] = x_ref[:, cols] + w[:, 0:1] * a + w[:, 1:2] * b


def _combine_norm_kernel(x_ref, ya_ref, yb_ref, w_ref, g_ref, xo_ref, o_ref, o8_ref, s_ref):
    _combine_kernel(x_ref, ya_ref, yb_ref, w_ref, xo_ref)
    _rmsnorm_kernel(xo_ref, g_ref, o_ref, o8_ref, s_ref)


def _moe_combine(x, y, wts, next_gain=None):
    n, d = x.shape
    tm = _tile(256, n)
    nt = n // tm
    row = pl.BlockSpec((tm, d), lambda i: (i, 0))
    in_specs = [row, pl.BlockSpec((tm, d // 2), lambda i: (i, 0)),
                pl.BlockSpec((tm, d // 2), lambda i: (nt + i, 0)), pl.BlockSpec((tm, 2), lambda i: (i, 0))]
    if next_gain is None:
        return pl.pallas_call(
            _combine_kernel, grid=(nt,), in_specs=in_specs, out_specs=row,
            out_shape=jax.ShapeDtypeStruct((n, d), F32), compiler_params=_params("parallel"), name="moe_combine",
        )(x, y, y, wts)
    return pl.pallas_call(
        _combine_norm_kernel,
        grid=(nt,),
        in_specs=in_specs + [pl.BlockSpec((1, d), lambda i: (0, 0))],
        out_specs=[row, row, row, pl.BlockSpec((tm, LANES), lambda i: (i, 0))],
        out_shape=[jax.ShapeDtypeStruct((n, d), F32), jax.ShapeDtypeStruct((n, d), BF16),
                   jax.ShapeDtypeStruct((n, d), FP8), jax.ShapeDtypeStruct((n, LANES), F32)],
        compiler_params=_params("parallel"),
        name="moe_combine_norm",
    )(x, y, y, wts, next_gain.reshape(1, d))


def _rope_tables(seq):
    half = HEAD_DIM // 2
    inv_freq = jnp.power(ROPE_THETA, -jnp.arange(half, dtype=F32) / half)
    ang = jnp.arange(seq, dtype=F32)[:, None] * inv_freq[None, :]
    cos, sin = jnp.cos(ang), jnp.sin(ang)
    return jnp.concatenate([cos, cos], axis=-1), jnp.concatenate([-sin, sin], axis=-1)


def kernel(x, attn_norm, w_in, fox_forget_bias, fox_q_norm, fox_k_norm, ch_q_norm, ch_k_norm, ch_rel_bias,
           diff_q_norm, diff_k_norm, diff_lambda, diff_subln, w_branch, w_out, ffn_norm, router_w, router_bias,
           w_gate, w_up, w_down):
    batch, seq, d_model = x.shape
    depth = w_in.shape[0]
    width = d_model // N_BRANCHES
    fox_heads = width // HEAD_DIM
    qkv_cols = 3 * width
    n_experts, _, d_expert = w_gate.shape[1:]
    n = batch * seq
    scale = HEAD_DIM ** -0.5
    assert fox_heads <= LANES and width % (2 * HEAD_DIM) == 0 and seq % CHUNK == 0

    cos_t, sin_t = _rope_tables(seq)
    ones = jnp.ones((HEAD_DIM,), F32)
    res_spec = lambda tm, tn: pl.BlockSpec((tm, tn), lambda i, j: (i, j))
    xf = x.reshape(n, d_model)

    for layer in range(depth):
        w_proj, col_amax = _stage_in_proj(w_in, layer, qkv_cols, fox_heads)
        w_qkv = w_proj
        w_f = _cast_lane_group(w_in, layer, qkv_cols, fox_heads)
        gate_col0 = 4 * qkv_cols
        w_g8 = _quantize_cols(w_proj, col_amax, gate_col0, N_BRANCHES * d_model)
        g_scale = col_amax[:, gate_col0:] * (1.0 / FP8_MAX)
        f_bias = jnp.pad(fox_forget_bias[layer], (0, LANES - fox_heads)).reshape(1, LANES)
        gains = jnp.stack([
            fox_q_norm[layer] * scale, fox_k_norm[layer], ones,
            ones * scale, ones, ones,
            ch_q_norm[layer] * scale, ch_k_norm[layer], ones,
            diff_q_norm[layer] * scale, diff_k_norm[layer], ones,
            ones, ones, ones, ones])

        if layer == 0:
            h, h8, h_scale = _rmsnorm(xf, attn_norm[layer])
        qkv = _qkv_proj(h, w_qkv, gains, cos_t, sin_t, seq, width)
        gates = _matmul(h8, w_g8, _epi_scaled_sigmoid, BF16, extras=(h_scale, g_scale),
                        extra_specs=(lambda tm, tn: pl.BlockSpec((tm, LANES), lambda i, j: (i, 0)),
                                     lambda tm, tn: pl.BlockSpec((1, tn), lambda i, j: (0, j))),
                        name="gate_proj")
        cum = _forget_cumsum(h, w_f, f_bias, batch, seq)
        cum_t = cum[:, :fox_heads].T

        lam_init = 0.8 - 0.6 * math.exp(-0.3 * layer)
        branches = (
            _fox_attention(qkv, cum, cum_t, batch, seq, width),
            _sb_attention(qkv, batch, seq, width),
            _band_attention(qkv, ch_rel_bias[layer], batch, seq, width),
            _diff_attention(qkv, diff_lambda[layer], diff_subln[layer], lam_init, batch, seq, width),
        )
        merged = _gated_merge(branches, _cast_layer(w_branch, layer).reshape(w_branch.shape[1:]), gates)
        xf = _matmul(merged, _cast_layer(w_out, layer), _epi_residual, F32, extras=(xf,), extra_specs=(res_spec,),
                     tn=512, name="out_proj")

        hp, idx, wts = _router(xf, ffn_norm[layer], router_w, router_bias)
        wgu = _cast_gate_up(w_gate, w_up, layer)
        y = _moe_experts(hp, idx, wgu, _cast_layer(w_down, layer).reshape(w_down.shape[1:]))
        if layer + 1 < depth:
            xf, h, h8, h_scale = _moe_combine(xf, y, wts.T, attn_norm[layer + 1])
        else:
            xf = _moe_combine(xf, y, wts.T)
    return xf.reshape(batch, seq, d_model)
```
